```python
import math
import jax, jax.numpy as jnp
from jax import lax
import numpy as np

D_MODEL = 2048
BATCH = 4
SEQ = 2048
DEPTH = 1
DEC_BATCH = 128
DEC_SEQ = 8
PAST_LEN = 16384
PAGE_SIZE = 128

N_META = 16
SSD_HEADS = 32
SSD_HEAD_DIM = 64
SSD_D_INNER = SSD_HEADS * SSD_HEAD_DIM
SSD_GROUPS = 2
SSD_STATE = 128
SSD_CONV = 4
SSD_CONV_DIM = SSD_D_INNER + 2 * SSD_GROUPS * SSD_STATE
ML_HEADS = 8
ML_QK_DIM = 128
ML_V_DIM = 256
ML_D_INNER = ML_HEADS * ML_V_DIM
MIX_WIDTH = SSD_D_INNER + ML_D_INNER
D_FF = 5632
FFN_CONV = 3
CHUNK = 128
EPS = 1e-6

IN_SIZES = [SSD_D_INNER,
            SSD_CONV_DIM,
            SSD_HEADS,
            ML_HEADS * ML_QK_DIM,
            ML_HEADS * ML_QK_DIM,
            ML_D_INNER,
            ML_HEADS,
            ML_HEADS,
            ML_D_INNER]
IN_COLS = int(sum(IN_SIZES))
IN_SPLITS = [int(s) for s in np.cumsum(IN_SIZES)[:-1]]

kernel_name = "hymba_ssd_mlstm_convffn_step"


def rmsnorm(x, w):
    xf = x.astype(jnp.float32)
    r = lax.rsqrt(jnp.mean(xf * xf, axis=-1, keepdims=True) + EPS)
    return (xf * r).astype(x.dtype) * w


def causal_dwconv(x, buf, w, b):
    K = w.shape[0]
    T = x.shape[1]
    xp = jnp.concatenate([buf.astype(x.dtype), x], axis=1)
    y = b
    for j in range(K):
        y = y + xp[:, j:j + T] * w[j]
    return y, xp[:, xp.shape[1] - (K - 1):]


def to_chunks(a, L):
    Bsz, T = a.shape[0], a.shape[1]
    return jnp.moveaxis(a.reshape(Bsz, T // L, L, *a.shape[2:]), 1, 0)


def from_chunks(a):
    a = jnp.moveaxis(a, 0, 1)
    return a.reshape(a.shape[0], a.shape[1] * a.shape[2], *a.shape[3:])


def ssd_scan(x, dt, A, Bm, Cm, S0, L):
    Bsz, T, H, P = x.shape
    G, N = Bm.shape[2], Bm.shape[3]
    E = H // G
    f32 = jnp.float32
    xs = to_chunks(x.astype(f32).reshape(Bsz, T, G, E, P), L)
    dts = to_chunks(dt.astype(f32).reshape(Bsz, T, G, E), L)
    Bs = to_chunks(Bm.astype(f32), L)
    Cs = to_chunks(Cm.astype(f32), L)
    Ag = A.astype(f32).reshape(G, E)
    causal = jnp.tril(jnp.ones((L, L), dtype=bool))

    def step(S, inp):
        xc, dtc, Bc, Cc = inp
        cum = jnp.cumsum(dtc * Ag, axis=1)
        seg = cum[:, :, None] - cum[:, None, :]
        decay = jnp.exp(jnp.where(causal[None, :, :, None, None], seg, -jnp.inf))
        CB = jnp.einsum('btgn,bsgn->btsg', Cc, Bc)
        y = jnp.einsum('btsg,btsge,bsge,bsgep->btgep', CB, decay, dtc, xc)
        y = y + jnp.einsum('btgn,bgepn,btge->btgep', Cc, S, jnp.exp(cum))
        tail = jnp.exp(cum[:, -1:] - cum) * dtc
        S = S * jnp.exp(cum[:, -1])[..., None, None] + jnp.einsum('bsgn,bsge,bsgep->bgepn', Bc, tail, xc)
        return S, y

    S, ys = lax.scan(step, S0.astype(f32).reshape(Bsz, G, E, P, N), (xs, dts, Bs, Cs))
    return from_chunks(ys).reshape(Bsz, T, H, P), S.reshape(Bsz, H, P, N)


def mlstm_scan(q, k, v, ig, lf, C0, n0, m0, L):
    f32 = jnp.float32
    qs, ks, vs = (to_chunks(a.astype(f32), L) for a in (q, k, v))
    is_, fs = to_chunks(ig, L), to_chunks(lf, L)
    causal = jnp.tril(jnp.ones((L, L), dtype=bool))

    def step(carry, inp):
        Cp, npv, mp = carry
        qc, kc, vc, ic, fc = inp
        F = jnp.cumsum(fc, axis=1)
        Dm = F[:, :, None] - F[:, None, :] + ic[:, None, :]
        Dm = jnp.where(causal[None, :, :, None], Dm, -jnp.inf)
        inter = F + mp[:, None]
        m_t = jnp.maximum(jnp.max(Dm, axis=2), inter)
        W = jnp.exp(Dm - m_t[:, :, None]) * jnp.einsum('bthd,bshd->btsh', qc, kc)
        wi = jnp.exp(inter - m_t)
        num = jnp.einsum('btsh,bshv->bthv', W, vc) + wi[..., None] * jnp.einsum('bthd,bhdv->bthv', qc, Cp)
        den = jnp.sum(W, axis=2) + wi * jnp.einsum('bthd,bhd->bth', qc, npv)
        h = num / jnp.maximum(jnp.abs(den), jnp.exp(-m_t))[..., None]
        FL = F[:, -1]
        lw = FL[:, None] - F + ic
        m_new = jnp.maximum(FL + mp, jnp.max(lw, axis=1))
        sc = jnp.exp(lw - m_new[:, None])
        dec = jnp.exp(FL + mp - m_new)
        C_new = dec[..., None, None] * Cp + jnp.einsum('bsh,bshd,bshv->bhdv', sc, kc, vc)
        n_new = dec[..., None] * npv + jnp.einsum('bsh,bshd->bhd', sc, kc)
        return (C_new, n_new, m_new), h

    (C, n, m), hs = lax.scan(step, (C0.astype(f32), n0.astype(f32), m0.astype(f32)), (qs, ks, vs, is_, fs))
    return from_chunks(hs), C, n, m


def hybrid_layer(h, seg_lens, states, params):
    conv_buf, S, Cst, nst, mst, ffn_buf = states
    (norm1_w, w_in, ssd_conv_w, ssd_conv_b, ssd_dt_bias, ssd_A_log, ssd_D, ssd_norm_w,
     ml_i_bias, ml_f_bias, ml_norm_w, w_out, norm2_w, w_up, ffn_conv_w, ffn_conv_b, w_down) = params
    Bsz, T, _ = h.shape
    f32 = jnp.float32
    u = rmsnorm(h, norm1_w) @ w_in
    z, xBC, dt_raw, q, k, v, i_raw, f_raw, o_raw = jnp.split(u, IN_SPLITS, axis=-1)
    xBC, conv_new = causal_dwconv(xBC, conv_buf, ssd_conv_w, ssd_conv_b)
    xBC = jax.nn.silu(xBC)
    xs, Bm, Cm = jnp.split(xBC, [SSD_D_INNER, SSD_D_INNER + SSD_GROUPS * SSD_STATE], axis=-1)
    xs = xs.reshape(Bsz, T, SSD_HEADS, SSD_HEAD_DIM)
    Bm = Bm.reshape(Bsz, T, SSD_GROUPS, SSD_STATE)
    Cm = Cm.reshape(Bsz, T, SSD_GROUPS, SSD_STATE)
    dt = jax.nn.softplus(dt_raw.astype(f32) + ssd_dt_bias.astype(f32))
    A = -jnp.exp(ssd_A_log.astype(f32))
    q = q.reshape(Bsz, T, ML_HEADS, ML_QK_DIM)
    k = k.reshape(Bsz, T, ML_HEADS, ML_QK_DIM) * (ML_QK_DIM ** -0.5)
    v = v.reshape(Bsz, T, ML_HEADS, ML_V_DIM)
    ig = i_raw.astype(f32) + ml_i_bias.astype(f32)
    lf = jax.nn.log_sigmoid(f_raw.astype(f32) + ml_f_bias.astype(f32))
    y_ssd_parts, h_ml_parts = [], []
    start = 0
    for Lseg in seg_lens:
        sl = slice(start, start + Lseg)
        L = math.gcd(Lseg, CHUNK)
        y_seg, S = ssd_scan(xs[:, sl], dt[:, sl], A, Bm[:, sl], Cm[:, sl], S, L)
        h_seg, Cst, nst, mst = mlstm_scan(q[:, sl], k[:, sl], v[:, sl], ig[:, sl], lf[:, sl], Cst, nst, mst, L)
        y_ssd_parts.append(y_seg)
        h_ml_parts.append(h_seg)
        start += Lseg
    y_ssd = jnp.concatenate(y_ssd_parts, axis=1)
    y_ssd = (y_ssd + ssd_D.astype(f32)[:, None] * xs.astype(f32)).astype(h.dtype)
    y_ssd = rmsnorm(y_ssd.reshape(Bsz, T, SSD_D_INNER) * jax.nn.silu(z), ssd_norm_w)
    h_ml = jnp.concatenate(h_ml_parts, axis=1).astype(h.dtype)
    h_ml = rmsnorm(h_ml, ml_norm_w.reshape(ML_HEADS, ML_V_DIM)).reshape(Bsz, T, ML_D_INNER)
    y_ml = jax.nn.sigmoid(o_raw) * h_ml
    h = h + jnp.concatenate([y_ssd, y_ml], axis=-1) @ w_out
    up = rmsnorm(h, norm2_w) @ w_up
    up, ffn_new = causal_dwconv(up, ffn_buf, ffn_conv_w, ffn_conv_b)
    gate, val = jnp.split(up, 2, axis=-1)
    h = h + (jax.nn.silu(gate) * val) @ w_down
    new_states = (conv_new, S.astype(conv_new.dtype), Cst, nst, mst, ffn_new)
    return h, new_states


def setup_inputs(seed: int = 0) -> dict:
    key = jax.random.key(seed)
    ks = jax.random.split(key, 32)
    f32 = jnp.float32
    nrm = lambda kk, shape, s: jax.random.normal(kk, shape, f32) * s
    dt0 = jnp.exp(jax.random.uniform(ks[16], (DEPTH, SSD_HEADS), f32, math.log(1e-3), math.log(1e-1)))
    return {
        "x_prompt": nrm(ks[0], (BATCH, SEQ, D_MODEL), 1.0),
        "x_sample": nrm(ks[1], (DEC_BATCH, DEC_SEQ, D_MODEL), 1.0),
        "state_ssd_conv": nrm(ks[2], (DEPTH, DEC_BATCH, SSD_CONV - 1, SSD_CONV_DIM), 1.0),
        "state_ssd": nrm(ks[3], (DEPTH, DEC_BATCH, SSD_HEADS, SSD_HEAD_DIM, SSD_STATE), 0.1),
        "state_mlstm_C": nrm(ks[4], (DEPTH, DEC_BATCH, ML_HEADS, ML_QK_DIM, ML_V_DIM), 0.1),
        "state_mlstm_n": jnp.abs(nrm(ks[5], (DEPTH, DEC_BATCH, ML_HEADS, ML_QK_DIM), 0.5)),
        "state_mlstm_m": nrm(ks[6], (DEPTH, DEC_BATCH, ML_HEADS), 0.5),
        "state_ffn_conv": nrm(ks[7], (DEPTH, DEC_BATCH, FFN_CONV - 1, 2 * D_FF), 1.0),
        "meta_tokens": nrm(ks[8], (N_META, D_MODEL), 1.0),
        "norm1_w": 1.0 + nrm(ks[9], (DEPTH, D_MODEL), 0.02),
        "w_in": nrm(ks[10], (DEPTH, D_MODEL, IN_COLS), D_MODEL ** -0.5),
        "ssd_conv_w": nrm(ks[11], (DEPTH, SSD_CONV, SSD_CONV_DIM), SSD_CONV ** -0.5),
        "ssd_conv_b": nrm(ks[12], (DEPTH, SSD_CONV_DIM), 0.02),
        "ssd_dt_bias": dt0 + jnp.log(-jnp.expm1(-dt0)),
        "ssd_A_log": jnp.log(jax.random.uniform(ks[13], (DEPTH, SSD_HEADS), f32, 1.0, 16.0)),
        "ssd_D": 1.0 + nrm(ks[14], (DEPTH, SSD_HEADS), 0.02),
        "ssd_norm_w": 1.0 + nrm(ks[15], (DEPTH, SSD_D_INNER), 0.02),
        "ml_i_bias": nrm(ks[17], (DEPTH, ML_HEADS), 0.1) - 1.0,
        "ml_f_bias": jax.random.uniform(ks[18], (DEPTH, ML_HEADS), f32, 3.0, 6.0),
        "ml_norm_w": 1.0 + nrm(ks[19], (DEPTH, ML_D_INNER), 0.02),
        "w_out": nrm(ks[20], (DEPTH, MIX_WIDTH, D_MODEL), MIX_WIDTH ** -0.5),
        "norm2_w": 1.0 + nrm(ks[21], (DEPTH, D_MODEL), 0.02),
        "w_up": nrm(ks[22], (DEPTH, D_MODEL, 2 * D_FF), D_MODEL ** -0.5),
        "ffn_conv_w": nrm(ks[23], (DEPTH, FFN_CONV, 2 * D_FF), FFN_CONV ** -0.5),
        "ffn_conv_b": nrm(ks[24], (DEPTH, 2 * D_FF), 0.02),
        "w_down": nrm(ks[25], (DEPTH, D_FF, D_MODEL), D_FF ** -0.5),
        "final_norm_w": 1.0 + nrm(ks[26], (D_MODEL,), 0.02),
    }


def reference(x_prompt, x_sample, state_ssd_conv, state_ssd, state_mlstm_C, state_mlstm_n,
              state_mlstm_m, state_ffn_conv, meta_tokens, norm1_w, w_in, ssd_conv_w, ssd_conv_b,
              ssd_dt_bias, ssd_A_log, ssd_D, ssd_norm_w, ml_i_bias, ml_f_bias, ml_norm_w, w_out,
              norm2_w, w_up, ffn_conv_w, ffn_conv_b, w_down, final_norm_w):
    Bp = x_prompt.shape[0]
    f32 = jnp.float32
    meta = jnp.broadcast_to(meta_tokens[None].astype(x_prompt.dtype), (Bp, N_META, D_MODEL))
    hp = jnp.concatenate([meta, x_prompt], axis=1)
    hs = x_sample
    p_lists = [[] for _ in range(6)]
    s_lists = [[] for _ in range(6)]
    for l in range(DEPTH):
        params = (norm1_w[l], w_in[l], ssd_conv_w[l], ssd_conv_b[l], ssd_dt_bias[l], ssd_A_log[l],
                  ssd_D[l], ssd_norm_w[l], ml_i_bias[l], ml_f_bias[l], ml_norm_w[l], w_out[l],
                  norm2_w[l], w_up[l], ffn_conv_w[l], ffn_conv_b[l], w_down[l])
        p_init = (jnp.zeros((Bp, SSD_CONV - 1, SSD_CONV_DIM), hp.dtype),
                  jnp.zeros((Bp, SSD_HEADS, SSD_HEAD_DIM, SSD_STATE), f32),
                  jnp.zeros((Bp, ML_HEADS, ML_QK_DIM, ML_V_DIM), f32),
                  jnp.zeros((Bp, ML_HEADS, ML_QK_DIM), f32),
                  jnp.zeros((Bp, ML_HEADS), f32),
                  jnp.zeros((Bp, FFN_CONV - 1, 2 * D_FF), hp.dtype))
        s_init = (state_ssd_conv[l], state_ssd[l], state_mlstm_C[l], state_mlstm_n[l],
                  state_mlstm_m[l], state_ffn_conv[l])
        hp, p_new = hybrid_layer(hp, (N_META, hp.shape[1] - N_META), p_init, params)
        hs, s_new = hybrid_layer(hs, (hs.shape[1],), s_init, params)
        for j in range(6):
            p_lists[j].append(p_new[j])
            s_lists[j].append(s_new[j])
    y_prompt = rmsnorm(hp, final_norm_w)[:, N_META:]
    y_sample = rmsnorm(hs, final_norm_w)
    p_st = [jnp.stack(a, axis=0) for a in p_lists]
    s_st = [jnp.stack(a, axis=0) for a in s_lists]
    return (y_prompt, y_sample,
            p_st[0], p_st[1], p_st[2], p_st[3], p_st[4], p_st[5],
            s_st[0], s_st[1], s_st[2], s_st[3], s_st[4], s_st[5])
```

```python
import functools

import jax
import jax.numpy as jnp
from jax import lax
from jax.experimental import pallas as pl
from jax.experimental.pallas import tpu as pltpu

f32 = jnp.float32
bf16 = jnp.bfloat16

D_MODEL = 2048
N_META = 16
SSD_HEADS = 32
SSD_HEAD_DIM = 64
SSD_D_INNER = SSD_HEADS * SSD_HEAD_DIM
SSD_GROUPS = 2
SSD_STATE = 128
SSD_CONV = 4
SSD_BC = 2 * SSD_GROUPS * SSD_STATE
SSD_CONV_DIM = SSD_D_INNER + SSD_BC
ML_HEADS = 8
ML_QK_DIM = 128
ML_V_DIM = 256
ML_QK_INNER = ML_HEADS * ML_QK_DIM
ML_D_INNER = ML_HEADS * ML_V_DIM
D_FF = 5632
FFN_CONV = 3
EPS = 1e-6
NEG = -1e30

LANES = 128
SUBLANES = 8
VMEM_LIMIT = 56 * 1024 * 1024

U_Z, U_X, U_V, U_O = 0, 1, 2, 3
U_Q, U_K = 8, 9
U_BC = 20
U_MAIN = 4 * 2048 + 2 * 1024 + 512
GATE_LANE = 32
U_SMALL = 2 * LANES


def _dot(a, b):
    return jnp.dot(a.astype(bf16), b.astype(bf16), preferred_element_type=f32)


def _dot_nt(a, b):
    return lax.dot_general(a.astype(bf16), b.astype(bf16), (((1,), (1,)), ((), ())),
                           preferred_element_type=f32)


def _dot_tn(a, b):
    return lax.dot_general(a.astype(bf16), b.astype(bf16), (((0,), (0,)), ((), ())),
                           preferred_element_type=f32)


def _split3(a):
    hi = a.astype(bf16)
    r1 = a - hi.astype(f32)
    mid = r1.astype(bf16)
    lo = (r1 - mid.astype(f32)).astype(bf16)
    return hi, mid, lo


def _sel_right(a, e01):
    hi, mid, lo = _split3(a)
    d = lambda p: jnp.dot(p, e01, preferred_element_type=f32)
    return d(hi) + d(mid) + d(lo)


def _sel_left(e01, a):
    hi, mid, lo = _split3(a)
    d = lambda p: jnp.dot(e01, p, preferred_element_type=f32)
    return d(hi) + d(mid) + d(lo)


def _transpose_exact(a, eye):
    hi, mid, lo = _split3(a)
    d = lambda p: lax.dot_general(eye, p, (((1,), (1,)), ((), ())), preferred_element_type=f32)
    return d(hi) + d(mid) + d(lo)


def _iota2(shape, axis):
    return lax.broadcasted_iota(jnp.int32, shape, axis)


def _as01(m):
    return jnp.where(m, 1.0, 0.0).astype(bf16)


def _eye():
    return _as01(_iota2((LANES, LANES), 0) == _iota2((LANES, LANES), 1))


def _seq_masks(L, Ls):
    t = _iota2((L, L), 0)
    s = _iota2((L, L), 1)
    shift = Ls.bit_length() - 1
    same = (t >> shift) == (s >> shift)
    causal = same & (s <= t)
    causal_t = same & (t <= s)
    last = s == (t | (Ls - 1))
    return causal, _as01(causal), _as01(causal_t), _as01(last)


def _sigmoid(x):
    return 1.0 / (1.0 + jnp.exp(-x))


def _silu(x):
    return x * _sigmoid(x)


def _softplus(x):
    return jnp.maximum(x, 0.0) + jnp.log1p(jnp.exp(-jnp.abs(x)))


def _rms(x, w):
    r = lax.rsqrt(jnp.mean(x * x, axis=-1, keepdims=True) + EPS)
    return (x * r) * w


def _causal_conv(x, hist, w_ref, b, Ls):
    L, C = x.shape
    K = w_ref.shape[0]
    y = b + x * w_ref[K - 1:K, :]
    for s in range(1, K):
        if hist.shape[0] == L:
            r = _iota2((L, C), 0) & (Ls - 1)
            z = jnp.where(r >= Ls - s, hist, x)
        else:
            r = _iota2((SUBLANES, C), 0)
            zt = jnp.where(r >= SUBLANES - s, hist, x[L - SUBLANES:, :])
            z = zt if L == SUBLANES else jnp.concatenate([x[:L - SUBLANES, :], zt], axis=0)
        y = y + pltpu.roll(z, s, 0) * w_ref[K - 1 - s:K - s, :]
    return y


def _params(n_axes):
    return pltpu.CompilerParams(dimension_semantics=("arbitrary",) * n_axes,
                                vmem_limit_bytes=VMEM_LIMIT)


def _inproj_kernel(x_ref, nw_ref, w_ref, ws_ref, u_ref, us_ref, xn_ref):
    @pl.when(pl.program_id(1) == 0)
    def _():
        xn = _rms(x_ref[...], nw_ref[...]).astype(bf16)
        xn_ref[...] = xn
        us_ref[...] = jnp.dot(xn, ws_ref[...], preferred_element_type=f32)

    u_ref[...] = jnp.dot(xn_ref[...], w_ref[...], preferred_element_type=f32)


def _inproj(x, nw, w_main, w_small, tm):
    M = x.shape[0]
    tn = 1536
    return pl.pallas_call(
        _inproj_kernel,
        grid=(M // tm, U_MAIN // tn),
        in_specs=[pl.BlockSpec((tm, D_MODEL), lambda i, j: (i, 0)),
                  pl.BlockSpec((1, D_MODEL), lambda i, j: (0, 0)),
                  pl.BlockSpec((D_MODEL, tn), lambda i, j: (0, j)),
                  pl.BlockSpec((D_MODEL, U_SMALL), lambda i, j: (0, 0))],
        out_specs=[pl.BlockSpec((tm, tn), lambda i, j: (i, j)),
                   pl.BlockSpec((tm, U_SMALL), lambda i, j: (i, 0))],
        out_shape=[jax.ShapeDtypeStruct((M, U_MAIN), f32),
                   jax.ShapeDtypeStruct((M, U_SMALL), f32)],
        scratch_shapes=[pltpu.VMEM((tm, D_MODEL), bf16)],
        compiler_params=_params(2),
        name="inproj",
    )(x, nw, w_main, w_small)


def _ssd_kernel(z_ref, x_ref, bc_ref, sm_ref, hx_ref, hbc_ref, cwx_ref, cwbc_ref, cbx_ref, cbbc_ref,
                dtb_ref, alog_ref, dexp_ref, nw_ref, ehp_ref, ehpt_ref, s0_ref,
                y_ref, s_ref, tailx_ref, tailbc_ref, yz_ref, *, L, Ls, n_chunks):
    nseq = L // Ls
    c = pl.program_id(1)
    xpre = x_ref[...]
    bcpre = bc_ref[...]
    if nseq > 1:
        hx, hbc = hx_ref[...], hbc_ref[...]
    elif n_chunks == 1:
        hx, hbc = hx_ref[0], hbc_ref[0]
    else:
        first = c == 0
        hx = jnp.where(first, hx_ref[0], tailx_ref[...])
        hbc = jnp.where(first, hbc_ref[0], tailbc_ref[...])
    xc = _silu(_causal_conv(xpre, hx, cwx_ref, cbx_ref[...], Ls))
    bcc = _silu(_causal_conv(bcpre, hbc, cwbc_ref, cbbc_ref[...], Ls))
    if n_chunks > 1:
        tailx_ref[...] = xpre[L - SUBLANES:, :]
        tailbc_ref[...] = bcpre[L - SUBLANES:, :]

    lane = _iota2((L, LANES), 1)
    dt = jnp.where(lane < SSD_HEADS, _softplus(sm_ref[:, :LANES] + dtb_ref[...]), 0.0)
    dta = dt * (-jnp.exp(alog_ref[...]))

    causal, tri, tri_t, last = _seq_masks(L, Ls)
    eye = _eye()
    cum = _sel_left(tri, dta)
    cum_t = _sel_right(_transpose_exact(dta, eye), tri_t)
    cum_last = _sel_left(last, cum)
    ehp = ehp_ref[...]
    dt_e = _sel_right(dt, ehp)
    cum_e = _sel_right(cum, ehp)
    cl_e = _sel_right(cum_last, ehp)
    xdt = xc * dt_e
    xdtw = xdt * jnp.exp(cl_e - cum_e)
    ecum = jnp.exp(cum_e)

    if n_chunks > 1:
        @pl.when(c == 0)
        def _():
            s_ref[...] = s0_ref[...]
        sprev_ref = s_ref
    else:
        sprev_ref = s0_ref

    GE = SSD_D_INNER // SSD_GROUPS
    HPG = SSD_HEADS // SSD_GROUPS
    shift = Ls.bit_length() - 1
    seq_of_row = _iota2((L, 1), 0) >> shift
    lane_lo = lane < SSD_HEAD_DIM
    ssq = jnp.zeros((L, 1), f32)
    for g in range(SSD_GROUPS):
        bm = bcc[:, g * SSD_STATE:(g + 1) * SSD_STATE]
        cm = bcc[:, (SSD_GROUPS + g) * SSD_STATE:(SSD_GROUPS + g + 1) * SSD_STATE]
        cb = _dot_nt(cm, bm)
        ys = None
        for j in range(nseq):
            cmj = cm if nseq == 1 else jnp.where(seq_of_row == j, cm, 0.0)
            t = _dot_nt(cmj, sprev_ref[j, g * GE:(g + 1) * GE, :])
            ys = t if ys is None else ys + t
        for hp in range(HPG // 2):
            col0 = g * GE + hp * LANES
            xpair = xdt[:, col0:col0 + LANES].astype(bf16)
            outs = []
            for e in range(2):
                h = g * HPG + hp * 2 + e
                seg = cum[:, h:h + 1] - cum_t[h:h + 1, :]
                m = jnp.exp(jnp.where(causal, seg, NEG)) * cb
                outs.append(jnp.dot(m.astype(bf16), xpair, preferred_element_type=f32))
            blk = slice(col0, col0 + LANES)
            y = (jnp.where(lane_lo, outs[0], outs[1])
                 + ecum[:, blk] * ys[:, hp * LANES:(hp + 1) * LANES]
                 + dexp_ref[:, blk] * xc[:, blk])
            yz = y * _silu(z_ref[:, blk])
            ssq = ssq + jnp.sum(yz * yz, axis=-1, keepdims=True)
            yz_ref[:, blk] = yz
    r = lax.rsqrt(ssq * (1.0 / SSD_D_INNER) + EPS)
    y_ref[...] = ((yz_ref[...] * r) * nw_ref[...]).astype(bf16)

    ehpt = ehpt_ref[...]
    for j in range(nseq):
        tl = (j + 1) * Ls - 1
        dec = jnp.exp(jnp.broadcast_to(cum_t[:, tl:tl + 1], (LANES, LANES)))
        dec_rows = _sel_left(ehpt, dec)
        for g in range(SSD_GROUPS):
            bm = bcc[:, g * SSD_STATE:(g + 1) * SSD_STATE]
            xw = xdtw[:, g * GE:(g + 1) * GE]
            if nseq > 1:
                xw = jnp.where(seq_of_row == j, xw, 0.0)
            ds = _dot_tn(xw, bm)
            rs = slice(g * GE, (g + 1) * GE)
            s_ref[j, rs, :] = dec_rows[rs, :] * sprev_ref[j, rs, :] + ds


def _ssd(u, us, hx, hbc, s0, wts, *, B, T, L, Ls):
    nseq = L // Ls
    n_chunks = T // Ls if nseq == 1 else 1
    nblk = B // nseq
    rb = lambda i, c: i * n_chunks + c
    if nseq > 1:
        h_specs = [pl.BlockSpec((L, SSD_D_INNER), lambda i, c: (i, 0)),
                   pl.BlockSpec((L, SSD_BC), lambda i, c: (i, 0))]
    else:
        h_specs = [pl.BlockSpec((1, SUBLANES, SSD_D_INNER), lambda i, c: (i, 0, 0)),
                   pl.BlockSpec((1, SUBLANES, SSD_BC), lambda i, c: (i, 0, 0))]
    const = lambda shape: pl.BlockSpec(shape, lambda i, c: (0,) * len(shape))
    kern = functools.partial(_ssd_kernel, L=L, Ls=Ls, n_chunks=n_chunks)
    return pl.pallas_call(
        kern,
        grid=(nblk, n_chunks),
        in_specs=[pl.BlockSpec((L, SSD_D_INNER), lambda i, c: (rb(i, c), U_Z)),
                  pl.BlockSpec((L, SSD_D_INNER), lambda i, c: (rb(i, c), U_X)),
                  pl.BlockSpec((L, SSD_BC), lambda i, c: (rb(i, c), U_BC)),
                  pl.BlockSpec((L, U_SMALL), lambda i, c: (rb(i, c), 0)),
                  *h_specs,
                  const((SSD_CONV, SSD_D_INNER)), const((SSD_CONV, SSD_BC)),
                  const((1, SSD_D_INNER)), const((1, SSD_BC)),
                  const((1, LANES)), const((1, LANES)),
                  const((1, SSD_D_INNER)), const((1, SSD_D_INNER)),
                  const((LANES, SSD_D_INNER)), const((SSD_D_INNER, LANES)),
                  pl.BlockSpec((nseq, SSD_D_INNER, SSD_STATE), lambda i, c: (i, 0, 0))],
        out_specs=[pl.BlockSpec((L, SSD_D_INNER), lambda i, c: (rb(i, c), 0)),
                   pl.BlockSpec((nseq, SSD_D_INNER, SSD_STATE), lambda i, c: (i, 0, 0))],
        out_shape=[jax.ShapeDtypeStruct((B * T, SSD_D_INNER), bf16),
                   jax.ShapeDtypeStruct((B, SSD_D_INNER, SSD_STATE), f32)],
        scratch_shapes=[pltpu.VMEM((SUBLANES, SSD_D_INNER), f32),
                        pltpu.VMEM((SUBLANES, SSD_BC), f32),
                        pltpu.VMEM((L, SSD_D_INNER), f32)],
        compiler_params=_params(2),
        name="ssd",
    )(u, u, u, us, hx, hbc, wts["cw_x"], wts["cw_bc"], wts["cb_x"], wts["cb_bc"],
      wts["dt_bias"], wts["a_log"], wts["d_exp"], wts["ssd_norm_w"], wts["ehp"], wts["ehpt"], s0)


def _mlstm_kernel(q_ref, k_ref, v_ref, o_ref, sm_ref, ib_ref, fb_ref, nw_ref, c0_ref, n0_ref, m0_ref,
                  y_ref, c_ref, n_ref, m_ref, *, L, Ls, n_chunks):
    nseq = L // Ls
    c = pl.program_id(1)
    if n_chunks > 1:
        @pl.when(c == 0)
        def _():
            c_ref[...] = c0_ref[...]
            n_ref[...] = n0_ref[...]
            m_ref[...] = m0_ref[...]
        cprev_ref, nprev_ref, mprev_ref = c_ref, n_ref, m_ref
    else:
        cprev_ref, nprev_ref, mprev_ref = c0_ref, n0_ref, m0_ref

    per_tok = lambda a: jnp.broadcast_to(a, (nseq, Ls, LANES)).reshape(L, LANES)
    ig = sm_ref[:, :LANES] + ib_ref[...]
    fraw = sm_ref[:, LANES:] + fb_ref[...]
    lf = -_softplus(-fraw)
    causal, tri, _, last = _seq_masks(L, Ls)
    eye = _eye()
    F = _sel_left(tri, lf)
    FL = _sel_left(last, F)
    mp = per_tok(mprev_ref[:, 0:1, :])
    r_t = _transpose_exact(ig - F, eye)
    inter = F + mp
    lw = FL - F + ig
    segmax = jnp.max(lw.reshape(nseq, Ls, LANES), axis=1, keepdims=True)
    m_new = jnp.maximum(FL + mp, per_tok(segmax))
    sc = jnp.exp(lw - m_new)
    dec = jnp.exp(FL + mp - m_new)
    m_out = m_new.reshape(nseq, Ls, LANES)[:, 0:SUBLANES, :]

    shift = Ls.bit_length() - 1
    seq_of_row = _iota2((L, 1), 0) >> shift
    kscale = ML_QK_DIM ** -0.5
    for h in range(ML_HEADS):
        gl = GATE_LANE + h
        qh = q_ref[:, h * ML_QK_DIM:(h + 1) * ML_QK_DIM]
        kh = k_ref[:, h * ML_QK_DIM:(h + 1) * ML_QK_DIM] * kscale
        vh = v_ref[:, h * ML_V_DIM:(h + 1) * ML_V_DIM]
        dm = jnp.where(causal, F[:, gl:gl + 1] + r_t[gl:gl + 1, :], NEG)
        inter_h = inter[:, gl:gl + 1]
        m_t = jnp.maximum(jnp.max(dm, axis=-1, keepdims=True), inter_h)
        w = jnp.exp(dm - m_t) * _dot_nt(qh, kh)
        wi = jnp.exp(inter_h - m_t)
        crows = slice(h * ML_QK_DIM, (h + 1) * ML_QK_DIM)
        qc = None
        for j in range(nseq):
            qj = qh if nseq == 1 else jnp.where(seq_of_row == j, qh, 0.0)
            t = _dot(qj, cprev_ref[j, crows, :])
            qc = t if qc is None else qc + t
        n_tok = jnp.broadcast_to(nprev_ref[:, h:h + 1, :], (nseq, Ls, ML_QK_DIM)).reshape(L, ML_QK_DIM)
        qn = jnp.sum(qh * n_tok, axis=-1, keepdims=True)
        num = _dot(w, vh) + wi * qc
        den = jnp.sum(w, axis=-1, keepdims=True) + wi * qn
        hh = num / jnp.maximum(jnp.abs(den), jnp.exp(-m_t))
        hn = _rms(hh, nw_ref[:, h * ML_V_DIM:(h + 1) * ML_V_DIM])
        og = _sigmoid(o_ref[:, h * ML_V_DIM:(h + 1) * ML_V_DIM])
        y_ref[:, h * ML_V_DIM:(h + 1) * ML_V_DIM] = (og * hn).astype(bf16)
        ksc = kh * sc[:, gl:gl + 1]
        for j in range(nseq):
            r0 = j * Ls
            dj = dec[r0:r0 + 1, gl:gl + 1]
            kj = ksc if nseq == 1 else jnp.where(seq_of_row == j, ksc, 0.0)
            c_ref[j, crows, :] = dj * cprev_ref[j, crows, :] + _dot_tn(kj, vh)
            n_ref[j, h:h + 1, :] = (dj * nprev_ref[j, h:h + 1, :]
                                    + jnp.sum(ksc[r0:r0 + Ls, :], axis=0, keepdims=True))
    m_ref[...] = m_out


def _mlstm(u, us, c0, n0, m0, wts, *, B, T, L, Ls):
    nseq = L // Ls
    n_chunks = T // Ls if nseq == 1 else 1
    nblk = B // nseq
    rb = lambda i, c: i * n_chunks + c
    const = lambda shape: pl.BlockSpec(shape, lambda i, c: (0,) * len(shape))
    st = lambda shape: pl.BlockSpec((nseq,) + shape, lambda i, c: (i, 0, 0))
    kern = functools.partial(_mlstm_kernel, L=L, Ls=Ls, n_chunks=n_chunks)
    return pl.pallas_call(
        kern,
        grid=(nblk, n_chunks),
        in_specs=[pl.BlockSpec((L, ML_QK_INNER), lambda i, c: (rb(i, c), U_Q)),
                  pl.BlockSpec((L, ML_QK_INNER), lambda i, c: (rb(i, c), U_K)),
                  pl.BlockSpec((L, ML_D_INNER), lambda i, c: (rb(i, c), U_V)),
                  pl.BlockSpec((L, ML_D_INNER), lambda i, c: (rb(i, c), U_O)),
                  pl.BlockSpec((L, U_SMALL), lambda i, c: (rb(i, c), 0)),
                  const((1, LANES)), const((1, LANES)), const((1, ML_D_INNER)),
                  st((ML_QK_INNER, ML_V_DIM)), st((ML_HEADS, ML_QK_DIM)), st((SUBLANES, LANES))],
        out_specs=[pl.BlockSpec((L, ML_D_INNER), lambda i, c: (rb(i, c), 0)),
                   st((ML_QK_INNER, ML_V_DIM)), st((ML_HEADS, ML_QK_DIM)), st((SUBLANES, LANES))],
        out_shape=[jax.ShapeDtypeStruct((B * T, ML_D_INNER), bf16),
                   jax.ShapeDtypeStruct((B, ML_QK_INNER, ML_V_DIM), f32),
                   jax.ShapeDtypeStruct((B, ML_HEADS, ML_QK_DIM), f32),
                   jax.ShapeDtypeStruct((B, SUBLANES, LANES), f32)],
        compiler_params=_params(2),
        name="mlstm",
    )(u, u, u, u, us, wts["i_bias"], wts["f_bias"], wts["ml_norm_w"], c0, n0, m0)


def _outproj_kernel(ys_ref, ym_ref, ws_ref, wm_ref, h_ref, o_ref):
    o_ref[...] = (h_ref[...]
                  + jnp.dot(ys_ref[...], ws_ref[...], preferred_element_type=f32)
                  + jnp.dot(ym_ref[...], wm_ref[...], preferred_element_type=f32))


def _outproj(ys, ym, w_out, h, tm):
    M = h.shape[0]
    tn = 1024
    return pl.pallas_call(
        _outproj_kernel,
        grid=(M // tm, D_MODEL // tn),
        in_specs=[pl.BlockSpec((tm, SSD_D_INNER), lambda i, j: (i, 0)),
                  pl.BlockSpec((tm, ML_D_INNER), lambda i, j: (i, 0)),
                  pl.BlockSpec((SSD_D_INNER, tn), lambda i, j: (0, j)),
                  pl.BlockSpec((ML_D_INNER, tn), lambda i, j: (1, j)),
                  pl.BlockSpec((tm, tn), lambda i, j: (i, j))],
        out_specs=pl.BlockSpec((tm, tn), lambda i, j: (i, j)),
        out_shape=jax.ShapeDtypeStruct((M, D_MODEL), f32),
        compiler_params=_params(2),
        name="outproj",
    )(ys, ym, w_out, w_out, h)


def _ffn_kernel(h_ref, nw_ref, wg_ref, wv_ref, hg_ref, hv_ref, cwg_ref, cwv_ref, cbg_ref, cbv_ref,
                wd_ref, fw_ref, y_ref, tg_ref, tv_ref, xn_ref, acc_ref, carg_ref, carv_ref,
                *, tm, Ls, blocks_per_seq):
    i = pl.program_id(0)
    j = pl.program_id(1)
    multi = Ls < tm

    @pl.when(j == 0)
    def _():
        xn_ref[...] = _rms(h_ref[...], nw_ref[...]).astype(bf16)
        acc_ref[...] = jnp.zeros_like(acc_ref)

    xn = xn_ref[...]
    up_g = jnp.dot(xn, wg_ref[...], preferred_element_type=f32)
    up_v = jnp.dot(xn, wv_ref[...], preferred_element_type=f32)
    if multi:
        hg, hv = hg_ref[...], hv_ref[...]
        tg_ref[...] = up_g
        tv_ref[...] = up_v
    else:
        if blocks_per_seq == 1:
            hg, hv = hg_ref[0], hv_ref[0]
        else:
            first = (i % blocks_per_seq) == 0
            hg = jnp.where(first, hg_ref[0], carg_ref[j])
            hv = jnp.where(first, hv_ref[0], carv_ref[j])
            carg_ref[j] = up_g[tm - SUBLANES:, :]
            carv_ref[j] = up_v[tm - SUBLANES:, :]
        tg_ref[0] = up_g[tm - SUBLANES:, :]
        tv_ref[0] = up_v[tm - SUBLANES:, :]
    gate = _causal_conv(up_g, hg, cwg_ref, cbg_ref[...], Ls)
    val = _causal_conv(up_v, hv, cwv_ref, cbv_ref[...], Ls)
    act = (_silu(gate) * val).astype(bf16)
    acc_ref[...] += jnp.dot(act, wd_ref[...], preferred_element_type=f32)

    @pl.when(j == pl.num_programs(1) - 1)
    def _():
        y_ref[...] = _rms(h_ref[...] + acc_ref[...], fw_ref[...])


def _ffn(h, hg, hv, wts, *, tm, Ls, T):
    M = h.shape[0]
    tf = 512
    n_ff = D_FF // tf
    multi = Ls < tm
    blocks_per_seq = 1 if multi else T // tm
    if multi:
        h_specs = [pl.BlockSpec((tm, tf), lambda i, j: (i, j)),
                   pl.BlockSpec((tm, tf), lambda i, j: (i, n_ff + j))]
        t_specs = [pl.BlockSpec((tm, tf), lambda i, j: (i, j))] * 2
        t_shape = jax.ShapeDtypeStruct((M, D_FF), f32)
    else:
        h_specs = [pl.BlockSpec((1, SUBLANES, tf), lambda i, j: (i // blocks_per_seq, 0, j)),
                   pl.BlockSpec((1, SUBLANES, tf), lambda i, j: (i // blocks_per_seq, 0, n_ff + j))]
        t_specs = [pl.BlockSpec((1, SUBLANES, tf), lambda i, j: (i, 0, j))] * 2
        t_shape = jax.ShapeDtypeStruct((M // tm, SUBLANES, D_FF), f32)
    const = lambda shape: pl.BlockSpec(shape, lambda i, j: (0,) * len(shape))
    kern = functools.partial(_ffn_kernel, tm=tm, Ls=Ls, blocks_per_seq=blocks_per_seq)
    y, tg, tv = pl.pallas_call(
        kern,
        grid=(M // tm, n_ff),
        in_specs=[pl.BlockSpec((tm, D_MODEL), lambda i, j: (i, 0)),
                  const((1, D_MODEL)),
                  pl.BlockSpec((D_MODEL, tf), lambda i, j: (0, j)),
                  pl.BlockSpec((D_MODEL, tf), lambda i, j: (0, n_ff + j)),
                  *h_specs,
                  pl.BlockSpec((FFN_CONV, tf), lambda i, j: (0, j)),
                  pl.BlockSpec((FFN_CONV, tf), lambda i, j: (0, n_ff + j)),
                  pl.BlockSpec((1, tf), lambda i, j: (0, j)),
                  pl.BlockSpec((1, tf), lambda i, j: (0, n_ff + j)),
                  pl.BlockSpec((tf, D_MODEL), lambda i, j: (j, 0)),
                  const((1, D_MODEL))],
        out_specs=[pl.BlockSpec((tm, D_MODEL), lambda i, j: (i, 0)), *t_specs],
        out_shape=[jax.ShapeDtypeStruct((M, D_MODEL), f32), t_shape, t_shape],
        scratch_shapes=[pltpu.VMEM((tm, D_MODEL), bf16),
                        pltpu.VMEM((tm, D_MODEL), f32),
                        pltpu.VMEM((n_ff, SUBLANES, tf), f32),
                        pltpu.VMEM((n_ff, SUBLANES, tf), f32)],
        compiler_params=_params(2),
        name="ffn",
    )(h, wts["norm2_w"], wts["w_up"], wts["w_up"], hg, hv, wts["ffn_cw"], wts["ffn_cw"],
      wts["ffn_cb"], wts["ffn_cb"], wts["w_down"], wts["final_norm_w"])
    return y, jnp.concatenate([tg, tv], axis=-1)


def _hist_tile(state):
    return jnp.pad(state, ((0, 0), (SUBLANES - state.shape[1], 0), (0, 0)))


def _hist_rows(state, nseq):
    B, _, C = state.shape
    t = _hist_tile(state).reshape(B // nseq, nseq, SUBLANES, C)
    return jnp.roll(t, -1, axis=1).reshape(B * SUBLANES, C)


def _layer(h, states, wts, *, B, T, L, Ls, tm):
    conv0, s0, c0, n0, m0, ffn0 = states
    nseq = L // Ls
    multi = nseq > 1
    u, us = _inproj(h, wts["norm1_w"], wts["w_main"], wts["w_small"], tm)
    cx, cbc = conv0[:, :, :SSD_D_INNER], conv0[:, :, SSD_D_INNER:]
    if multi:
        hx, hbc = _hist_rows(cx, nseq), _hist_rows(cbc, nseq)
    else:
        hx, hbc = _hist_tile(cx), _hist_tile(cbc)
    y_ssd, s_new = _ssd(u, us, hx, hbc, s0.reshape(B, SSD_D_INNER, SSD_STATE), wts, B=B, T=T, L=L, Ls=Ls)
    m_pad = jnp.broadcast_to(
        jnp.pad(m0, ((0, 0), (GATE_LANE, LANES - GATE_LANE - ML_HEADS)))[:, None, :], (B, SUBLANES, LANES))
    y_ml, c_new, n_new, m_new = _mlstm(u, us, c0.reshape(B, ML_QK_INNER, ML_V_DIM), n0, m_pad, wts,
                                       B=B, T=T, L=L, Ls=Ls)
    h1 = _outproj(y_ssd, y_ml, wts["w_out"], h, tm)
    ffn_multi = T < tm
    if ffn_multi:
        hg = _hist_rows(ffn0, tm // T)
        y, up_rows = _ffn(h1, hg, hg, wts, tm=tm, Ls=T, T=T)
        ffn_new = up_rows.reshape(B, T, 2 * D_FF)[:, T - (FFN_CONV - 1):, :]
    else:
        hg = _hist_tile(ffn0)
        y, up_tail = _ffn(h1, hg, hg, wts, tm=tm, Ls=tm, T=T)
        bps = T // tm
        ffn_new = up_tail.reshape(B, bps, SUBLANES, 2 * D_FF)[:, bps - 1, SUBLANES - (FFN_CONV - 1):, :]
    ur = u.reshape(B, T, U_MAIN)[:, T - (SSD_CONV - 1):, :]
    conv_new = jnp.concatenate([ur[:, :, U_X * SSD_D_INNER:(U_X + 1) * SSD_D_INNER],
                                ur[:, :, U_BC * SSD_BC:(U_BC + 1) * SSD_BC]], axis=-1)
    new_states = (conv_new,
                  s_new.reshape(B, SSD_HEADS, SSD_HEAD_DIM, SSD_STATE),
                  c_new.reshape(B, ML_HEADS, ML_QK_DIM, ML_V_DIM),
                  n_new,
                  m_new[:, 0, GATE_LANE:GATE_LANE + ML_HEADS],
                  ffn_new)
    return y, new_states


def _prep_weights(norm1_w, w_in, ssd_conv_w, ssd_conv_b, ssd_dt_bias, ssd_A_log, ssd_D, ssd_norm_w,
                  ml_i_bias, ml_f_bias, ml_norm_w, w_out, norm2_w, w_up, ffn_conv_w, ffn_conv_b, w_down,
                  final_norm_w):
    o = 0
    cols = {}
    for name, width in (("z", SSD_D_INNER), ("x", SSD_D_INNER), ("bc", SSD_BC), ("dt", SSD_HEADS),
                        ("q", ML_QK_INNER), ("k", ML_QK_INNER), ("v", ML_D_INNER), ("i", ML_HEADS),
                        ("f", ML_HEADS), ("o", ML_D_INNER)):
        cols[name] = w_in[:, o:o + width]
        o += width
    w_main = jnp.concatenate([cols[n] for n in ("z", "x", "v", "o", "q", "k", "bc")], axis=1).astype(bf16)
    zpad = lambda n: jnp.zeros((D_MODEL, n), f32)
    w_small = jnp.concatenate([cols["dt"], cols["i"], zpad(LANES - GATE_LANE - ML_HEADS),
                               zpad(GATE_LANE), cols["f"], zpad(LANES - GATE_LANE - ML_HEADS)],
                              axis=1).astype(bf16)
    lane_row = lambda v, off: jnp.pad(v.astype(f32), (off, LANES - off - v.shape[0]))[None, :]
    hp = jnp.arange(SSD_D_INNER) // SSD_HEAD_DIM
    ehp = (jnp.arange(LANES)[:, None] == hp[None, :]).astype(bf16)
    return dict(
        norm1_w=norm1_w[None, :], w_main=w_main, w_small=w_small,
        cw_x=ssd_conv_w[:, :SSD_D_INNER], cw_bc=ssd_conv_w[:, SSD_D_INNER:],
        cb_x=ssd_conv_b[None, :SSD_D_INNER], cb_bc=ssd_conv_b[None, SSD_D_INNER:],
        dt_bias=lane_row(ssd_dt_bias, 0), a_log=lane_row(ssd_A_log, 0),
        d_exp=jnp.repeat(ssd_D.astype(f32), SSD_HEAD_DIM)[None, :], ssd_norm_w=ssd_norm_w[None, :],
        ehp=ehp, ehpt=ehp.T,
        i_bias=lane_row(ml_i_bias, GATE_LANE), f_bias=lane_row(ml_f_bias, GATE_LANE),
        ml_norm_w=ml_norm_w[None, :],
        w_out=w_out.astype(bf16), norm2_w=norm2_w[None, :], w_up=w_up.astype(bf16),
        ffn_cw=ffn_conv_w, ffn_cb=ffn_conv_b[None, :], w_down=w_down.astype(bf16),
        final_norm_w=final_norm_w[None, :])


def kernel(x_prompt, x_sample, state_ssd_conv, state_ssd, state_mlstm_C, state_mlstm_n, state_mlstm_m,
           state_ffn_conv, meta_tokens, norm1_w, w_in, ssd_conv_w, ssd_conv_b, ssd_dt_bias, ssd_A_log,
           ssd_D, ssd_norm_w, ml_i_bias, ml_f_bias, ml_norm_w, w_out, norm2_w, w_up, ffn_conv_w,
           ffn_conv_b, w_down, final_norm_w):
    depth = w_in.shape[0]
    assert depth == 1, "single-layer step"
    Bp, Tp, _ = x_prompt.shape
    Bs, Ts, _ = x_sample.shape
    wts = _prep_weights(norm1_w[0], w_in[0], ssd_conv_w[0], ssd_conv_b[0], ssd_dt_bias[0], ssd_A_log[0],
                        ssd_D[0], ssd_norm_w[0], ml_i_bias[0], ml_f_bias[0], ml_norm_w[0], w_out[0],
                        norm2_w[0], w_up[0], ffn_conv_w[0], ffn_conv_b[0], w_down[0], final_norm_w)
    zero_states = (jnp.zeros((1, SSD_CONV - 1, SSD_CONV_DIM), f32),
                   jnp.zeros((1, SSD_HEADS, SSD_HEAD_DIM, SSD_STATE), f32),
                   jnp.zeros((1, ML_HEADS, ML_QK_DIM, ML_V_DIM), f32),
                   jnp.zeros((1, ML_HEADS, ML_QK_DIM), f32),
                   jnp.zeros((1, ML_HEADS), f32),
                   jnp.zeros((1, FFN_CONV - 1, 2 * D_FF), f32))
    _, meta_states = _layer(meta_tokens.astype(f32), zero_states, wts, B=1, T=N_META, L=N_META, Ls=N_META,
                            tm=N_META)
    p_init = tuple(jnp.broadcast_to(s, (Bp,) + s.shape[1:]) for s in meta_states)
    yp, p_new = _layer(x_prompt.reshape(Bp * Tp, D_MODEL), p_init, wts, B=Bp, T=Tp, L=128, Ls=128, tm=512)
    s_init = (state_ssd_conv[0], state_ssd[0], state_mlstm_C[0], state_mlstm_n[0], state_mlstm_m[0],
              state_ffn_conv[0])
    ys, s_new = _layer(x_sample.reshape(Bs * Ts, D_MODEL), s_init, wts, B=Bs, T=Ts, L=8 * Ts, Ls=Ts, tm=512)
    return (yp.reshape(Bp, Tp, D_MODEL), ys.reshape(Bs, Ts, D_MODEL),
            *(s[None] for s in p_new), *(s[None] for s in s_new))
```

```python
import functools

import jax
import jax.numpy as jnp
from jax import lax
from jax.experimental import pallas as pl
from jax.experimental.pallas import tpu as pltpu

f32 = jnp.float32
bf16 = jnp.bfloat16

D_MODEL = 2048
N_META = 16
SSD_HEADS = 32
SSD_HEAD_DIM = 64
SSD_D_INNER = SSD_HEADS * SSD_HEAD_DIM
SSD_GROUPS = 2
SSD_STATE = 128
SSD_CONV = 4
SSD_BC = 2 * SSD_GROUPS * SSD_STATE
SSD_CONV_DIM = SSD_D_INNER + SSD_BC
ML_HEADS = 8
ML_QK_DIM = 128
ML_V_DIM = 256
ML_QK_INNER = ML_HEADS * ML_QK_DIM
ML_D_INNER = ML_HEADS * ML_V_DIM
D_FF = 5632
FFN_CONV = 3
EPS = 1e-6
NEG = -1e30

LANES = 128
SUBLANES = 8
VMEM_LIMIT = 56 * 1024 * 1024
MXU_COLS = 256
ROW_TILE = 1024
FFN_TILE = 512
INPROJ_TILE = 1536
OUTPROJ_TILE = 1024


def _row_tile(M):
    return ROW_TILE if M % ROW_TILE == 0 else M


U_Z, U_X, U_V, U_O = 0, 1, 2, 3
U_Q, U_K = 8, 9
U_BC = 20
U_MAIN = 4 * 2048 + 2 * 1024 + 512
GATE_LANE = 32
U_SMALL = 2 * LANES


def _dot(a, b):
    return jnp.dot(a.astype(bf16), b.astype(bf16), preferred_element_type=f32)


def _dot_nt(a, b):
    return lax.dot_general(a.astype(bf16), b.astype(bf16), (((1,), (1,)), ((), ())),
                           preferred_element_type=f32)


def _dot_tn(a, b):
    return lax.dot_general(a.astype(bf16), b.astype(bf16), (((0,), (0,)), ((), ())),
                           preferred_element_type=f32)


def _split3(a):
    hi = a.astype(bf16)
    r1 = a - hi.astype(f32)
    mid = r1.astype(bf16)
    lo = (r1 - mid.astype(f32)).astype(bf16)
    return hi, mid, lo


def _sel_right(a, e01):
    hi, mid, lo = _split3(a)
    d = lambda p: jnp.dot(p, e01, preferred_element_type=f32)
    return d(hi) + d(mid) + d(lo)


def _sel_left(e01, a):
    hi, mid, lo = _split3(a)
    d = lambda p: jnp.dot(e01, p, preferred_element_type=f32)
    return d(hi) + d(mid) + d(lo)


def _expand_heads(a, e01):
    hi = a.astype(bf16)
    mid = (a - hi.astype(f32)).astype(bf16)
    return (jnp.dot(hi, e01, preferred_element_type=f32)
            + jnp.dot(mid, e01, preferred_element_type=f32))


def _transpose_exact(a, eye):
    hi, mid, lo = _split3(a)
    d = lambda p: lax.dot_general(eye, p, (((1,), (1,)), ((), ())), preferred_element_type=f32)
    return d(hi) + d(mid) + d(lo)


def _iota2(shape, axis):
    return lax.broadcasted_iota(jnp.int32, shape, axis)


def _as01(m):
    return jnp.where(m, 1.0, 0.0).astype(bf16)


def _eye():
    return _as01(_iota2((LANES, LANES), 0) == _iota2((LANES, LANES), 1))


def _seq_masks(L, Ls):
    t = _iota2((L, L), 0)
    s = _iota2((L, L), 1)
    shift = Ls.bit_length() - 1
    same = (t >> shift) == (s >> shift)
    causal = same & (s <= t)
    causal_t = same & (t <= s)
    last = s == (t | (Ls - 1))
    return causal, _as01(causal), _as01(causal_t), _as01(last)


def _sigmoid(x):
    return 1.0 / (1.0 + jnp.exp(-x))


def _silu(x):
    return x * _sigmoid(x)


def _softplus(x):
    return jnp.maximum(x, 0.0) + jnp.log(1.0 + jnp.exp(-jnp.abs(x)))


def _rms(x, w):
    r = lax.rsqrt(jnp.mean(x * x, axis=-1, keepdims=True) + EPS)
    return (x * r) * w


def _causal_conv(x, hist, w, b):
    L, C = x.shape
    K = w.shape[0]
    r = _iota2((SUBLANES, C), 0)
    y = b + x * w[K - 1:K, :]
    for s in range(1, K):
        zt = jnp.where(r >= SUBLANES - s, hist, x[L - SUBLANES:, :])
        z = zt if L == SUBLANES else jnp.concatenate([x[:L - SUBLANES, :], zt], axis=0)
        y = y + pltpu.roll(z, s, 0) * w[K - 1 - s:K - s, :]
    return y


def _causal_conv_seqs(x, prev, w, b):
    L, C = x.shape
    K = w.shape[0]
    nseq = L // SUBLANES
    r = _iota2((L, C), 0) & (SUBLANES - 1)
    per_row = lambda a: jnp.broadcast_to(a, (nseq, SUBLANES, C)).reshape(L, C)
    y = b + x * w[K - 1:K, :]
    for s in range(1, K):
        head = per_row(prev[K - 1 - s])
        for rr in range(1, s):
            head = jnp.where(r == rr, per_row(prev[K - 1 - s + rr]), head)
        y = y + jnp.where(r >= s, pltpu.roll(x, s, 0), head) * w[K - 1 - s:K - s, :]
    return y


def _params(n_axes):
    return pltpu.CompilerParams(dimension_semantics=("arbitrary",) * n_axes,
                                vmem_limit_bytes=VMEM_LIMIT)


def _inproj_kernel(x_ref, nw_ref, w_ref, ws_ref, u_ref, us_ref, xn_ref):
    @pl.when(pl.program_id(1) == 0)
    def _():
        xn = _rms(x_ref[...], nw_ref[...]).astype(bf16)
        xn_ref[...] = xn
        us_ref[...] = jnp.dot(xn, ws_ref[...], preferred_element_type=f32)

    u_ref[...] = jnp.dot(xn_ref[...], w_ref[...], preferred_element_type=f32)


def _inproj(x, nw, w_main, w_small):
    M = x.shape[0]
    tm = _row_tile(M)
    tn = INPROJ_TILE
    return pl.pallas_call(
        _inproj_kernel,
        grid=(M // tm, U_MAIN // tn),
        in_specs=[pl.BlockSpec((tm, D_MODEL), lambda i, j: (i, 0)),
                  pl.BlockSpec((1, D_MODEL), lambda i, j: (0, 0)),
                  pl.BlockSpec((D_MODEL, tn), lambda i, j: (0, j)),
                  pl.BlockSpec((D_MODEL, U_SMALL), lambda i, j: (0, 0))],
        out_specs=[pl.BlockSpec((tm, tn), lambda i, j: (i, j)),
                   pl.BlockSpec((tm, U_SMALL), lambda i, j: (i, 0))],
        out_shape=[jax.ShapeDtypeStruct((M, U_MAIN), f32),
                   jax.ShapeDtypeStruct((M, U_SMALL), f32)],
        scratch_shapes=[pltpu.VMEM((tm, D_MODEL), bf16)],
        compiler_params=_params(2),
        name="inproj",
    )(x, nw, w_main, w_small)


def _ssd_kernel(z_ref, x_ref, bc_ref, sm_ref, hx_ref, hbc_ref, cwx_ref, cwbc_ref, cbx_ref, cbbc_ref,
                dtb_ref, alog_ref, dexp_ref, nw_ref, ehp_ref, ehpt_ref, s0_ref,
                y_ref, s_ref, tailx_ref, tailbc_ref, yz_ref, *, L, Ls, n_chunks):
    nseq = L // Ls
    c = pl.program_id(1)
    xpre = x_ref[...]
    bcpre = bc_ref[...]
    if nseq > 1:
        prev_x = [hx_ref[:, k:k + 1, :] for k in range(SSD_CONV - 1)]
        prev_bc = [hbc_ref[:, k:k + 1, :] for k in range(SSD_CONV - 1)]
        xc = _silu(_causal_conv_seqs(xpre, prev_x, cwx_ref[...], cbx_ref[...]))
        bcc = _silu(_causal_conv_seqs(bcpre, prev_bc, cwbc_ref[...], cbbc_ref[...]))
    else:
        if n_chunks == 1:
            hx, hbc = hx_ref[0], hbc_ref[0]
        else:
            first = c == 0
            hx = jnp.where(first, hx_ref[0], tailx_ref[...])
            hbc = jnp.where(first, hbc_ref[0], tailbc_ref[...])
        xc = _silu(_causal_conv(xpre, hx, cwx_ref[...], cbx_ref[...]))
        bcc = _silu(_causal_conv(bcpre, hbc, cwbc_ref[...], cbbc_ref[...]))
    if n_chunks > 1:
        tailx_ref[...] = xpre[L - SUBLANES:, :]
        tailbc_ref[...] = bcpre[L - SUBLANES:, :]

    lane = _iota2((L, LANES), 1)
    dt = jnp.where(lane < SSD_HEADS, _softplus(sm_ref[:, :LANES] + dtb_ref[...]), 0.0)
    dta = dt * (-jnp.exp(alog_ref[...]))

    causal, tri, tri_t, last = _seq_masks(L, Ls)
    eye = _eye()
    cum = _sel_left(tri, dta)
    cum_t = _sel_right(_transpose_exact(dta, eye), tri_t)
    cum_last = _sel_left(last, cum)
    ehp = ehp_ref[...]
    xdt = xc * _expand_heads(dt, ehp)
    xdtw = xdt * jnp.exp(_expand_heads(cum_last - cum, ehp))
    ecum = jnp.exp(_expand_heads(cum, ehp))

    if n_chunks > 1:
        @pl.when(c == 0)
        def _():
            s_ref[...] = s0_ref[...]
        sprev_ref = s_ref
    else:
        sprev_ref = s0_ref

    GE = SSD_D_INNER // SSD_GROUPS
    HPG = SSD_HEADS // SSD_GROUPS
    shift = Ls.bit_length() - 1
    seq_of_row = _iota2((L, 1), 0) >> shift
    lane_lo = lane < SSD_HEAD_DIM
    ssq = jnp.zeros((L, 1), f32)
    for g in range(SSD_GROUPS):
        bm = bcc[:, g * SSD_STATE:(g + 1) * SSD_STATE]
        cm = bcc[:, (SSD_GROUPS + g) * SSD_STATE:(SSD_GROUPS + g + 1) * SSD_STATE]
        cb = _dot_nt(cm, bm)
        ys = None
        for j in range(nseq):
            cmj = cm if nseq == 1 else jnp.where(seq_of_row == j, cm, 0.0)
            t = _dot_nt(cmj, sprev_ref[j, g * GE:(g + 1) * GE, :])
            ys = t if ys is None else ys + t
        for hp in range(HPG // 2):
            col0 = g * GE + hp * LANES
            xpair = xdt[:, col0:col0 + LANES].astype(bf16)
            outs = []
            for e in range(2):
                h = g * HPG + hp * 2 + e
                seg = cum[:, h:h + 1] - cum_t[h:h + 1, :]
                m = jnp.exp(jnp.where(causal, seg, NEG)) * cb
                outs.append(jnp.dot(m.astype(bf16), xpair, preferred_element_type=f32))
            blk = slice(col0, col0 + LANES)
            y = (jnp.where(lane_lo, outs[0], outs[1])
                 + ecum[:, blk] * ys[:, hp * LANES:(hp + 1) * LANES]
                 + dexp_ref[:, blk] * xc[:, blk])
            yz = y * _silu(z_ref[:, blk])
            ssq = ssq + jnp.sum(yz * yz, axis=-1, keepdims=True)
            yz_ref[:, blk] = yz
    r = lax.rsqrt(ssq * (1.0 / SSD_D_INNER) + EPS)
    y_ref[...] = ((yz_ref[...] * r) * nw_ref[...]).astype(bf16)

    ehpt = ehpt_ref[...]
    for j in range(nseq):
        tl = (j + 1) * Ls - 1
        dec = jnp.exp(jnp.broadcast_to(cum_t[:, tl:tl + 1], (LANES, LANES)))
        dec_rows = _sel_left(ehpt, dec)
        for g in range(SSD_GROUPS):
            bm = bcc[:, g * SSD_STATE:(g + 1) * SSD_STATE]
            xw = xdtw[:, g * GE:(g + 1) * GE]
            if nseq > 1:
                xw = jnp.where(seq_of_row == j, xw, 0.0)
            ds = _dot_tn(xw, bm)
            rs = slice(g * GE, (g + 1) * GE)
            s_ref[j, rs, :] = dec_rows[rs, :] * sprev_ref[j, rs, :] + ds


def _ssd(u, us, hx, hbc, s0, wts, *, B, T, L, Ls):
    nseq = L // Ls
    n_chunks = T // Ls if nseq == 1 else 1
    nblk = B // nseq
    rb = lambda i, c: i * n_chunks + c
    if nseq > 1:
        assert Ls == SUBLANES
        h_specs = [pl.BlockSpec((nseq, SSD_CONV - 1, SSD_D_INNER), lambda i, c: (i, 0, 0)),
                   pl.BlockSpec((nseq, SSD_CONV - 1, SSD_BC), lambda i, c: (i, 0, SSD_D_INNER // SSD_BC))]
    else:
        h_specs = [pl.BlockSpec((1, SUBLANES, SSD_D_INNER), lambda i, c: (i, 0, 0)),
                   pl.BlockSpec((1, SUBLANES, SSD_BC), lambda i, c: (i, 0, 0))]
    const = lambda shape: pl.BlockSpec(shape, lambda i, c: (0,) * len(shape))
    kern = functools.partial(_ssd_kernel, L=L, Ls=Ls, n_chunks=n_chunks)
    return pl.pallas_call(
        kern,
        grid=(nblk, n_chunks),
        in_specs=[pl.BlockSpec((L, SSD_D_INNER), lambda i, c: (rb(i, c), U_Z)),
                  pl.BlockSpec((L, SSD_D_INNER), lambda i, c: (rb(i, c), U_X)),
                  pl.BlockSpec((L, SSD_BC), lambda i, c: (rb(i, c), U_BC)),
                  pl.BlockSpec((L, U_SMALL), lambda i, c: (rb(i, c), 0)),
                  *h_specs,
                  const((SSD_CONV, SSD_D_INNER)), const((SSD_CONV, SSD_BC)),
                  const((1, SSD_D_INNER)), const((1, SSD_BC)),
                  const((1, LANES)), const((1, LANES)),
                  const((1, SSD_D_INNER)), const((1, SSD_D_INNER)),
                  const((LANES, SSD_D_INNER)), const((SSD_D_INNER, LANES)),
                  pl.BlockSpec((nseq, SSD_D_INNER, SSD_STATE), lambda i, c: (i, 0, 0))],
        out_specs=[pl.BlockSpec((L, SSD_D_INNER), lambda i, c: (rb(i, c), 0)),
                   pl.BlockSpec((nseq, SSD_D_INNER, SSD_STATE), lambda i, c: (i, 0, 0))],
        out_shape=[jax.ShapeDtypeStruct((B * T, SSD_D_INNER), bf16),
                   jax.ShapeDtypeStruct((B, SSD_D_INNER, SSD_STATE), f32)],
        scratch_shapes=[pltpu.VMEM((SUBLANES, SSD_D_INNER), f32),
                        pltpu.VMEM((SUBLANES, SSD_BC), f32),
                        pltpu.VMEM((L, SSD_D_INNER), f32)],
        compiler_params=_params(2),
        name="ssd",
    )(u, u, u, us, hx, hbc, wts["cw_x"], wts["cw_bc"], wts["cb_x"], wts["cb_bc"],
      wts["dt_bias"], wts["a_log"], wts["d_exp"], wts["ssd_norm_w"], wts["ehp"], wts["ehpt"], s0)


def _mlstm_kernel(q_ref, k_ref, v_ref, o_ref, sm_ref, ib_ref, fb_ref, nw_ref, c0_ref, n0_ref, m0_ref,
                  y_ref, c_ref, n_ref, m_ref, *, L, Ls, n_chunks):
    nseq = L // Ls
    c = pl.program_id(1)
    if n_chunks > 1:
        @pl.when(c == 0)
        def _():
            c_ref[...] = c0_ref[...]
            n_ref[...] = n0_ref[...]
            m_ref[...] = m0_ref[...]
        cprev_ref, nprev_ref, mprev_ref = c_ref, n_ref, m_ref
    else:
        cprev_ref, nprev_ref, mprev_ref = c0_ref, n0_ref, m0_ref

    per_tok = lambda a: jnp.broadcast_to(a, (nseq, Ls, LANES)).reshape(L, LANES)
    ig = sm_ref[:, :LANES] + ib_ref[...]
    fraw = sm_ref[:, LANES:] + fb_ref[...]
    lf = -_softplus(-fraw)
    causal, tri, _, last = _seq_masks(L, Ls)
    eye = _eye()
    F = _sel_left(tri, lf)
    FL = _sel_left(last, F)
    mp = per_tok(mprev_ref[:, 0:1, :])
    r_t = _transpose_exact(ig - F, eye)
    inter = F + mp
    lw = FL - F + ig
    segmax = jnp.max(lw.reshape(nseq, Ls, LANES), axis=1, keepdims=True)
    m_new = jnp.maximum(FL + mp, per_tok(segmax))
    sc = jnp.exp(lw - m_new)
    dec = jnp.exp(FL + mp - m_new)
    m_out = m_new.reshape(nseq, Ls, LANES)[:, 0:SUBLANES, :]

    shift = Ls.bit_length() - 1
    seq_of_row = _iota2((L, 1), 0) >> shift
    kscale = ML_QK_DIM ** -0.5
    for h in range(ML_HEADS):
        gl = GATE_LANE + h
        qh = q_ref[:, h * ML_QK_DIM:(h + 1) * ML_QK_DIM]
        kh = k_ref[:, h * ML_QK_DIM:(h + 1) * ML_QK_DIM] * kscale
        vh = v_ref[:, h * ML_V_DIM:(h + 1) * ML_V_DIM]
        dm = jnp.where(causal, F[:, gl:gl + 1] + r_t[gl:gl + 1, :], NEG)
        inter_h = inter[:, gl:gl + 1]
        m_t = jnp.maximum(jnp.max(dm, axis=-1, keepdims=True), inter_h)
        w = jnp.exp(dm - m_t) * _dot_nt(qh, kh)
        wi = jnp.exp(inter_h - m_t)
        crows = slice(h * ML_QK_DIM, (h + 1) * ML_QK_DIM)
        qc = None
        for j in range(nseq):
            qj = qh if nseq == 1 else jnp.where(seq_of_row == j, qh, 0.0)
            t = _dot(qj, cprev_ref[j, crows, :])
            qc = t if qc is None else qc + t
        n_tok = jnp.broadcast_to(nprev_ref[:, h:h + 1, :], (nseq, Ls, ML_QK_DIM)).reshape(L, ML_QK_DIM)
        qn = jnp.sum(qh * n_tok, axis=-1, keepdims=True)
        num = _dot(w, vh) + wi * qc
        den = jnp.sum(w, axis=-1, keepdims=True) + wi * qn
        hh = num / jnp.maximum(jnp.abs(den), jnp.exp(-m_t))
        hn = _rms(hh, nw_ref[:, h * ML_V_DIM:(h + 1) * ML_V_DIM])
        og = _sigmoid(o_ref[:, h * ML_V_DIM:(h + 1) * ML_V_DIM])
        y_ref[:, h * ML_V_DIM:(h + 1) * ML_V_DIM] = (og * hn).astype(bf16)
        ksc = kh * sc[:, gl:gl + 1]
        for j in range(nseq):
            r0 = j * Ls
            dj = dec[r0:r0 + 1, gl:gl + 1]
            kj = ksc if nseq == 1 else jnp.where(seq_of_row == j, ksc, 0.0)
            c_ref[j, crows, :] = dj * cprev_ref[j, crows, :] + _dot_tn(kj, vh)
            n_ref[j, h:h + 1, :] = (dj * nprev_ref[j, h:h + 1, :]
                                    + jnp.sum(ksc[r0:r0 + Ls, :], axis=0, keepdims=True))
    m_ref[...] = m_out


def _mlstm(u, us, c0, n0, m0, wts, *, B, T, L, Ls):
    nseq = L // Ls
    n_chunks = T // Ls if nseq == 1 else 1
    nblk = B // nseq
    rb = lambda i, c: i * n_chunks + c
    const = lambda shape: pl.BlockSpec(shape, lambda i, c: (0,) * len(shape))
    st = lambda shape: pl.BlockSpec((nseq,) + shape, lambda i, c: (i, 0, 0))
    kern = functools.partial(_mlstm_kernel, L=L, Ls=Ls, n_chunks=n_chunks)
    return pl.pallas_call(
        kern,
        grid=(nblk, n_chunks),
        in_specs=[pl.BlockSpec((L, ML_QK_INNER), lambda i, c: (rb(i, c), U_Q)),
                  pl.BlockSpec((L, ML_QK_INNER), lambda i, c: (rb(i, c), U_K)),
                  pl.BlockSpec((L, ML_D_INNER), lambda i, c: (rb(i, c), U_V)),
                  pl.BlockSpec((L, ML_D_INNER), lambda i, c: (rb(i, c), U_O)),
                  pl.BlockSpec((L, U_SMALL), lambda i, c: (rb(i, c), 0)),
                  const((1, LANES)), const((1, LANES)), const((1, ML_D_INNER)),
                  st((ML_QK_INNER, ML_V_DIM)), st((ML_HEADS, ML_QK_DIM)), st((SUBLANES, LANES))],
        out_specs=[pl.BlockSpec((L, ML_D_INNER), lambda i, c: (rb(i, c), 0)),
                   st((ML_QK_INNER, ML_V_DIM)), st((ML_HEADS, ML_QK_DIM)), st((SUBLANES, LANES))],
        out_shape=[jax.ShapeDtypeStruct((B * T, ML_D_INNER), bf16),
                   jax.ShapeDtypeStruct((B, ML_QK_INNER, ML_V_DIM), f32),
                   jax.ShapeDtypeStruct((B, ML_HEADS, ML_QK_DIM), f32),
                   jax.ShapeDtypeStruct((B, SUBLANES, LANES), f32)],
        compiler_params=_params(2),
        name="mlstm",
    )(u, u, u, u, us, wts["i_bias"], wts["f_bias"], wts["ml_norm_w"], c0, n0, m0)


def _outproj_kernel(ys_ref, ym_ref, ws_ref, wm_ref, h_ref, o_ref):
    o_ref[...] = (h_ref[...]
                  + jnp.dot(ys_ref[...], ws_ref[...], preferred_element_type=f32)
                  + jnp.dot(ym_ref[...], wm_ref[...], preferred_element_type=f32))


def _outproj(ys, ym, w_out, h):
    M = h.shape[0]
    tm = _row_tile(M)
    tn = OUTPROJ_TILE
    return pl.pallas_call(
        _outproj_kernel,
        grid=(M // tm, D_MODEL // tn),
        in_specs=[pl.BlockSpec((tm, SSD_D_INNER), lambda i, j: (i, 0)),
                  pl.BlockSpec((tm, ML_D_INNER), lambda i, j: (i, 0)),
                  pl.BlockSpec((SSD_D_INNER, tn), lambda i, j: (0, j)),
                  pl.BlockSpec((ML_D_INNER, tn), lambda i, j: (1, j)),
                  pl.BlockSpec((tm, tn), lambda i, j: (i, j))],
        out_specs=pl.BlockSpec((tm, tn), lambda i, j: (i, j)),
        out_shape=jax.ShapeDtypeStruct((M, D_MODEL), f32),
        compiler_params=_params(2),
        name="outproj",
    )(ys, ym, w_out, w_out, h)


def _ffn_kernel(h_ref, nw_ref, wg_ref, wv_ref, cwg_ref, cwv_ref, cbg_ref, cbv_ref, wd_ref, fw_ref, *rest,
                tm, tf, multi, blocks_per_seq):
    if multi:
        hg_ref, y_ref, tg_ref, xn_ref = rest
    else:
        hg_ref, hv_ref, y_ref, tg_ref, tv_ref, xn_ref, carg_ref, carv_ref = rest
    i = pl.program_id(0)
    j = pl.program_id(1)

    @pl.when(j == 0)
    def _():
        xn_ref[...] = _rms(h_ref[...], nw_ref[...]).astype(bf16)
        y_ref[...] = jnp.zeros_like(y_ref)

    xn = xn_ref[...]
    for c0 in range(0, tf, MXU_COLS):
        cs = slice(c0, c0 + MXU_COLS)
        ups = [jnp.dot(xn, w_ref[:, cs], preferred_element_type=f32) for w_ref in (wg_ref, wv_ref)]
        convd = []
        for half, (up, cw_ref, cb_ref) in enumerate(zip(ups, (cwg_ref, cwv_ref), (cbg_ref, cbv_ref))):
            if multi:
                prev = [hg_ref[:, 2 * k + half:2 * k + half + 1, cs] for k in range(FFN_CONV - 1)]
                convd.append(_causal_conv_seqs(up, prev, cw_ref[:, cs], cb_ref[:, cs]))
                up3 = up.reshape(tm // SUBLANES, SUBLANES, MXU_COLS)
                for k in range(FFN_CONV - 1):
                    r = SUBLANES - (FFN_CONV - 1) + k
                    tg_ref[:, 2 * k + half:2 * k + half + 1, cs] = up3[:, r:r + 1, :]
            else:
                h_ref_, car_ref, t_ref = ((hg_ref, carg_ref, tg_ref), (hv_ref, carv_ref, tv_ref))[half]
                tail = up[tm - SUBLANES:, :]
                if blocks_per_seq == 1:
                    hist = h_ref_[0, :, cs]
                else:
                    hist = jnp.where((i % blocks_per_seq) == 0, h_ref_[0, :, cs], car_ref[j, :, cs])
                    car_ref[j, :, cs] = tail
                t_ref[0, :, cs] = tail
                convd.append(_causal_conv(up, hist, cw_ref[:, cs], cb_ref[:, cs]))
        act = (_silu(convd[0]) * convd[1]).astype(bf16)
        y_ref[...] += jnp.dot(act, wd_ref[cs, :], preferred_element_type=f32)

    @pl.when(j == pl.num_programs(1) - 1)
    def _():
        y_ref[...] = _rms(h_ref[...] + y_ref[...], fw_ref[...])


def _ffn(h, ffn0, wts, *, B, T):
    M = h.shape[0]
    tm = _row_tile(M)
    tf = FFN_TILE
    n_ff = D_FF // tf
    multi = T < tm
    const = lambda shape: pl.BlockSpec(shape, lambda i, j: (0,) * len(shape))
    if multi:
        assert T == SUBLANES
        blocks_per_seq = 1
        nseq = tm // T
        st_shape = (B, 2 * (FFN_CONV - 1), D_FF)
        st_spec = pl.BlockSpec((nseq, 2 * (FFN_CONV - 1), tf), lambda i, j: (i, 0, j))
        hist, h_specs, t_specs = [ffn0.reshape(st_shape)], [st_spec], [st_spec]
        t_shapes = [jax.ShapeDtypeStruct(st_shape, f32)]
        scratch = []
    else:
        blocks_per_seq = T // tm
        pad = jnp.pad(ffn0, ((0, 0), (SUBLANES - (FFN_CONV - 1), 0), (0, 0)))
        hist = [pad, pad]
        h_specs = [pl.BlockSpec((1, SUBLANES, tf), lambda i, j: (i // blocks_per_seq, 0, j)),
                   pl.BlockSpec((1, SUBLANES, tf), lambda i, j: (i // blocks_per_seq, 0, n_ff + j))]
        t_specs = [pl.BlockSpec((1, SUBLANES, tf), lambda i, j: (i, 0, j))] * 2
        t_shapes = [jax.ShapeDtypeStruct((M // tm, SUBLANES, D_FF), f32)] * 2
        scratch = [pltpu.VMEM((n_ff, SUBLANES, tf), f32)] * 2
    kern = functools.partial(_ffn_kernel, tm=tm, tf=tf, multi=multi, blocks_per_seq=blocks_per_seq)
    y, *tails = pl.pallas_call(
        kern,
        grid=(M // tm, n_ff),
        in_specs=[pl.BlockSpec((tm, D_MODEL), lambda i, j: (i, 0), pipeline_mode=pl.Buffered(1)),
                  const((1, D_MODEL)),
                  pl.BlockSpec((D_MODEL, tf), lambda i, j: (0, j)),
                  pl.BlockSpec((D_MODEL, tf), lambda i, j: (0, n_ff + j)),
                  pl.BlockSpec((FFN_CONV, tf), lambda i, j: (0, j)),
                  pl.BlockSpec((FFN_CONV, tf), lambda i, j: (0, n_ff + j)),
                  pl.BlockSpec((1, tf), lambda i, j: (0, j)),
                  pl.BlockSpec((1, tf), lambda i, j: (0, n_ff + j)),
                  pl.BlockSpec((tf, D_MODEL), lambda i, j: (j, 0)),
                  const((1, D_MODEL)),
                  *h_specs],
        out_specs=[pl.BlockSpec((tm, D_MODEL), lambda i, j: (i, 0)), *t_specs],
        out_shape=[jax.ShapeDtypeStruct((M, D_MODEL), f32), *t_shapes],
        scratch_shapes=[pltpu.VMEM((tm, D_MODEL), bf16), *scratch],
        compiler_params=_params(2),
        name="ffn",
    )(h, wts["norm2_w"], wts["w_up"], wts["w_up"], wts["ffn_cw"], wts["ffn_cw"],
      wts["ffn_cb"], wts["ffn_cb"], wts["w_down"], wts["final_norm_w"], *hist)
    if multi:
        return y, tails[0].reshape(B, FFN_CONV - 1, 2 * D_FF)
    last = jnp.concatenate(tails, axis=-1).reshape(B, blocks_per_seq, SUBLANES, 2 * D_FF)
    return y, last[:, blocks_per_seq - 1, SUBLANES - (FFN_CONV - 1):, :]


def _hist_tile(state):
    return jnp.pad(state, ((0, 0), (SUBLANES - state.shape[1], 0), (0, 0)))


def _layer(h, states, wts, *, B, T, L, Ls):
    conv0, s0, c0, n0, m0, ffn0 = states
    u, us = _inproj(h, wts["norm1_w"], wts["w_main"], wts["w_small"])
    if L // Ls > 1:
        hx = hbc = conv0
    else:
        hx, hbc = _hist_tile(conv0[:, :, :SSD_D_INNER]), _hist_tile(conv0[:, :, SSD_D_INNER:])
    y_ssd, s_new = _ssd(u, us, hx, hbc, s0.reshape(B, SSD_D_INNER, SSD_STATE), wts, B=B, T=T, L=L, Ls=Ls)
    m_pad = jnp.broadcast_to(
        jnp.pad(m0, ((0, 0), (GATE_LANE, LANES - GATE_LANE - ML_HEADS)))[:, None, :], (B, SUBLANES, LANES))
    y_ml, c_new, n_new, m_new = _mlstm(u, us, c0.reshape(B, ML_QK_INNER, ML_V_DIM), n0, m_pad, wts,
                                       B=B, T=T, L=L, Ls=Ls)
    h1 = _outproj(y_ssd, y_ml, wts["w_out"], h)
    y, ffn_new = _ffn(h1, ffn0, wts, B=B, T=T)
    ur = u.reshape(B, T, U_MAIN)[:, T - (SSD_CONV - 1):, :]
    conv_new = jnp.concatenate([ur[:, :, U_X * SSD_D_INNER:(U_X + 1) * SSD_D_INNER],
                                ur[:, :, U_BC * SSD_BC:(U_BC + 1) * SSD_BC]], axis=-1)
    new_states = (conv_new,
                  s_new.reshape(B, SSD_HEADS, SSD_HEAD_DIM, SSD_STATE),
                  c_new.reshape(B, ML_HEADS, ML_QK_DIM, ML_V_DIM),
                  n_new,
                  m_new[:, 0, GATE_LANE:GATE_LANE + ML_HEADS],
                  ffn_new)
    return y, new_states


def _prep_weights(norm1_w, w_in, ssd_conv_w, ssd_conv_b, ssd_dt_bias, ssd_A_log, ssd_D, ssd_norm_w,
                  ml_i_bias, ml_f_bias, ml_norm_w, w_out, norm2_w, w_up, ffn_conv_w, ffn_conv_b, w_down,
                  final_norm_w):
    o = 0
    cols = {}
    for name, width in (("z", SSD_D_INNER), ("x", SSD_D_INNER), ("bc", SSD_BC), ("dt", SSD_HEADS),
                        ("q", ML_QK_INNER), ("k", ML_QK_INNER), ("v", ML_D_INNER), ("i", ML_HEADS),
                        ("f", ML_HEADS), ("o", ML_D_INNER)):
        cols[name] = w_in[:, o:o + width].astype(bf16)
        o += width
    w_main = jnp.concatenate([cols[n] for n in ("z", "x", "v", "o", "q", "k", "bc")], axis=1)
    zpad = lambda n: jnp.zeros((D_MODEL, n), bf16)
    w_small = jnp.concatenate([cols["dt"], cols["i"], zpad(LANES - GATE_LANE - ML_HEADS),
                               zpad(GATE_LANE), cols["f"], zpad(LANES - GATE_LANE - ML_HEADS)], axis=1)
    lane_row = lambda v, off: jnp.pad(v.astype(f32), (off, LANES - off - v.shape[0]))[None, :]
    hp = jnp.arange(SSD_D_INNER) // SSD_HEAD_DIM
    ehp = (jnp.arange(LANES)[:, None] == hp[None, :]).astype(bf16)
    return dict(
        norm1_w=norm1_w[None, :], w_main=w_main, w_small=w_small,
        cw_x=ssd_conv_w[:, :SSD_D_INNER], cw_bc=ssd_conv_w[:, SSD_D_INNER:],
        cb_x=ssd_conv_b[None, :SSD_D_INNER], cb_bc=ssd_conv_b[None, SSD_D_INNER:],
        dt_bias=lane_row(ssd_dt_bias, 0), a_log=lane_row(ssd_A_log, 0),
        d_exp=jnp.repeat(ssd_D.astype(f32), SSD_HEAD_DIM)[None, :], ssd_norm_w=ssd_norm_w[None, :],
        ehp=ehp, ehpt=ehp.T,
        i_bias=lane_row(ml_i_bias, GATE_LANE), f_bias=lane_row(ml_f_bias, GATE_LANE),
        ml_norm_w=ml_norm_w[None, :],
        w_out=w_out.astype(bf16), norm2_w=norm2_w[None, :], w_up=w_up.astype(bf16),
        ffn_cw=ffn_conv_w, ffn_cb=ffn_conv_b[None, :], w_down=w_down.astype(bf16),
        final_norm_w=final_norm_w[None, :])


def kernel(x_prompt, x_sample, state_ssd_conv, state_ssd, state_mlstm_C, state_mlstm_n, state_mlstm_m,
           state_ffn_conv, meta_tokens, norm1_w, w_in, ssd_conv_w, ssd_conv_b, ssd_dt_bias, ssd_A_log,
           ssd_D, ssd_norm_w, ml_i_bias, ml_f_bias, ml_norm_w, w_out, norm2_w, w_up, ffn_conv_w,
           ffn_conv_b, w_down, final_norm_w):
    depth = w_in.shape[0]
    assert depth == 1, "single-layer step"
    Bp, Tp, _ = x_prompt.shape
    Bs, Ts, _ = x_sample.shape
    wts = _prep_weights(norm1_w[0], w_in[0], ssd_conv_w[0], ssd_conv_b[0], ssd_dt_bias[0], ssd_A_log[0],
                        ssd_D[0], ssd_norm_w[0], ml_i_bias[0], ml_f_bias[0], ml_norm_w[0], w_out[0],
                        norm2_w[0], w_up[0], ffn_conv_w[0], ffn_conv_b[0], w_down[0], final_norm_w)
    zero_states = (jnp.zeros((1, SSD_CONV - 1, SSD_CONV_DIM), f32),
                   jnp.zeros((1, SSD_HEADS, SSD_HEAD_DIM, SSD_STATE), f32),
                   jnp.zeros((1, ML_HEADS, ML_QK_DIM, ML_V_DIM), f32),
                   jnp.zeros((1, ML_HEADS, ML_QK_DIM), f32),
                   jnp.zeros((1, ML_HEADS), f32),
                   jnp.zeros((1, FFN_CONV - 1, 2 * D_FF), f32))
    _, meta_states = _layer(meta_tokens.astype(f32), zero_states, wts, B=1, T=N_META, L=N_META, Ls=N_META)
    p_init = tuple(jnp.broadcast_to(s, (Bp,) + s.shape[1:]) for s in meta_states)
    yp, p_new = _layer(x_prompt.reshape(Bp * Tp, D_MODEL), p_init, wts, B=Bp, T=Tp, L=128, Ls=128)
    s_init = (state_ssd_conv[0], state_ssd[0], state_mlstm_C[0], state_mlstm_n[0], state_mlstm_m[0],
              state_ffn_conv[0])
    ys, s_new = _layer(x_sample.reshape(Bs * Ts, D_MODEL), s_init, wts, B=Bs, T=Ts, L=8 * Ts, Ls=Ts)
    return (yp.reshape(Bp, Tp, D_MODEL), ys.reshape(Bs, Ts, D_MODEL),
            *(s[None] for s in p_new), *(s[None] for s in s_new))
```

```python
import functools

import jax
import jax.numpy as jnp
from jax import lax
from jax.experimental import pallas as pl
from jax.experimental.pallas import tpu as pltpu

f32 = jnp.float32
bf16 = jnp.bfloat16

D_MODEL = 2048
N_META = 16
SSD_HEADS = 32
SSD_HEAD_DIM = 64
SSD_D_INNER = SSD_HEADS * SSD_HEAD_DIM
SSD_GROUPS = 2
SSD_STATE = 128
SSD_CONV = 4
SSD_BC = 2 * SSD_GROUPS * SSD_STATE
SSD_CONV_DIM = SSD_D_INNER + SSD_BC
ML_HEADS = 8
ML_QK_DIM = 128
ML_V_DIM = 256
ML_QK_INNER = ML_HEADS * ML_QK_DIM
ML_D_INNER = ML_HEADS * ML_V_DIM
D_FF = 5632
FFN_CONV = 3
EPS = 1e-6
NEG = -1e30

LANES = 128
SUBLANES = 8
VMEM_LIMIT = 56 * 1024 * 1024
MXU_COLS = 256
ROW_TILE = 1024
FFN_TILE = 512
INPROJ_TILE = 1536
OUTPROJ_TILE = 1024


def _row_tile(M):
    return ROW_TILE if M % ROW_TILE == 0 else M


U_Z, U_X, U_V, U_O = 0, 1, 2, 3
U_Q, U_K = 8, 9
U_BC = 20
U_MAIN = 4 * 2048 + 2 * 1024 + 512
GATE_LANE = 32
U_SMALL = 2 * LANES


def _dot(a, b):
    return jnp.dot(a.astype(bf16), b.astype(bf16), preferred_element_type=f32)


def _dot_nt(a, b):
    return lax.dot_general(a.astype(bf16), b.astype(bf16), (((1,), (1,)), ((), ())),
                           preferred_element_type=f32)


def _dot_tn(a, b):
    return lax.dot_general(a.astype(bf16), b.astype(bf16), (((0,), (0,)), ((), ())),
                           preferred_element_type=f32)


def _split3(a):
    hi = a.astype(bf16)
    r1 = a - hi.astype(f32)
    mid = r1.astype(bf16)
    lo = (r1 - mid.astype(f32)).astype(bf16)
    return hi, mid, lo


def _sel_right(a, e01):
    hi, mid, lo = _split3(a)
    d = lambda p: jnp.dot(p, e01, preferred_element_type=f32)
    return d(hi) + d(mid) + d(lo)


def _sel_left(e01, a):
    hi, mid, lo = _split3(a)
    d = lambda p: jnp.dot(e01, p, preferred_element_type=f32)
    return d(hi) + d(mid) + d(lo)


def _expand_heads(a, e01):
    hi = a.astype(bf16)
    mid = (a - hi.astype(f32)).astype(bf16)
    return (jnp.dot(hi, e01, preferred_element_type=f32)
            + jnp.dot(mid, e01, preferred_element_type=f32))


def _transpose_exact(a, eye):
    hi, mid, lo = _split3(a)
    d = lambda p: lax.dot_general(eye, p, (((1,), (1,)), ((), ())), preferred_element_type=f32)
    return d(hi) + d(mid) + d(lo)


def _iota2(shape, axis):
    return lax.broadcasted_iota(jnp.int32, shape, axis)


def _as01(m):
    return jnp.where(m, 1.0, 0.0).astype(bf16)


def _eye():
    return _as01(_iota2((LANES, LANES), 0) == _iota2((LANES, LANES), 1))


def _seq_masks(L, Ls):
    t = _iota2((L, L), 0)
    s = _iota2((L, L), 1)
    shift = Ls.bit_length() - 1
    same = (t >> shift) == (s >> shift)
    causal = same & (s <= t)
    causal_t = same & (t <= s)
    last = s == (t | (Ls - 1))
    return causal, _as01(causal), _as01(causal_t), _as01(last)


def _sigmoid(x):
    return 1.0 / (1.0 + jnp.exp(-x))


def _silu(x):
    return x * _sigmoid(x)


def _softplus(x):
    return jnp.maximum(x, 0.0) + jnp.log(1.0 + jnp.exp(-jnp.abs(x)))


def _rms(x, w):
    r = lax.rsqrt(jnp.mean(x * x, axis=-1, keepdims=True) + EPS)
    return (x * r) * w


def _causal_conv(x, hist, w, b):
    L, C = x.shape
    K = w.shape[0]
    r = _iota2((SUBLANES, C), 0)
    y = b + x * w[K - 1:K, :]
    for s in range(1, K):
        zt = jnp.where(r >= SUBLANES - s, hist, x[L - SUBLANES:, :])
        z = zt if L == SUBLANES else jnp.concatenate([x[:L - SUBLANES, :], zt], axis=0)
        y = y + pltpu.roll(z, s, 0) * w[K - 1 - s:K - s, :]
    return y


def _causal_conv_seqs(x, prev, w, b):
    L, C = x.shape
    K = w.shape[0]
    nseq = L // SUBLANES
    r = _iota2((L, C), 0) & (SUBLANES - 1)
    per_row = lambda a: jnp.broadcast_to(a, (nseq, SUBLANES, C)).reshape(L, C)
    y = b + x * w[K - 1:K, :]
    for s in range(1, K):
        head = per_row(prev[K - 1 - s])
        for rr in range(1, s):
            head = jnp.where(r == rr, per_row(prev[K - 1 - s + rr]), head)
        y = y + jnp.where(r >= s, pltpu.roll(x, s, 0), head) * w[K - 1 - s:K - s, :]
    return y


def _params(n_axes):
    return pltpu.CompilerParams(dimension_semantics=("arbitrary",) * n_axes,
                                vmem_limit_bytes=VMEM_LIMIT)


def _inproj_kernel(x_ref, nw_ref, w_ref, ws_ref, u_ref, us_ref, xn_ref):
    @pl.when(pl.program_id(1) == 0)
    def _():
        xn = _rms(x_ref[...], nw_ref[...]).astype(bf16)
        xn_ref[...] = xn
        us_ref[...] = jnp.dot(xn, ws_ref[...], preferred_element_type=f32)

    u_ref[...] = jnp.dot(xn_ref[...], w_ref[...], preferred_element_type=f32)


def _inproj(x, nw, w_main, w_small):
    M = x.shape[0]
    tm = _row_tile(M)
    tn = INPROJ_TILE
    return pl.pallas_call(
        _inproj_kernel,
        grid=(M // tm, U_MAIN // tn),
        in_specs=[pl.BlockSpec((tm, D_MODEL), lambda i, j: (i, 0)),
                  pl.BlockSpec((1, D_MODEL), lambda i, j: (0, 0)),
                  pl.BlockSpec((D_MODEL, tn), lambda i, j: (0, j)),
                  pl.BlockSpec((D_MODEL, U_SMALL), lambda i, j: (0, 0))],
        out_specs=[pl.BlockSpec((tm, tn), lambda i, j: (i, j)),
                   pl.BlockSpec((tm, U_SMALL), lambda i, j: (i, 0))],
        out_shape=[jax.ShapeDtypeStruct((M, U_MAIN), f32),
                   jax.ShapeDtypeStruct((M, U_SMALL), f32)],
        scratch_shapes=[pltpu.VMEM((tm, D_MODEL), bf16)],
        compiler_params=_params(2),
        name="inproj",
    )(x, nw, w_main, w_small)


def _ssd_kernel(z_ref, x_ref, bc_ref, sm_ref, hx_ref, hbc_ref, cwx_ref, cwbc_ref, cbx_ref, cbbc_ref,
                dtb_ref, alog_ref, dexp_ref, nw_ref, ehp_ref, ehpt_ref, s0_ref,
                y_ref, s_ref, tailx_ref, tailbc_ref, yz_ref, *, L, Ls, n_chunks):
    nseq = L // Ls
    c = pl.program_id(1)
    xpre = x_ref[...]
    bcpre = bc_ref[...]
    if nseq > 1:
        prev_x = [hx_ref[:, k:k + 1, :] for k in range(SSD_CONV - 1)]
        prev_bc = [hbc_ref[:, k:k + 1, :] for k in range(SSD_CONV - 1)]
        xc = _silu(_causal_conv_seqs(xpre, prev_x, cwx_ref[...], cbx_ref[...]))
        bcc = _silu(_causal_conv_seqs(bcpre, prev_bc, cwbc_ref[...], cbbc_ref[...]))
    else:
        if n_chunks == 1:
            hx, hbc = hx_ref[0], hbc_ref[0]
        else:
            first = c == 0
            hx = jnp.where(first, hx_ref[0], tailx_ref[...])
            hbc = jnp.where(first, hbc_ref[0], tailbc_ref[...])
        xc = _silu(_causal_conv(xpre, hx, cwx_ref[...], cbx_ref[...]))
        bcc = _silu(_causal_conv(bcpre, hbc, cwbc_ref[...], cbbc_ref[...]))
    if n_chunks > 1:
        tailx_ref[...] = xpre[L - SUBLANES:, :]
        tailbc_ref[...] = bcpre[L - SUBLANES:, :]

    lane = _iota2((L, LANES), 1)
    dt = jnp.where(lane < SSD_HEADS, _softplus(sm_ref[:, :LANES] + dtb_ref[...]), 0.0)
    dta = dt * (-jnp.exp(alog_ref[...]))

    causal, tri, tri_t, last = _seq_masks(L, Ls)
    eye = _eye()
    cum = _sel_left(tri, dta)
    cum_t = _sel_right(_transpose_exact(dta, eye), tri_t)
    cum_last = _sel_left(last, cum)
    ehp = ehp_ref[...]
    xdt = xc * _expand_heads(dt, ehp)
    xdtw = xdt * jnp.exp(_expand_heads(cum_last - cum, ehp))
    ecum = jnp.exp(_expand_heads(cum, ehp))

    if n_chunks > 1:
        @pl.when(c == 0)
        def _():
            s_ref[...] = s0_ref[...]
        sprev_ref = s_ref
    else:
        sprev_ref = s0_ref

    GE = SSD_D_INNER // SSD_GROUPS
    HPG = SSD_HEADS // SSD_GROUPS
    shift = Ls.bit_length() - 1
    seq_of_row = _iota2((L, 1), 0) >> shift
    lane_lo = lane < SSD_HEAD_DIM
    ys, intra = [], []
    for g in range(SSD_GROUPS):
        bm = bcc[:, g * SSD_STATE:(g + 1) * SSD_STATE]
        cm = bcc[:, (SSD_GROUPS + g) * SSD_STATE:(SSD_GROUPS + g + 1) * SSD_STATE]
        cb = _dot_nt(cm, bm)
        acc = None
        for j in range(nseq):
            cmj = cm if nseq == 1 else jnp.where(seq_of_row == j, cm, 0.0)
            t = _dot_nt(cmj, sprev_ref[j, g * GE:(g + 1) * GE, :])
            acc = t if acc is None else acc + t
        ys.append(acc)
        for hp in range(HPG // 2):
            col0 = g * GE + hp * LANES
            xpair = xdt[:, col0:col0 + LANES].astype(bf16)
            for e in range(2):
                h = g * HPG + hp * 2 + e
                seg = cum[:, h:h + 1] - cum_t[h:h + 1, :]
                m = jnp.exp(jnp.where(causal, seg, NEG)) * cb
                intra.append(jnp.dot(m.astype(bf16), xpair, preferred_element_type=f32))
    ssq = jnp.zeros((L, 1), f32)
    for g in range(SSD_GROUPS):
        for hp in range(HPG // 2):
            col0 = g * GE + hp * LANES
            blk = slice(col0, col0 + LANES)
            o0, o1 = intra[col0 // LANES * 2], intra[col0 // LANES * 2 + 1]
            y = (jnp.where(lane_lo, o0, o1)
                 + ecum[:, blk] * ys[g][:, hp * LANES:(hp + 1) * LANES]
                 + dexp_ref[:, blk] * xc[:, blk])
            yz = y * _silu(z_ref[:, blk])
            ssq = ssq + jnp.sum(yz * yz, axis=-1, keepdims=True)
            yz_ref[:, blk] = yz
    r = lax.rsqrt(ssq * (1.0 / SSD_D_INNER) + EPS)
    y_ref[...] = ((yz_ref[...] * r) * nw_ref[...]).astype(bf16)

    ehpt = ehpt_ref[...]
    for j in range(nseq):
        tl = (j + 1) * Ls - 1
        dec = jnp.exp(jnp.broadcast_to(cum_t[:, tl:tl + 1], (LANES, LANES)))
        dec_rows = _sel_left(ehpt, dec)
        for g in range(SSD_GROUPS):
            bm = bcc[:, g * SSD_STATE:(g + 1) * SSD_STATE]
            xw = xdtw[:, g * GE:(g + 1) * GE]
            if nseq > 1:
                xw = jnp.where(seq_of_row == j, xw, 0.0)
            ds = _dot_tn(xw, bm)
            rs = slice(g * GE, (g + 1) * GE)
            s_ref[j, rs, :] = dec_rows[rs, :] * sprev_ref[j, rs, :] + ds


def _ssd(u, us, hx, hbc, s0, wts, *, B, T, L, Ls):
    nseq = L // Ls
    n_chunks = T // Ls if nseq == 1 else 1
    nblk = B // nseq
    rb = lambda i, c: i * n_chunks + c
    if nseq > 1:
        assert Ls == SUBLANES
        h_specs = [pl.BlockSpec((nseq, SSD_CONV - 1, SSD_D_INNER), lambda i, c: (i, 0, 0)),
                   pl.BlockSpec((nseq, SSD_CONV - 1, SSD_BC), lambda i, c: (i, 0, SSD_D_INNER // SSD_BC))]
    else:
        h_specs = [pl.BlockSpec((1, SUBLANES, SSD_D_INNER), lambda i, c: (i, 0, 0)),
                   pl.BlockSpec((1, SUBLANES, SSD_BC), lambda i, c: (i, 0, 0))]
    const = lambda shape: pl.BlockSpec(shape, lambda i, c: (0,) * len(shape))
    kern = functools.partial(_ssd_kernel, L=L, Ls=Ls, n_chunks=n_chunks)
    return pl.pallas_call(
        kern,
        grid=(nblk, n_chunks),
        in_specs=[pl.BlockSpec((L, SSD_D_INNER), lambda i, c: (rb(i, c), U_Z)),
                  pl.BlockSpec((L, SSD_D_INNER), lambda i, c: (rb(i, c), U_X)),
                  pl.BlockSpec((L, SSD_BC), lambda i, c: (rb(i, c), U_BC)),
                  pl.BlockSpec((L, U_SMALL), lambda i, c: (rb(i, c), 0)),
                  *h_specs,
                  const((SSD_CONV, SSD_D_INNER)), const((SSD_CONV, SSD_BC)),
                  const((1, SSD_D_INNER)), const((1, SSD_BC)),
                  const((1, LANES)), const((1, LANES)),
                  const((1, SSD_D_INNER)), const((1, SSD_D_INNER)),
                  const((LANES, SSD_D_INNER)), const((SSD_D_INNER, LANES)),
                  pl.BlockSpec((nseq, SSD_D_INNER, SSD_STATE), lambda i, c: (i, 0, 0))],
        out_specs=[pl.BlockSpec((L, SSD_D_INNER), lambda i, c: (rb(i, c), 0)),
                   pl.BlockSpec((nseq, SSD_D_INNER, SSD_STATE), lambda i, c: (i, 0, 0))],
        out_shape=[jax.ShapeDtypeStruct((B * T, SSD_D_INNER), bf16),
                   jax.ShapeDtypeStruct((B, SSD_D_INNER, SSD_STATE), f32)],
        scratch_shapes=[pltpu.VMEM((SUBLANES, SSD_D_INNER), f32),
                        pltpu.VMEM((SUBLANES, SSD_BC), f32),
                        pltpu.VMEM((L, SSD_D_INNER), f32)],
        compiler_params=_params(2),
        name="ssd",
    )(u, u, u, us, hx, hbc, wts["cw_x"], wts["cw_bc"], wts["cb_x"], wts["cb_bc"],
      wts["dt_bias"], wts["a_log"], wts["d_exp"], wts["ssd_norm_w"], wts["ehp"], wts["ehpt"], s0)


def _mlstm_kernel(q_ref, k_ref, v_ref, o_ref, sm_ref, ib_ref, fb_ref, nw_ref, c0_ref, n0_ref, m0_ref,
                  y_ref, c_ref, n_ref, m_ref, *, L, Ls, n_chunks):
    nseq = L // Ls
    c = pl.program_id(1)
    if n_chunks > 1:
        @pl.when(c == 0)
        def _():
            c_ref[...] = c0_ref[...]
            n_ref[...] = n0_ref[...]
            m_ref[...] = m0_ref[...]
        cprev_ref, nprev_ref, mprev_ref = c_ref, n_ref, m_ref
    else:
        cprev_ref, nprev_ref, mprev_ref = c0_ref, n0_ref, m0_ref

    per_tok = lambda a: jnp.broadcast_to(a, (nseq, Ls, LANES)).reshape(L, LANES)
    ig = sm_ref[:, :LANES] + ib_ref[...]
    fraw = sm_ref[:, LANES:] + fb_ref[...]
    lf = -_softplus(-fraw)
    causal, tri, _, last = _seq_masks(L, Ls)
    eye = _eye()
    F = _sel_left(tri, lf)
    FL = _sel_left(last, F)
    mp = per_tok(mprev_ref[:, 0:1, :])
    r_t = _transpose_exact(ig - F, eye)
    inter = F + mp
    lw = FL - F + ig
    segmax = jnp.max(lw.reshape(nseq, Ls, LANES), axis=1, keepdims=True)
    m_new = jnp.maximum(FL + mp, per_tok(segmax))
    sc = jnp.exp(lw - m_new)
    dec = jnp.exp(FL + mp - m_new)
    m_out = m_new.reshape(nseq, Ls, LANES)[:, 0:SUBLANES, :]

    shift = Ls.bit_length() - 1
    seq_of_row = _iota2((L, 1), 0) >> shift
    kscale = ML_QK_DIM ** -0.5
    heads = range(ML_HEADS)
    qcols = lambda h: slice(h * ML_QK_DIM, (h + 1) * ML_QK_DIM)
    vcols = lambda h: slice(h * ML_V_DIM, (h + 1) * ML_V_DIM)
    q = [q_ref[:, qcols(h)] for h in heads]
    k = [k_ref[:, qcols(h)] * kscale for h in heads]
    v = [v_ref[:, vcols(h)].astype(bf16) for h in heads]
    qk = [_dot_nt(q[h], k[h]) for h in heads]
    qc = []
    for h in heads:
        acc = None
        for j in range(nseq):
            qj = q[h] if nseq == 1 else jnp.where(seq_of_row == j, q[h], 0.0)
            t = _dot(qj, cprev_ref[j, qcols(h), :])
            acc = t if acc is None else acc + t
        qc.append(acc)
    dm, m_t = [], []
    for h in heads:
        gl = GATE_LANE + h
        d = jnp.where(causal, F[:, gl:gl + 1] + r_t[gl:gl + 1, :], NEG)
        dm.append(d)
        m_t.append(jnp.maximum(jnp.max(d, axis=-1, keepdims=True), inter[:, gl:gl + 1]))
    w = [jnp.exp(dm[h] - m_t[h]) * qk[h] for h in heads]
    wv = [jnp.dot(w[h].astype(bf16), v[h], preferred_element_type=f32) for h in heads]
    for h in heads:
        gl = GATE_LANE + h
        wi = jnp.exp(inter[:, gl:gl + 1] - m_t[h])
        n_tok = jnp.broadcast_to(nprev_ref[:, h:h + 1, :], (nseq, Ls, ML_QK_DIM)).reshape(L, ML_QK_DIM)
        qn = jnp.sum(q[h] * n_tok, axis=-1, keepdims=True)
        num = wv[h] + wi * qc[h]
        den = jnp.sum(w[h], axis=-1, keepdims=True) + wi * qn
        hh = num / jnp.maximum(jnp.abs(den), jnp.exp(-m_t[h]))
        hn = _rms(hh, nw_ref[:, vcols(h)])
        y_ref[:, vcols(h)] = (_sigmoid(o_ref[:, vcols(h)]) * hn).astype(bf16)
    for h in heads:
        gl = GATE_LANE + h
        ksc = k[h] * sc[:, gl:gl + 1]
        for j in range(nseq):
            r0 = j * Ls
            dj = dec[r0:r0 + 1, gl:gl + 1]
            kj = ksc if nseq == 1 else jnp.where(seq_of_row == j, ksc, 0.0)
            c_ref[j, qcols(h), :] = dj * cprev_ref[j, qcols(h), :] + _dot_tn(kj, v[h])
            n_ref[j, h:h + 1, :] = (dj * nprev_ref[j, h:h + 1, :]
                                    + jnp.sum(ksc[r0:r0 + Ls, :], axis=0, keepdims=True))
    m_ref[...] = m_out


def _mlstm(u, us, c0, n0, m0, wts, *, B, T, L, Ls):
    nseq = L // Ls
    n_chunks = T // Ls if nseq == 1 else 1
    nblk = B // nseq
    rb = lambda i, c: i * n_chunks + c
    const = lambda shape: pl.BlockSpec(shape, lambda i, c: (0,) * len(shape))
    st = lambda shape: pl.BlockSpec((nseq,) + shape, lambda i, c: (i, 0, 0))
    kern = functools.partial(_mlstm_kernel, L=L, Ls=Ls, n_chunks=n_chunks)
    return pl.pallas_call(
        kern,
        grid=(nblk, n_chunks),
        in_specs=[pl.BlockSpec((L, ML_QK_INNER), lambda i, c: (rb(i, c), U_Q)),
                  pl.BlockSpec((L, ML_QK_INNER), lambda i, c: (rb(i, c), U_K)),
                  pl.BlockSpec((L, ML_D_INNER), lambda i, c: (rb(i, c), U_V)),
                  pl.BlockSpec((L, ML_D_INNER), lambda i, c: (rb(i, c), U_O)),
                  pl.BlockSpec((L, U_SMALL), lambda i, c: (rb(i, c), 0)),
                  const((1, LANES)), const((1, LANES)), const((1, ML_D_INNER)),
                  st((ML_QK_INNER, ML_V_DIM)), st((ML_HEADS, ML_QK_DIM)), st((SUBLANES, LANES))],
        out_specs=[pl.BlockSpec((L, ML_D_INNER), lambda i, c: (rb(i, c), 0)),
                   st((ML_QK_INNER, ML_V_DIM)), st((ML_HEADS, ML_QK_DIM)), st((SUBLANES, LANES))],
        out_shape=[jax.ShapeDtypeStruct((B * T, ML_D_INNER), bf16),
                   jax.ShapeDtypeStruct((B, ML_QK_INNER, ML_V_DIM), f32),
                   jax.ShapeDtypeStruct((B, ML_HEADS, ML_QK_DIM), f32),
                   jax.ShapeDtypeStruct((B, SUBLANES, LANES), f32)],
        compiler_params=_params(2),
        name="mlstm",
    )(u, u, u, u, us, wts["i_bias"], wts["f_bias"], wts["ml_norm_w"], c0, n0, m0)


def _outproj_kernel(ys_ref, ym_ref, ws_ref, wm_ref, h_ref, o_ref):
    o_ref[...] = (h_ref[...]
                  + jnp.dot(ys_ref[...], ws_ref[...], preferred_element_type=f32)
                  + jnp.dot(ym_ref[...], wm_ref[...], preferred_element_type=f32))


def _outproj(ys, ym, w_out, h):
    M = h.shape[0]
    tm = _row_tile(M)
    tn = OUTPROJ_TILE
    return pl.pallas_call(
        _outproj_kernel,
        grid=(M // tm, D_MODEL // tn),
        in_specs=[pl.BlockSpec((tm, SSD_D_INNER), lambda i, j: (i, 0)),
                  pl.BlockSpec((tm, ML_D_INNER), lambda i, j: (i, 0)),
                  pl.BlockSpec((SSD_D_INNER, tn), lambda i, j: (0, j)),
                  pl.BlockSpec((ML_D_INNER, tn), lambda i, j: (1, j)),
                  pl.BlockSpec((tm, tn), lambda i, j: (i, j))],
        out_specs=pl.BlockSpec((tm, tn), lambda i, j: (i, j)),
        out_shape=jax.ShapeDtypeStruct((M, D_MODEL), f32),
        compiler_params=_params(2),
        name="outproj",
    )(ys, ym, w_out, w_out, h)


def _ffn_kernel(h_ref, nw_ref, wg_ref, wv_ref, cwg_ref, cwv_ref, cbg_ref, cbv_ref, wd_ref, fw_ref, *rest,
                tm, tf, multi, blocks_per_seq):
    if multi:
        hg_ref, y_ref, tg_ref, xn_ref = rest
    else:
        hg_ref, hv_ref, y_ref, tg_ref, tv_ref, xn_ref, carg_ref, carv_ref = rest
    i = pl.program_id(0)
    j = pl.program_id(1)

    @pl.when(j == 0)
    def _():
        xn_ref[...] = _rms(h_ref[...], nw_ref[...]).astype(bf16)
        y_ref[...] = jnp.zeros_like(y_ref)

    xn = xn_ref[...]
    subs = [slice(c0, c0 + MXU_COLS) for c0 in range(0, tf, MXU_COLS)]
    up_dots = lambda cs: [jnp.dot(xn, w_ref[:, cs], preferred_element_type=f32) for w_ref in (wg_ref, wv_ref)]
    ups_next = up_dots(subs[0])
    for n, cs in enumerate(subs):
        ups = ups_next
        if n + 1 < len(subs):
            ups_next = up_dots(subs[n + 1])
        convd = []
        for half, (up, cw_ref, cb_ref) in enumerate(zip(ups, (cwg_ref, cwv_ref), (cbg_ref, cbv_ref))):
            if multi:
                prev = [hg_ref[:, 2 * k + half:2 * k + half + 1, cs] for k in range(FFN_CONV - 1)]
                convd.append(_causal_conv_seqs(up, prev, cw_ref[:, cs], cb_ref[:, cs]))
                up3 = up.reshape(tm // SUBLANES, SUBLANES, MXU_COLS)
                for k in range(FFN_CONV - 1):
                    r = SUBLANES - (FFN_CONV - 1) + k
                    tg_ref[:, 2 * k + half:2 * k + half + 1, cs] = up3[:, r:r + 1, :]
            else:
                h_ref_, car_ref, t_ref = ((hg_ref, carg_ref, tg_ref), (hv_ref, carv_ref, tv_ref))[half]
                tail = up[tm - SUBLANES:, :]
                if blocks_per_seq == 1:
                    hist = h_ref_[0, :, cs]
                else:
                    hist = jnp.where((i % blocks_per_seq) == 0, h_ref_[0, :, cs], car_ref[j, :, cs])
                    car_ref[j, :, cs] = tail
                t_ref[0, :, cs] = tail
                convd.append(_causal_conv(up, hist, cw_ref[:, cs], cb_ref[:, cs]))
        act = (_silu(convd[0]) * convd[1]).astype(bf16)
        y_ref[...] += jnp.dot(act, wd_ref[cs, :], preferred_element_type=f32)

    @pl.when(j == pl.num_programs(1) - 1)
    def _():
        y_ref[...] = _rms(h_ref[...] + y_ref[...], fw_ref[...])


def _ffn(h, ffn0, wts, *, B, T):
    M = h.shape[0]
    tm = _row_tile(M)
    tf = FFN_TILE
    n_ff = D_FF // tf
    multi = T < tm
    const = lambda shape: pl.BlockSpec(shape, lambda i, j: (0,) * len(shape))
    if multi:
        assert T == SUBLANES
        blocks_per_seq = 1
        nseq = tm // T
        st_shape = (B, 2 * (FFN_CONV - 1), D_FF)
        st_spec = pl.BlockSpec((nseq, 2 * (FFN_CONV - 1), tf), lambda i, j: (i, 0, j))
        hist, h_specs, t_specs = [ffn0.reshape(st_shape)], [st_spec], [st_spec]
        t_shapes = [jax.ShapeDtypeStruct(st_shape, f32)]
        scratch = []
    else:
        blocks_per_seq = T // tm
        pad = jnp.pad(ffn0, ((0, 0), (SUBLANES - (FFN_CONV - 1), 0), (0, 0)))
        hist = [pad, pad]
        h_specs = [pl.BlockSpec((1, SUBLANES, tf), lambda i, j: (i // blocks_per_seq, 0, j)),
                   pl.BlockSpec((1, SUBLANES, tf), lambda i, j: (i // blocks_per_seq, 0, n_ff + j))]
        t_specs = [pl.BlockSpec((1, SUBLANES, tf), lambda i, j: (i, 0, j))] * 2
        t_shapes = [jax.ShapeDtypeStruct((M // tm, SUBLANES, D_FF), f32)] * 2
        scratch = [pltpu.VMEM((n_ff, SUBLANES, tf), f32)] * 2
    kern = functools.partial(_ffn_kernel, tm=tm, tf=tf, multi=multi, blocks_per_seq=blocks_per_seq)
    y, *tails = pl.pallas_call(
        kern,
        grid=(M // tm, n_ff),
        in_specs=[pl.BlockSpec((tm, D_MODEL), lambda i, j: (i, 0), pipeline_mode=pl.Buffered(1)),
                  const((1, D_MODEL)),
                  pl.BlockSpec((D_MODEL, tf), lambda i, j: (0, j)),
                  pl.BlockSpec((D_MODEL, tf), lambda i, j: (0, n_ff + j)),
                  pl.BlockSpec((FFN_CONV, tf), lambda i, j: (0, j)),
                  pl.BlockSpec((FFN_CONV, tf), lambda i, j: (0, n_ff + j)),
                  pl.BlockSpec((1, tf), lambda i, j: (0, j)),
                  pl.BlockSpec((1, tf), lambda i, j: (0, n_ff + j)),
                  pl.BlockSpec((tf, D_MODEL), lambda i, j: (j, 0)),
                  const((1, D_MODEL)),
                  *h_specs],
        out_specs=[pl.BlockSpec((tm, D_MODEL), lambda i, j: (i, 0)), *t_specs],
        out_shape=[jax.ShapeDtypeStruct((M, D_MODEL), f32), *t_shapes],
        scratch_shapes=[pltpu.VMEM((tm, D_MODEL), bf16), *scratch],
        compiler_params=_params(2),
        name="ffn",
    )(h, wts["norm2_w"], wts["w_up"], wts["w_up"], wts["ffn_cw"], wts["ffn_cw"],
      wts["ffn_cb"], wts["ffn_cb"], wts["w_down"], wts["final_norm_w"], *hist)
    if multi:
        return y, tails[0].reshape(B, FFN_CONV - 1, 2 * D_FF)
    last = jnp.concatenate(tails, axis=-1).reshape(B, blocks_per_seq, SUBLANES, 2 * D_FF)
    return y, last[:, blocks_per_seq - 1, SUBLANES - (FFN_CONV - 1):, :]


def _hist_tile(state):
    return jnp.pad(state, ((0, 0), (SUBLANES - state.shape[1], 0), (0, 0)))


def _layer(h, states, wts, *, B, T, L, Ls):
    conv0, s0, c0, n0, m0, ffn0 = states
    u, us = _inproj(h, wts["norm1_w"], wts["w_main"], wts["w_small"])
    if L // Ls > 1:
        hx = hbc = conv0
    else:
        hx, hbc = _hist_tile(conv0[:, :, :SSD_D_INNER]), _hist_tile(conv0[:, :, SSD_D_INNER:])
    y_ssd, s_new = _ssd(u, us, hx, hbc, s0.reshape(B, SSD_D_INNER, SSD_STATE), wts, B=B, T=T, L=L, Ls=Ls)
    m_pad = jnp.broadcast_to(
        jnp.pad(m0, ((0, 0), (GATE_LANE, LANES - GATE_LANE - ML_HEADS)))[:, None, :], (B, SUBLANES, LANES))
    y_ml, c_new, n_new, m_new = _mlstm(u, us, c0.reshape(B, ML_QK_INNER, ML_V_DIM), n0, m_pad, wts,
                                       B=B, T=T, L=L, Ls=Ls)
    h1 = _outproj(y_ssd, y_ml, wts["w_out"], h)
    y, ffn_new = _ffn(h1, ffn0, wts, B=B, T=T)
    ur = u.reshape(B, T, U_MAIN)[:, T - (SSD_CONV - 1):, :]
    conv_new = jnp.concatenate([ur[:, :, U_X * SSD_D_INNER:(U_X + 1) * SSD_D_INNER],
                                ur[:, :, U_BC * SSD_BC:(U_BC + 1) * SSD_BC]], axis=-1)
    new_states = (conv_new,
                  s_new.reshape(B, SSD_HEADS, SSD_HEAD_DIM, SSD_STATE),
                  c_new.reshape(B, ML_HEADS, ML_QK_DIM, ML_V_DIM),
                  n_new,
                  m_new[:, 0, GATE_LANE:GATE_LANE + ML_HEADS],
                  ffn_new)
    return y, new_states


def _repack_kernel(w_ref, o_ref, *, moves):
    for src, width, dst in moves:
        o_ref[:, dst:dst + width] = w_ref[:, src:src + width].astype(bf16)


def _repack_w_in(w_in, offs):
    moves, dst = [], 0
    for name in ("z", "x", "v", "o", "q", "k", "bc"):
        src, width = offs[name]
        moves.append((src, width, dst))
        dst += width
    rows = 256
    return pl.pallas_call(
        functools.partial(_repack_kernel, moves=tuple(moves)),
        grid=(D_MODEL // rows,),
        in_specs=[pl.BlockSpec((rows, w_in.shape[1]), lambda i: (i, 0))],
        out_specs=pl.BlockSpec((rows, U_MAIN), lambda i: (i, 0)),
        out_shape=jax.ShapeDtypeStruct((D_MODEL, U_MAIN), bf16),
        compiler_params=_params(1),
        name="repack_w_in",
    )(w_in)


def _prep_weights(norm1_w, w_in, ssd_conv_w, ssd_conv_b, ssd_dt_bias, ssd_A_log, ssd_D, ssd_norm_w,
                  ml_i_bias, ml_f_bias, ml_norm_w, w_out, norm2_w, w_up, ffn_conv_w, ffn_conv_b, w_down,
                  final_norm_w):
    o = 0
    cols, offs = {}, {}
    for name, width in (("z", SSD_D_INNER), ("x", SSD_D_INNER), ("bc", SSD_BC), ("dt", SSD_HEADS),
                        ("q", ML_QK_INNER), ("k", ML_QK_INNER), ("v", ML_D_INNER), ("i", ML_HEADS),
                        ("f", ML_HEADS), ("o", ML_D_INNER)):
        offs[name] = (o, width)
        if name in ("dt", "i", "f"):
            cols[name] = w_in[:, o:o + width].astype(bf16)
        o += width
    w_main = _repack_w_in(w_in, offs)
    zpad = lambda n: jnp.zeros((D_MODEL, n), bf16)
    w_small = jnp.concatenate([cols["dt"], cols["i"], zpad(LANES - GATE_LANE - ML_HEADS),
                               zpad(GATE_LANE), cols["f"], zpad(LANES - GATE_LANE - ML_HEADS)], axis=1)
    lane_row = lambda v, off: jnp.pad(v.astype(f32), (off, LANES - off - v.shape[0]))[None, :]
    hp = jnp.arange(SSD_D_INNER) // SSD_HEAD_DIM
    ehp = (jnp.arange(LANES)[:, None] == hp[None, :]).astype(bf16)
    return dict(
        norm1_w=norm1_w[None, :], w_main=w_main, w_small=w_small,
        cw_x=ssd_conv_w[:, :SSD_D_INNER], cw_bc=ssd_conv_w[:, SSD_D_INNER:],
        cb_x=ssd_conv_b[None, :SSD_D_INNER], cb_bc=ssd_conv_b[None, SSD_D_INNER:],
        dt_bias=lane_row(ssd_dt_bias, 0), a_log=lane_row(ssd_A_log, 0),
        d_exp=jnp.repeat(ssd_D.astype(f32), SSD_HEAD_DIM)[None, :], ssd_norm_w=ssd_norm_w[None, :],
        ehp=ehp, ehpt=ehp.T,
        i_bias=lane_row(ml_i_bias, GATE_LANE), f_bias=lane_row(ml_f_bias, GATE_LANE),
        ml_norm_w=ml_norm_w[None, :],
        w_out=w_out.astype(bf16), norm2_w=norm2_w[None, :], w_up=w_up.astype(bf16),
        ffn_cw=ffn_conv_w, ffn_cb=ffn_conv_b[None, :], w_down=w_down.astype(bf16),
        final_norm_w=final_norm_w[None, :])


def kernel(x_prompt, x_sample, state_ssd_conv, state_ssd, state_mlstm_C, state_mlstm_n, state_mlstm_m,
           state_ffn_conv, meta_tokens, norm1_w, w_in, ssd_conv_w, ssd_conv_b, ssd_dt_bias, ssd_A_log,
           ssd_D, ssd_norm_w, ml_i_bias, ml_f_bias, ml_norm_w, w_out, norm2_w, w_up, ffn_conv_w,
           ffn_conv_b, w_down, final_norm_w):
    depth = w_in.shape[0]
    assert depth == 1, "single-layer step"
    Bp, Tp, _ = x_prompt.shape
    Bs, Ts, _ = x_sample.shape
    wts = _prep_weights(norm1_w[0], w_in[0], ssd_conv_w[0], ssd_conv_b[0], ssd_dt_bias[0], ssd_A_log[0],
                        ssd_D[0], ssd_norm_w[0], ml_i_bias[0], ml_f_bias[0], ml_norm_w[0], w_out[0],
                        norm2_w[0], w_up[0], ffn_conv_w[0], ffn_conv_b[0], w_down[0], final_norm_w)
    zero_states = (jnp.zeros((1, SSD_CONV - 1, SSD_CONV_DIM), f32),
                   jnp.zeros((1, SSD_HEADS, SSD_HEAD_DIM, SSD_STATE), f32),
                   jnp.zeros((1, ML_HEADS, ML_QK_DIM, ML_V_DIM), f32),
                   jnp.zeros((1, ML_HEADS, ML_QK_DIM), f32),
                   jnp.zeros((1, ML_HEADS), f32),
                   jnp.zeros((1, FFN_CONV - 1, 2 * D_FF), f32))
    _, meta_states = _layer(meta_tokens.astype(f32), zero_states, wts, B=1, T=N_META, L=N_META, Ls=N_META)
    p_init = tuple(jnp.broadcast_to(s, (Bp,) + s.shape[1:]) for s in meta_states)
    yp, p_new = _layer(x_prompt.reshape(Bp * Tp, D_MODEL), p_init, wts, B=Bp, T=Tp, L=128, Ls=128)
    s_init = (state_ssd_conv[0], state_ssd[0], state_mlstm_C[0], state_mlstm_n[0], state_mlstm_m[0],
              state_ffn_conv[0])
    ys, s_new = _layer(x_sample.reshape(Bs * Ts, D_MODEL), s_init, wts, B=Bs, T=Ts, L=8 * Ts, Ls=Ts)
    return (yp.reshape(Bp, Tp, D_MODEL), ys.reshape(Bs, Ts, D_MODEL),
            *(s[None] for s in p_new), *(s[None] for s in s_new))
```

```python
import functools

import jax
import jax.numpy as jnp
from jax import lax
from jax.experimental import pallas as pl
from jax.experimental.pallas import tpu as pltpu

f32 = jnp.float32
bf16 = jnp.bfloat16

D_MODEL = 2048
N_META = 16
SSD_HEADS = 32
SSD_HEAD_DIM = 64
SSD_D_INNER = SSD_HEADS * SSD_HEAD_DIM
SSD_GROUPS = 2
SSD_STATE = 128
SSD_CONV = 4
SSD_BC = 2 * SSD_GROUPS * SSD_STATE
SSD_CONV_DIM = SSD_D_INNER + SSD_BC
ML_HEADS = 8
ML_QK_DIM = 128
ML_V_DIM = 256
ML_QK_INNER = ML_HEADS * ML_QK_DIM
ML_D_INNER = ML_HEADS * ML_V_DIM
D_FF = 5632
FFN_CONV = 3
EPS = 1e-6
NEG = -1e30

LANES = 128
SUBLANES = 8
VMEM_LIMIT = 56 * 1024 * 1024
MXU_COLS = 256
ROW_TILE = 1024
FFN_TILE = 512
INPROJ_TILE = 1536
OUTPROJ_TILE = 512
REPACK_ROWS = 256


def _row_tile(M):
    return ROW_TILE if M % ROW_TILE == 0 else M


U_Z, U_X, U_V, U_O = 0, 1, 2, 3
U_Q, U_K = 8, 9
U_BC = 20
U_MAIN = 4 * 2048 + 2 * 1024 + 512
GATE_LANE = 32
U_SMALL = 2 * LANES


def _dot(a, b):
    return jnp.dot(a.astype(bf16), b.astype(bf16), preferred_element_type=f32)


def _dot_nt(a, b):
    return lax.dot_general(a.astype(bf16), b.astype(bf16), (((1,), (1,)), ((), ())),
                           preferred_element_type=f32)


def _dot_tn(a, b):
    return lax.dot_general(a.astype(bf16), b.astype(bf16), (((0,), (0,)), ((), ())),
                           preferred_element_type=f32)


def _split3(a):
    hi = a.astype(bf16)
    r1 = a - hi.astype(f32)
    mid = r1.astype(bf16)
    lo = (r1 - mid.astype(f32)).astype(bf16)
    return hi, mid, lo


def _sel_right(a, e01):
    hi, mid, lo = _split3(a)
    d = lambda p: jnp.dot(p, e01, preferred_element_type=f32)
    return d(hi) + d(mid) + d(lo)


def _sel_left(e01, a):
    hi, mid, lo = _split3(a)
    d = lambda p: jnp.dot(e01, p, preferred_element_type=f32)
    return d(hi) + d(mid) + d(lo)


def _expand_heads(a, e01):
    hi = a.astype(bf16)
    mid = (a - hi.astype(f32)).astype(bf16)
    return (jnp.dot(hi, e01, preferred_element_type=f32)
            + jnp.dot(mid, e01, preferred_element_type=f32))


def _transpose_exact(a, eye):
    hi, mid, lo = _split3(a)
    d = lambda p: lax.dot_general(eye, p, (((1,), (1,)), ((), ())), preferred_element_type=f32)
    return d(hi) + d(mid) + d(lo)


def _iota2(shape, axis):
    return lax.broadcasted_iota(jnp.int32, shape, axis)


def _as01(m):
    return jnp.where(m, 1.0, 0.0).astype(bf16)


def _eye():
    return _as01(_iota2((LANES, LANES), 0) == _iota2((LANES, LANES), 1))


def _seq_masks(L, Ls):
    t = _iota2((L, L), 0)
    s = _iota2((L, L), 1)
    shift = Ls.bit_length() - 1
    same = (t >> shift) == (s >> shift)
    causal = same & (s <= t)
    causal_t = same & (t <= s)
    last = s == (t | (Ls - 1))
    return causal, _as01(causal), _as01(causal_t), _as01(last)


def _sigmoid(x):
    return 1.0 / (1.0 + jnp.exp(-x))


def _silu(x):
    return x * _sigmoid(x)


def _softplus(x):
    return jnp.maximum(x, 0.0) + jnp.log(1.0 + jnp.exp(-jnp.abs(x)))


def _rms(x, w):
    r = lax.rsqrt(jnp.mean(x * x, axis=-1, keepdims=True) + EPS)
    return (x * r) * w


def _causal_conv(x, hist, w, b):
    L, C = x.shape
    K = w.shape[0]
    r = _iota2((SUBLANES, C), 0)
    y = b + x * w[K - 1:K, :]
    for s in range(1, K):
        zt = jnp.where(r >= SUBLANES - s, hist, x[L - SUBLANES:, :])
        z = zt if L == SUBLANES else jnp.concatenate([x[:L - SUBLANES, :], zt], axis=0)
        y = y + pltpu.roll(z, s, 0) * w[K - 1 - s:K - s, :]
    return y


def _causal_conv_seqs(x, prev, w, b):
    L, C = x.shape
    K = w.shape[0]
    nseq = L // SUBLANES
    r = _iota2((L, C), 0) & (SUBLANES - 1)
    per_row = lambda a: jnp.broadcast_to(a, (nseq, SUBLANES, C)).reshape(L, C)
    y = b + x * w[K - 1:K, :]
    for s in range(1, K):
        head = per_row(prev[K - 1 - s])
        for rr in range(1, s):
            head = jnp.where(r == rr, per_row(prev[K - 1 - s + rr]), head)
        y = y + jnp.where(r >= s, pltpu.roll(x, s, 0), head) * w[K - 1 - s:K - s, :]
    return y


def _params(n_axes):
    return pltpu.CompilerParams(dimension_semantics=("arbitrary",) * n_axes,
                                vmem_limit_bytes=VMEM_LIMIT)


def _inproj_kernel(x_ref, nw_ref, w_ref, ws_ref, u_ref, us_ref, xn_ref):
    @pl.when(pl.program_id(1) == 0)
    def _():
        xn = _rms(x_ref[...], nw_ref[...]).astype(bf16)
        xn_ref[...] = xn
        us_ref[...] = _dot_nt(xn, ws_ref[...])

    u_ref[...] = _dot_nt(xn_ref[...], w_ref[...])


def _inproj(x, nw, w_main, w_small):
    M = x.shape[0]
    tm = _row_tile(M)
    tn = INPROJ_TILE
    return pl.pallas_call(
        _inproj_kernel,
        grid=(M // tm, U_MAIN // tn),
        in_specs=[pl.BlockSpec((tm, D_MODEL), lambda i, j: (i, 0)),
                  pl.BlockSpec((1, D_MODEL), lambda i, j: (0, 0)),
                  pl.BlockSpec((tn, D_MODEL), lambda i, j: (j, 0)),
                  pl.BlockSpec((U_SMALL, D_MODEL), lambda i, j: (0, 0))],
        out_specs=[pl.BlockSpec((tm, tn), lambda i, j: (i, j)),
                   pl.BlockSpec((tm, U_SMALL), lambda i, j: (i, 0))],
        out_shape=[jax.ShapeDtypeStruct((M, U_MAIN), f32),
                   jax.ShapeDtypeStruct((M, U_SMALL), f32)],
        scratch_shapes=[pltpu.VMEM((tm, D_MODEL), bf16)],
        compiler_params=_params(2),
        name="inproj",
    )(x, nw, w_main, w_small)


def _ssd_kernel(z_ref, x_ref, bc_ref, sm_ref, hx_ref, hbc_ref, cwx_ref, cwbc_ref, cbx_ref, cbbc_ref,
                dtb_ref, alog_ref, dexp_ref, nw_ref, ehp_ref, ehpt_ref, s0_ref,
                y_ref, s_ref, tailx_ref, tailbc_ref, yz_ref, *, L, Ls, n_chunks):
    nseq = L // Ls
    c = pl.program_id(1)
    xpre = x_ref[...]
    bcpre = bc_ref[...]
    if nseq > 1:
        prev_x = [hx_ref[:, k:k + 1, :] for k in range(SSD_CONV - 1)]
        prev_bc = [hbc_ref[:, k:k + 1, :] for k in range(SSD_CONV - 1)]
        xc = _silu(_causal_conv_seqs(xpre, prev_x, cwx_ref[...], cbx_ref[...]))
        bcc = _silu(_causal_conv_seqs(bcpre, prev_bc, cwbc_ref[...], cbbc_ref[...]))
    else:
        if n_chunks == 1:
            hx, hbc = hx_ref[0], hbc_ref[0]
        else:
            first = c == 0
            hx = jnp.where(first, hx_ref[0], tailx_ref[...])
            hbc = jnp.where(first, hbc_ref[0], tailbc_ref[...])
        xc = _silu(_causal_conv(xpre, hx, cwx_ref[...], cbx_ref[...]))
        bcc = _silu(_causal_conv(bcpre, hbc, cwbc_ref[...], cbbc_ref[...]))
    if n_chunks > 1:
        tailx_ref[...] = xpre[L - SUBLANES:, :]
        tailbc_ref[...] = bcpre[L - SUBLANES:, :]

    lane = _iota2((L, LANES), 1)
    dt = jnp.where(lane < SSD_HEADS, _softplus(sm_ref[:, :LANES] + dtb_ref[...]), 0.0)
    dta = dt * (-jnp.exp(alog_ref[...]))

    causal, tri, tri_t, last = _seq_masks(L, Ls)
    eye = _eye()
    cum = _sel_left(tri, dta)
    cum_t = _sel_right(_transpose_exact(dta, eye), tri_t)
    cum_last = _sel_left(last, cum)
    ehp = ehp_ref[...]
    xdt = xc * _expand_heads(dt, ehp)
    xdtw = xdt * jnp.exp(_expand_heads(cum_last - cum, ehp))
    ecum = jnp.exp(_expand_heads(cum, ehp))

    if n_chunks > 1:
        @pl.when(c == 0)
        def _():
            s_ref[...] = s0_ref[...]
        sprev_ref = s_ref
    else:
        sprev_ref = s0_ref

    GE = SSD_D_INNER // SSD_GROUPS
    HPG = SSD_HEADS // SSD_GROUPS
    shift = Ls.bit_length() - 1
    seq_of_row = _iota2((L, 1), 0) >> shift
    lane_lo = lane < SSD_HEAD_DIM
    ys, intra = [], []
    for g in range(SSD_GROUPS):
        bm = bcc[:, g * SSD_STATE:(g + 1) * SSD_STATE]
        cm = bcc[:, (SSD_GROUPS + g) * SSD_STATE:(SSD_GROUPS + g + 1) * SSD_STATE]
        cb = _dot_nt(cm, bm)
        acc = None
        for j in range(nseq):
            cmj = cm if nseq == 1 else jnp.where(seq_of_row == j, cm, 0.0)
            t = _dot_nt(cmj, sprev_ref[j, g * GE:(g + 1) * GE, :])
            acc = t if acc is None else acc + t
        ys.append(acc)
        for hp in range(HPG // 2):
            col0 = g * GE + hp * LANES
            xpair = xdt[:, col0:col0 + LANES].astype(bf16)
            for e in range(2):
                h = g * HPG + hp * 2 + e
                seg = cum[:, h:h + 1] - cum_t[h:h + 1, :]
                m = jnp.exp(jnp.where(causal, seg, NEG)) * cb
                intra.append(jnp.dot(m.astype(bf16), xpair, preferred_element_type=f32))
    ssq = jnp.zeros((L, 1), f32)
    for g in range(SSD_GROUPS):
        for hp in range(HPG // 2):
            col0 = g * GE + hp * LANES
            blk = slice(col0, col0 + LANES)
            o0, o1 = intra[col0 // LANES * 2], intra[col0 // LANES * 2 + 1]
            y = (jnp.where(lane_lo, o0, o1)
                 + ecum[:, blk] * ys[g][:, hp * LANES:(hp + 1) * LANES]
                 + dexp_ref[:, blk] * xc[:, blk])
            yz = y * _silu(z_ref[:, blk])
            ssq = ssq + jnp.sum(yz * yz, axis=-1, keepdims=True)
            yz_ref[:, blk] = yz
    r = lax.rsqrt(ssq * (1.0 / SSD_D_INNER) + EPS)
    y_ref[...] = ((yz_ref[...] * r) * nw_ref[...]).astype(bf16)

    ehpt = ehpt_ref[...]
    for j in range(nseq):
        tl = (j + 1) * Ls - 1
        dec = jnp.exp(jnp.broadcast_to(cum_t[:, tl:tl + 1], (LANES, LANES)))
        dec_rows = _sel_left(ehpt, dec)
        for g in range(SSD_GROUPS):
            bm = bcc[:, g * SSD_STATE:(g + 1) * SSD_STATE]
            xw = xdtw[:, g * GE:(g + 1) * GE]
            if nseq > 1:
                xw = jnp.where(seq_of_row == j, xw, 0.0)
            ds = _dot_tn(xw, bm)
            rs = slice(g * GE, (g + 1) * GE)
            s_ref[j, rs, :] = dec_rows[rs, :] * sprev_ref[j, rs, :] + ds


def _ssd(u, us, hx, hbc, s0, wts, *, B, T, L, Ls):
    nseq = L // Ls
    n_chunks = T // Ls if nseq == 1 else 1
    nblk = B // nseq
    rb = lambda i, c: i * n_chunks + c
    if nseq > 1:
        assert Ls == SUBLANES
        h_specs = [pl.BlockSpec((nseq, SSD_CONV - 1, SSD_D_INNER), lambda i, c: (i, 0, 0)),
                   pl.BlockSpec((nseq, SSD_CONV - 1, SSD_BC), lambda i, c: (i, 0, SSD_D_INNER // SSD_BC))]
    else:
        h_specs = [pl.BlockSpec((1, SUBLANES, SSD_D_INNER), lambda i, c: (i, 0, 0)),
                   pl.BlockSpec((1, SUBLANES, SSD_BC), lambda i, c: (i, 0, 0))]
    const = lambda shape: pl.BlockSpec(shape, lambda i, c: (0,) * len(shape))
    kern = functools.partial(_ssd_kernel, L=L, Ls=Ls, n_chunks=n_chunks)
    return pl.pallas_call(
        kern,
        grid=(nblk, n_chunks),
        in_specs=[pl.BlockSpec((L, SSD_D_INNER), lambda i, c: (rb(i, c), U_Z)),
                  pl.BlockSpec((L, SSD_D_INNER), lambda i, c: (rb(i, c), U_X)),
                  pl.BlockSpec((L, SSD_BC), lambda i, c: (rb(i, c), U_BC)),
                  pl.BlockSpec((L, U_SMALL), lambda i, c: (rb(i, c), 0)),
                  *h_specs,
                  const((SSD_CONV, SSD_D_INNER)), const((SSD_CONV, SSD_BC)),
                  const((1, SSD_D_INNER)), const((1, SSD_BC)),
                  const((1, LANES)), const((1, LANES)),
                  const((1, SSD_D_INNER)), const((1, SSD_D_INNER)),
                  const((LANES, SSD_D_INNER)), const((SSD_D_INNER, LANES)),
                  pl.BlockSpec((nseq, SSD_D_INNER, SSD_STATE), lambda i, c: (i, 0, 0))],
        out_specs=[pl.BlockSpec((L, SSD_D_INNER), lambda i, c: (rb(i, c), 0)),
                   pl.BlockSpec((nseq, SSD_D_INNER, SSD_STATE), lambda i, c: (i, 0, 0))],
        out_shape=[jax.ShapeDtypeStruct((B * T, SSD_D_INNER), bf16),
                   jax.ShapeDtypeStruct((B, SSD_D_INNER, SSD_STATE), f32)],
        scratch_shapes=[pltpu.VMEM((SUBLANES, SSD_D_INNER), f32),
                        pltpu.VMEM((SUBLANES, SSD_BC), f32),
                        pltpu.VMEM((L, SSD_D_INNER), f32)],
        compiler_params=_params(2),
        name="ssd",
    )(u, u, u, us, hx, hbc, wts["cw_x"], wts["cw_bc"], wts["cb_x"], wts["cb_bc"],
      wts["dt_bias"], wts["a_log"], wts["d_exp"], wts["ssd_norm_w"], wts["ehp"], wts["ehpt"], s0)


def _mlstm_kernel(q_ref, k_ref, v_ref, o_ref, sm_ref, ib_ref, fb_ref, nw_ref, c0_ref, n0_ref, m0_ref,
                  y_ref, c_ref, n_ref, m_ref, *, L, Ls, n_chunks):
    nseq = L // Ls
    c = pl.program_id(1)
    if n_chunks > 1:
        @pl.when(c == 0)
        def _():
            c_ref[...] = c0_ref[...]
            n_ref[...] = n0_ref[...]
            m_ref[...] = m0_ref[...]
        cprev_ref, nprev_ref, mprev_ref = c_ref, n_ref, m_ref
    else:
        cprev_ref, nprev_ref, mprev_ref = c0_ref, n0_ref, m0_ref

    per_tok = lambda a: jnp.broadcast_to(a, (nseq, Ls, LANES)).reshape(L, LANES)
    ig = sm_ref[:, :LANES] + ib_ref[...]
    fraw = sm_ref[:, LANES:] + fb_ref[...]
    lf = -_softplus(-fraw)
    causal, tri, _, last = _seq_masks(L, Ls)
    eye = _eye()
    F = _sel_left(tri, lf)
    FL = _sel_left(last, F)
    mp = per_tok(mprev_ref[:, 0:1, :])
    r_t = _transpose_exact(ig - F, eye)
    inter = F + mp
    lw = FL - F + ig
    segmax = jnp.max(lw.reshape(nseq, Ls, LANES), axis=1, keepdims=True)
    m_new = jnp.maximum(FL + mp, per_tok(segmax))
    sc = jnp.exp(lw - m_new)
    dec = jnp.exp(FL + mp - m_new)
    m_out = m_new.reshape(nseq, Ls, LANES)[:, 0:SUBLANES, :]

    shift = Ls.bit_length() - 1
    seq_of_row = _iota2((L, 1), 0) >> shift
    kscale = ML_QK_DIM ** -0.5
    heads = range(ML_HEADS)
    qcols = lambda h: slice(h * ML_QK_DIM, (h + 1) * ML_QK_DIM)
    vcols = lambda h: slice(h * ML_V_DIM, (h + 1) * ML_V_DIM)
    q = [q_ref[:, qcols(h)] for h in heads]
    k = [k_ref[:, qcols(h)] * kscale for h in heads]
    v = [v_ref[:, vcols(h)].astype(bf16) for h in heads]
    qk = [_dot_nt(q[h], k[h]) for h in heads]
    qc = []
    for h in heads:
        acc = None
        for j in range(nseq):
            qj = q[h] if nseq == 1 else jnp.where(seq_of_row == j, q[h], 0.0)
            t = _dot(qj, cprev_ref[j, qcols(h), :])
            acc = t if acc is None else acc + t
        qc.append(acc)
    dm, m_t = [], []
    for h in heads:
        gl = GATE_LANE + h
        d = jnp.where(causal, F[:, gl:gl + 1] + r_t[gl:gl + 1, :], NEG)
        dm.append(d)
        m_t.append(jnp.maximum(jnp.max(d, axis=-1, keepdims=True), inter[:, gl:gl + 1]))
    w = [jnp.exp(dm[h] - m_t[h]) * qk[h] for h in heads]
    wv = [jnp.dot(w[h].astype(bf16), v[h], preferred_element_type=f32) for h in heads]
    for h in heads:
        gl = GATE_LANE + h
        wi = jnp.exp(inter[:, gl:gl + 1] - m_t[h])
        n_tok = jnp.broadcast_to(nprev_ref[:, h:h + 1, :], (nseq, Ls, ML_QK_DIM)).reshape(L, ML_QK_DIM)
        qn = jnp.sum(q[h] * n_tok, axis=-1, keepdims=True)
        num = wv[h] + wi * qc[h]
        den = jnp.sum(w[h], axis=-1, keepdims=True) + wi * qn
        hh = num / jnp.maximum(jnp.abs(den), jnp.exp(-m_t[h]))
        hn = _rms(hh, nw_ref[:, vcols(h)])
        y_ref[:, vcols(h)] = (_sigmoid(o_ref[:, vcols(h)]) * hn).astype(bf16)
    for h in heads:
        gl = GATE_LANE + h
        ksc = k[h] * sc[:, gl:gl + 1]
        for j in range(nseq):
            r0 = j * Ls
            dj = dec[r0:r0 + 1, gl:gl + 1]
            kj = ksc if nseq == 1 else jnp.where(seq_of_row == j, ksc, 0.0)
            c_ref[j, qcols(h), :] = dj * cprev_ref[j, qcols(h), :] + _dot_tn(kj, v[h])
            n_ref[j, h:h + 1, :] = (dj * nprev_ref[j, h:h + 1, :]
                                    + jnp.sum(ksc[r0:r0 + Ls, :], axis=0, keepdims=True))
    m_ref[...] = m_out


def _mlstm(u, us, c0, n0, m0, wts, *, B, T, L, Ls):
    nseq = L // Ls
    n_chunks = T // Ls if nseq == 1 else 1
    nblk = B // nseq
    rb = lambda i, c: i * n_chunks + c
    const = lambda shape: pl.BlockSpec(shape, lambda i, c: (0,) * len(shape))
    st = lambda shape: pl.BlockSpec((nseq,) + shape, lambda i, c: (i, 0, 0))
    kern = functools.partial(_mlstm_kernel, L=L, Ls=Ls, n_chunks=n_chunks)
    return pl.pallas_call(
        kern,
        grid=(nblk, n_chunks),
        in_specs=[pl.BlockSpec((L, ML_QK_INNER), lambda i, c: (rb(i, c), U_Q)),
                  pl.BlockSpec((L, ML_QK_INNER), lambda i, c: (rb(i, c), U_K)),
                  pl.BlockSpec((L, ML_D_INNER), lambda i, c: (rb(i, c), U_V)),
                  pl.BlockSpec((L, ML_D_INNER), lambda i, c: (rb(i, c), U_O)),
                  pl.BlockSpec((L, U_SMALL), lambda i, c: (rb(i, c), 0)),
                  const((1, LANES)), const((1, LANES)), const((1, ML_D_INNER)),
                  st((ML_QK_INNER, ML_V_DIM)), st((ML_HEADS, ML_QK_DIM)), st((SUBLANES, LANES))],
        out_specs=[pl.BlockSpec((L, ML_D_INNER), lambda i, c: (rb(i, c), 0)),
                   st((ML_QK_INNER, ML_V_DIM)), st((ML_HEADS, ML_QK_DIM)), st((SUBLANES, LANES))],
        out_shape=[jax.ShapeDtypeStruct((B * T, ML_D_INNER), bf16),
                   jax.ShapeDtypeStruct((B, ML_QK_INNER, ML_V_DIM), f32),
                   jax.ShapeDtypeStruct((B, ML_HEADS, ML_QK_DIM), f32),
                   jax.ShapeDtypeStruct((B, SUBLANES, LANES), f32)],
        compiler_params=_params(2),
        name="mlstm",
    )(u, u, u, u, us, wts["i_bias"], wts["f_bias"], wts["ml_norm_w"], c0, n0, m0)


def _outproj_kernel(ys_ref, ym_ref, ws_ref, wm_ref, h_ref, o_ref, wsb_ref, wmb_ref):
    @pl.when(pl.program_id(1) == 0)
    def _():
        wsb_ref[...] = ws_ref[...].astype(bf16)
        wmb_ref[...] = wm_ref[...].astype(bf16)

    o_ref[...] = (h_ref[...]
                  + jnp.dot(ys_ref[...], wsb_ref[...], preferred_element_type=f32)
                  + jnp.dot(ym_ref[...], wmb_ref[...], preferred_element_type=f32))


def _outproj(ys, ym, w_out, h):
    M = h.shape[0]
    tm = _row_tile(M)
    tn = OUTPROJ_TILE
    return pl.pallas_call(
        _outproj_kernel,
        grid=(D_MODEL // tn, M // tm),
        in_specs=[pl.BlockSpec((tm, SSD_D_INNER), lambda j, i: (i, 0)),
                  pl.BlockSpec((tm, ML_D_INNER), lambda j, i: (i, 0)),
                  pl.BlockSpec((SSD_D_INNER, tn), lambda j, i: (0, j)),
                  pl.BlockSpec((ML_D_INNER, tn), lambda j, i: (1, j)),
                  pl.BlockSpec((tm, tn), lambda j, i: (i, j))],
        out_specs=pl.BlockSpec((tm, tn), lambda j, i: (i, j)),
        out_shape=jax.ShapeDtypeStruct((M, D_MODEL), f32),
        scratch_shapes=[pltpu.VMEM((SSD_D_INNER, tn), bf16), pltpu.VMEM((ML_D_INNER, tn), bf16)],
        compiler_params=_params(2),
        name="outproj",
    )(ys, ym, w_out, w_out, h)


def _ffn_kernel(h_ref, nw_ref, wg_ref, wv_ref, cwg_ref, cwv_ref, cbg_ref, cbv_ref, wd_ref, fw_ref, *rest,
                tm, tf, multi, blocks_per_seq):
    if multi:
        hg_ref, hv_ref, y_ref, tg_ref, tv_ref, xn_ref = rest
    else:
        hg_ref, hv_ref, y_ref, tg_ref, tv_ref, xn_ref, carg_ref, carv_ref = rest
    i = pl.program_id(0)
    j = pl.program_id(1)

    @pl.when(j == 0)
    def _():
        xn_ref[...] = _rms(h_ref[...], nw_ref[...]).astype(bf16)
        y_ref[...] = jnp.zeros_like(y_ref)

    xn = xn_ref[...]
    subs = [slice(c0, c0 + MXU_COLS) for c0 in range(0, tf, MXU_COLS)]
    up_dots = lambda cs: [jnp.dot(xn, w_ref[:, cs], preferred_element_type=f32) for w_ref in (wg_ref, wv_ref)]
    ups_next = up_dots(subs[0])
    for n, cs in enumerate(subs):
        ups = ups_next
        if n + 1 < len(subs):
            ups_next = up_dots(subs[n + 1])
        convd = []
        for half, (up, cw_ref, cb_ref) in enumerate(zip(ups, (cwg_ref, cwv_ref), (cbg_ref, cbv_ref))):
            if multi:
                s_ref, t_ref = ((hg_ref, tg_ref), (hv_ref, tv_ref))[half]
                prev = [s_ref[:, k:k + 1, cs] for k in range(FFN_CONV - 1)]
                convd.append(_causal_conv_seqs(up, prev, cw_ref[:, cs], cb_ref[:, cs]))
                up3 = up.reshape(tm // SUBLANES, SUBLANES, MXU_COLS)
                t_ref[:, :, cs] = up3[:, SUBLANES - (FFN_CONV - 1):, :]
            else:
                h_ref_, car_ref, t_ref = ((hg_ref, carg_ref, tg_ref), (hv_ref, carv_ref, tv_ref))[half]
                tail = up[tm - SUBLANES:, :]
                if blocks_per_seq == 1:
                    hist = h_ref_[0, :, cs]
                else:
                    hist = jnp.where((i % blocks_per_seq) == 0, h_ref_[0, :, cs], car_ref[j, :, cs])
                    car_ref[j, :, cs] = tail
                t_ref[0, :, cs] = tail
                convd.append(_causal_conv(up, hist, cw_ref[:, cs], cb_ref[:, cs]))
        act = (_silu(convd[0]) * convd[1]).astype(bf16)
        y_ref[...] += jnp.dot(act, wd_ref[cs, :], preferred_element_type=f32)

    @pl.when(j == pl.num_programs(1) - 1)
    def _():
        y_ref[...] = _rms(h_ref[...] + y_ref[...], fw_ref[...])


def _ffn(h, ffn0, wts, *, B, T):
    M = h.shape[0]
    tm = _row_tile(M)
    tf = FFN_TILE
    n_ff = D_FF // tf
    multi = T < tm
    const = lambda shape: pl.BlockSpec(shape, lambda i, j: (0,) * len(shape))
    if multi:
        assert T == SUBLANES
        blocks_per_seq = 1
        nseq = tm // T
        hist = [ffn0, ffn0]
        h_specs = [pl.BlockSpec((nseq, FFN_CONV - 1, tf), lambda i, j: (i, 0, j)),
                   pl.BlockSpec((nseq, FFN_CONV - 1, tf), lambda i, j: (i, 0, n_ff + j))]
        t_specs = [pl.BlockSpec((nseq, FFN_CONV - 1, tf), lambda i, j: (i, 0, j))] * 2
        t_shapes = [jax.ShapeDtypeStruct((B, FFN_CONV - 1, D_FF), f32)] * 2
        scratch = []
    else:
        blocks_per_seq = T // tm
        pad = jnp.pad(ffn0, ((0, 0), (SUBLANES - (FFN_CONV - 1), 0), (0, 0)))
        hist = [pad, pad]
        h_specs = [pl.BlockSpec((1, SUBLANES, tf), lambda i, j: (i // blocks_per_seq, 0, j)),
                   pl.BlockSpec((1, SUBLANES, tf), lambda i, j: (i // blocks_per_seq, 0, n_ff + j))]
        t_specs = [pl.BlockSpec((1, SUBLANES, tf), lambda i, j: (i, 0, j))] * 2
        t_shapes = [jax.ShapeDtypeStruct((M // tm, SUBLANES, D_FF), f32)] * 2
        scratch = [pltpu.VMEM((n_ff, SUBLANES, tf), f32)] * 2
    kern = functools.partial(_ffn_kernel, tm=tm, tf=tf, multi=multi, blocks_per_seq=blocks_per_seq)
    y, *tails = pl.pallas_call(
        kern,
        grid=(M // tm, n_ff),
        in_specs=[pl.BlockSpec((tm, D_MODEL), lambda i, j: (i, 0), pipeline_mode=pl.Buffered(1)),
                  const((1, D_MODEL)),
                  pl.BlockSpec((D_MODEL, tf), lambda i, j: (0, j)),
                  pl.BlockSpec((D_MODEL, tf), lambda i, j: (0, n_ff + j)),
                  pl.BlockSpec((FFN_CONV, tf), lambda i, j: (0, j)),
                  pl.BlockSpec((FFN_CONV, tf), lambda i, j: (0, n_ff + j)),
                  pl.BlockSpec((1, tf), lambda i, j: (0, j)),
                  pl.BlockSpec((1, tf), lambda i, j: (0, n_ff + j)),
                  pl.BlockSpec((tf, D_MODEL), lambda i, j: (j, 0)),
                  const((1, D_MODEL)),
                  *h_specs],
        out_specs=[pl.BlockSpec((tm, D_MODEL), lambda i, j: (i, 0)), *t_specs],
        out_shape=[jax.ShapeDtypeStruct((M, D_MODEL), f32), *t_shapes],
        scratch_shapes=[pltpu.VMEM((tm, D_MODEL), bf16), *scratch],
        compiler_params=_params(2),
        name="ffn",
    )(h, wts["norm2_w"], wts["w_up"], wts["w_up"], wts["ffn_cw"], wts["ffn_cw"],
      wts["ffn_cb"], wts["ffn_cb"], wts["w_down"], wts["final_norm_w"], *hist)
    if multi:
        return y, jnp.concatenate(tails, axis=-1)
    last =jnp.concatenate(tails, axis=-1).reshape(B, blocks_per_seq, SUBLANES, 2 * D_FF)
    return y, last[:, blocks_per_seq - 1, SUBLANES - (FFN_CONV - 1):, :]


def _hist_tile(state):
    return jnp.pad(state, ((0, 0), (SUBLANES - state.shape[1], 0), (0, 0)))


def _layer(h, states, wts, *, B, T, L, Ls):
    conv0, s0, c0, n0, m0, ffn0 = states
    u, us = _inproj(h, wts["norm1_w"], wts["w_main"], wts["w_small"])
    if L // Ls > 1:
        hx = hbc = conv0
    else:
        hx, hbc = _hist_tile(conv0[:, :, :SSD_D_INNER]), _hist_tile(conv0[:, :, SSD_D_INNER:])
    y_ssd, s_new = _ssd(u, us, hx, hbc, s0.reshape(B, SSD_D_INNER, SSD_STATE), wts, B=B, T=T, L=L, Ls=Ls)
    m_pad = jnp.broadcast_to(
        jnp.pad(m0, ((0, 0), (GATE_LANE, LANES - GATE_LANE - ML_HEADS)))[:, None, :], (B, SUBLANES, LANES))
    y_ml, c_new, n_new, m_new = _mlstm(u, us, c0.reshape(B, ML_QK_INNER, ML_V_DIM), n0, m_pad, wts,
                                       B=B, T=T, L=L, Ls=Ls)
    h1 = _outproj(y_ssd, y_ml, wts["w_out"], h)
    y, ffn_new = _ffn(h1, ffn0, wts, B=B, T=T)
    ur = u.reshape(B, T, U_MAIN)[:, T - (SSD_CONV - 1):, :]
    conv_new = jnp.concatenate([ur[:, :, U_X * SSD_D_INNER:(U_X + 1) * SSD_D_INNER],
                                ur[:, :, U_BC * SSD_BC:(U_BC + 1) * SSD_BC]], axis=-1)
    new_states = (conv_new,
                  s_new.reshape(B, SSD_HEADS, SSD_HEAD_DIM, SSD_STATE),
                  c_new.reshape(B, ML_HEADS, ML_QK_DIM, ML_V_DIM),
                  n_new,
                  m_new[:, 0, GATE_LANE:GATE_LANE + ML_HEADS],
                  ffn_new)
    return y, new_states


def _repack_kernel(w_ref, o_ref):
    o_ref[...] = w_ref[...].astype(bf16)


def _repack_w_in(w_t, offs):
    rows = REPACK_ROWS
    bounds, dst = [], 0
    for name in ("z", "x", "v", "o", "q", "k", "bc"):
        src, width = offs[name]
        assert width % rows == 0 and dst % rows == 0
        bounds.append((dst // rows, src))
        dst += width

    unit = 2 * SUBLANES
    assert all(src % unit == 0 for _, src in bounds)

    def src_row(i):
        row = jnp.int32(0)
        for first_blk, src in bounds:
            row = jnp.where(i >= first_blk, src // unit + (i - first_blk) * (rows // unit), row)
        return row * unit

    return pl.pallas_call(
        _repack_kernel,
        grid=(U_MAIN // rows,),
        in_specs=[pl.BlockSpec((pl.Element(rows), pl.Element(D_MODEL)), lambda i: (src_row(i), 0))],
        out_specs=pl.BlockSpec((rows, D_MODEL), lambda i: (i, 0)),
        out_shape=jax.ShapeDtypeStruct((U_MAIN, D_MODEL), bf16),
        compiler_params=_params(1),
        name="repack_w_in",
    )(w_t)


def _prep_weights(norm1_w, w_in, ssd_conv_w, ssd_conv_b, ssd_dt_bias, ssd_A_log, ssd_D, ssd_norm_w,
                  ml_i_bias, ml_f_bias, ml_norm_w, w_out, norm2_w, w_up, ffn_conv_w, ffn_conv_b, w_down,
                  final_norm_w):
    w_t = w_in.T
    o = 0
    rows, offs = {}, {}
    for name, width in (("z", SSD_D_INNER), ("x", SSD_D_INNER), ("bc", SSD_BC), ("dt", SSD_HEADS),
                        ("q", ML_QK_INNER), ("k", ML_QK_INNER), ("v", ML_D_INNER), ("i", ML_HEADS),
                        ("f", ML_HEADS), ("o", ML_D_INNER)):
        offs[name] = (o, width)
        if name in ("dt", "i", "f"):
            rows[name] = w_t[o:o + width, :].astype(bf16)
        o += width
    w_main = _repack_w_in(w_t, offs)
    zpad = lambda n: jnp.zeros((n, D_MODEL), bf16)
    w_small = jnp.concatenate([rows["dt"], rows["i"], zpad(LANES - GATE_LANE - ML_HEADS),
                               zpad(GATE_LANE), rows["f"], zpad(LANES - GATE_LANE - ML_HEADS)], axis=0)
    lane_row = lambda v, off: jnp.pad(v.astype(f32), (off, LANES - off - v.shape[0]))[None, :]
    hp = jnp.arange(SSD_D_INNER) // SSD_HEAD_DIM
    ehp = (jnp.arange(LANES)[:, None] == hp[None, :]).astype(bf16)
    return dict(
        norm1_w=norm1_w[None, :], w_main=w_main, w_small=w_small,
        cw_x=ssd_conv_w[:, :SSD_D_INNER], cw_bc=ssd_conv_w[:, SSD_D_INNER:],
        cb_x=ssd_conv_b[None, :SSD_D_INNER], cb_bc=ssd_conv_b[None, SSD_D_INNER:],
        dt_bias=lane_row(ssd_dt_bias, 0), a_log=lane_row(ssd_A_log, 0),
        d_exp=jnp.repeat(ssd_D.astype(f32), SSD_HEAD_DIM)[None, :], ssd_norm_w=ssd_norm_w[None, :],
        ehp=ehp, ehpt=ehp.T,
        i_bias=lane_row(ml_i_bias, GATE_LANE), f_bias=lane_row(ml_f_bias, GATE_LANE),
        ml_norm_w=ml_norm_w[None, :],
        w_out=w_out, norm2_w=norm2_w[None, :], w_up=w_up.astype(bf16),
        ffn_cw=ffn_conv_w, ffn_cb=ffn_conv_b[None, :], w_down=w_down.astype(bf16),
        final_norm_w=final_norm_w[None, :])


def kernel(x_prompt, x_sample, state_ssd_conv, state_ssd, state_mlstm_C, state_mlstm_n, state_mlstm_m,
           state_ffn_conv, meta_tokens, norm1_w, w_in, ssd_conv_w, ssd_conv_b, ssd_dt_bias, ssd_A_log,
           ssd_D, ssd_norm_w, ml_i_bias, ml_f_bias, ml_norm_w, w_out, norm2_w, w_up, ffn_conv_w,
           ffn_conv_b, w_down, final_norm_w):
    depth = w_in.shape[0]
    assert depth == 1, "single-layer step"
    Bp, Tp, _ = x_prompt.shape
    Bs, Ts, _ = x_sample.shape
    wts = _prep_weights(norm1_w[0], w_in[0], ssd_conv_w[0], ssd_conv_b[0], ssd_dt_bias[0], ssd_A_log[0],
                        ssd_D[0], ssd_norm_w[0], ml_i_bias[0], ml_f_bias[0], ml_norm_w[0], w_out[0],
                        norm2_w[0], w_up[0], ffn_conv_w[0], ffn_conv_b[0], w_down[0], final_norm_w)
    zero_states = (jnp.zeros((1, SSD_CONV - 1, SSD_CONV_DIM), f32),
                   jnp.zeros((1, SSD_HEADS, SSD_HEAD_DIM, SSD_STATE), f32),
                   jnp.zeros((1, ML_HEADS, ML_QK_DIM, ML_V_DIM), f32),
                   jnp.zeros((1, ML_HEADS, ML_QK_DIM), f32),
                   jnp.zeros((1, ML_HEADS), f32),
                   jnp.zeros((1, FFN_CONV - 1, 2 * D_FF), f32))
    _, meta_states = _layer(meta_tokens.astype(f32), zero_states, wts, B=1, T=N_META, L=N_META, Ls=N_META)
    p_init = tuple(jnp.broadcast_to(s, (Bp,) + s.shape[1:]) for s in meta_states)
    yp, p_new = _layer(x_prompt.reshape(Bp * Tp, D_MODEL), p_init, wts, B=Bp, T=Tp, L=128, Ls=128)
    s_init = (state_ssd_conv[0], state_ssd[0], state_mlstm_C[0], state_mlstm_n[0], state_mlstm_m[0],
              state_ffn_conv[0])
    ys, s_new = _layer(x_sample.reshape(Bs * Ts, D_MODEL), s_init, wts, B=Bs, T=Ts, L=8 * Ts, Ls=Ts)
    return (yp.reshape(Bp, Tp, D_MODEL), ys.reshape(Bs, Ts, D_MODEL),
            *(s[None] for s in p_new), *(s[None] for s in s_new))
```

```python
import functools

import jax
import jax.numpy as jnp
from jax import lax
from jax.experimental import pallas as pl
from jax.experimental.pallas import tpu as pltpu

f32 = jnp.float32
bf16 = jnp.bfloat16

D_MODEL = 2048
N_META = 16
SSD_HEADS = 32
SSD_HEAD_DIM = 64
SSD_D_INNER = SSD_HEADS * SSD_HEAD_DIM
SSD_GROUPS = 2
SSD_STATE = 128
SSD_CONV = 4
SSD_BC = 2 * SSD_GROUPS * SSD_STATE
SSD_CONV_DIM = SSD_D_INNER + SSD_BC
ML_HEADS = 8
ML_QK_DIM = 128
ML_V_DIM = 256
ML_QK_INNER = ML_HEADS * ML_QK_DIM
ML_D_INNER = ML_HEADS * ML_V_DIM
D_FF = 5632
FFN_CONV = 3
EPS = 1e-6
NEG = -1e30

LANES = 128
SUBLANES = 8
VMEM_LIMIT = 56 * 1024 * 1024
MXU_COLS = 256
ROW_TILE = 1024
FFN_TILE = 512
INPROJ_TILE = 1536
OUTPROJ_TILE = 512
REPACK_ROWS = 256
SCAN_GROUPS = 2


def _row_tile(M):
    return ROW_TILE if M % ROW_TILE == 0 else M


U_Z, U_X, U_V, U_O = 0, 1, 2, 3
U_Q, U_K = 8, 9
U_BC = 20
U_MAIN = 4 * 2048 + 2 * 1024 + 512
GATE_LANE = 32
U_SMALL = 2 * LANES


def _dot(a, b):
    return jnp.dot(a.astype(bf16), b.astype(bf16), preferred_element_type=f32)


def _dot_nt(a, b):
    return lax.dot_general(a.astype(bf16), b.astype(bf16), (((1,), (1,)), ((), ())),
                           preferred_element_type=f32)


def _dot_tn(a, b):
    return lax.dot_general(a.astype(bf16), b.astype(bf16), (((0,), (0,)), ((), ())),
                           preferred_element_type=f32)


def _split3(a):
    hi = a.astype(bf16)
    r1 = a - hi.astype(f32)
    mid = r1.astype(bf16)
    lo = (r1 - mid.astype(f32)).astype(bf16)
    return hi, mid, lo


def _sel_right(a, e01):
    hi, mid, lo = _split3(a)
    d = lambda p: jnp.dot(p, e01, preferred_element_type=f32)
    return d(hi) + d(mid) + d(lo)


def _sel_left(e01, a):
    hi, mid, lo = _split3(a)
    d = lambda p: jnp.dot(e01, p, preferred_element_type=f32)
    return d(hi) + d(mid) + d(lo)


def _expand_heads(a, e01):
    hi = a.astype(bf16)
    mid = (a - hi.astype(f32)).astype(bf16)
    return (jnp.dot(hi, e01, preferred_element_type=f32)
            + jnp.dot(mid, e01, preferred_element_type=f32))


def _rowsum_mxu(a):
    ones = jnp.ones((a.shape[1], LANES), bf16)
    hi = a.astype(bf16)
    mid = (a - hi.astype(f32)).astype(bf16)
    return (jnp.dot(hi, ones, preferred_element_type=f32)
            + jnp.dot(mid, ones, preferred_element_type=f32))


def _transpose_exact(a, eye):
    hi, mid, lo = _split3(a)
    d = lambda p: lax.dot_general(eye, p, (((1,), (1,)), ((), ())), preferred_element_type=f32)
    return d(hi) + d(mid) + d(lo)


def _iota2(shape, axis):
    return lax.broadcasted_iota(jnp.int32, shape, axis)


def _as01(m):
    return jnp.where(m, 1.0, 0.0).astype(bf16)


def _eye():
    return _as01(_iota2((LANES, LANES), 0) == _iota2((LANES, LANES), 1))


def _seq_masks(L, Ls):
    t = _iota2((L, L), 0)
    s = _iota2((L, L), 1)
    shift = Ls.bit_length() - 1
    same = (t >> shift) == (s >> shift)
    causal = same & (s <= t)
    causal_t = same & (t <= s)
    last = s == (t | (Ls - 1))
    return causal, _as01(causal), _as01(causal_t), _as01(last)


def _sigmoid(x):
    return 1.0 / (1.0 + jnp.exp(-x))


def _silu(x):
    return x * _sigmoid(x)


def _softplus(x):
    return jnp.maximum(x, 0.0) + jnp.log(1.0 + jnp.exp(-jnp.abs(x)))


def _rms(x, w):
    r = lax.rsqrt(jnp.mean(x * x, axis=-1, keepdims=True) + EPS)
    return (x * r) * w


def _causal_conv(x, hist, w, b):
    L, C = x.shape
    K = w.shape[0]
    r = _iota2((SUBLANES, C), 0)
    y = b + x * w[K - 1:K, :]
    for s in range(1, K):
        zt = jnp.where(r >= SUBLANES - s, hist, x[L - SUBLANES:, :])
        z = zt if L == SUBLANES else jnp.concatenate([x[:L - SUBLANES, :], zt], axis=0)
        y = y + pltpu.roll(z, s, 0) * w[K - 1 - s:K - s, :]
    return y


def _causal_conv_seqs(x, prev, w, b):
    L, C = x.shape
    K = w.shape[0]
    nseq = L // SUBLANES
    r = _iota2((L, C), 0) & (SUBLANES - 1)
    per_row = lambda a: jnp.broadcast_to(a, (nseq, SUBLANES, C)).reshape(L, C)
    y = b + x * w[K - 1:K, :]
    for s in range(1, K):
        head = per_row(prev[K - 1 - s])
        for rr in range(1, s):
            head = jnp.where(r == rr, per_row(prev[K - 1 - s + rr]), head)
        y = y + jnp.where(r >= s, pltpu.roll(x, s, 0), head) * w[K - 1 - s:K - s, :]
    return y


def _params(n_axes):
    return pltpu.CompilerParams(dimension_semantics=("arbitrary",) * n_axes,
                                vmem_limit_bytes=VMEM_LIMIT)


def _inproj_kernel(x_ref, nw_ref, w_ref, ws_ref, u_ref, us_ref, xn_ref):
    @pl.when(pl.program_id(1) == 0)
    def _():
        xn = _rms(x_ref[...], nw_ref[...]).astype(bf16)
        xn_ref[...] = xn
        us_ref[...] = _dot_nt(xn, ws_ref[...])

    u_ref[...] = _dot_nt(xn_ref[...], w_ref[...])


def _inproj(x, nw, w_main, w_small):
    M = x.shape[0]
    tm = _row_tile(M)
    tn = INPROJ_TILE
    return pl.pallas_call(
        _inproj_kernel,
        grid=(M // tm, U_MAIN // tn),
        in_specs=[pl.BlockSpec((tm, D_MODEL), lambda i, j: (i, 0)),
                  pl.BlockSpec((1, D_MODEL), lambda i, j: (0, 0)),
                  pl.BlockSpec((tn, D_MODEL), lambda i, j: (j, 0)),
                  pl.BlockSpec((U_SMALL, D_MODEL), lambda i, j: (0, 0))],
        out_specs=[pl.BlockSpec((tm, tn), lambda i, j: (i, j)),
                   pl.BlockSpec((tm, U_SMALL), lambda i, j: (i, 0))],
        out_shape=[jax.ShapeDtypeStruct((M, U_MAIN), f32),
                   jax.ShapeDtypeStruct((M, U_SMALL), f32)],
        scratch_shapes=[pltpu.VMEM((tm, D_MODEL), bf16)],
        compiler_params=_params(2),
        name="inproj",
    )(x, nw, w_main, w_small)


def _ssd_kernel(z_ref, x_ref, bc_ref, sm_ref, hx_ref, hbc_ref, cwx_ref, cwbc_ref, cbx_ref, cbbc_ref,
                dtb_ref, alog_ref, dexp_ref, nw_ref, ehp_ref, ehpt_ref, s0_ref,
                y_ref, s_ref, tailx_ref, tailbc_ref, yz_ref, *, L, Ls, n_chunks):
    nseq = L // Ls
    c = pl.program_id(1)
    xpre = x_ref[...]
    bcpre = bc_ref[...]
    if nseq > 1:
        prev_x = [hx_ref[:, k:k + 1, :] for k in range(SSD_CONV - 1)]
        prev_bc = [hbc_ref[:, k:k + 1, :] for k in range(SSD_CONV - 1)]
        xc = _silu(_causal_conv_seqs(xpre, prev_x, cwx_ref[...], cbx_ref[...]))
        bcc = _silu(_causal_conv_seqs(bcpre, prev_bc, cwbc_ref[...], cbbc_ref[...]))
    else:
        if n_chunks == 1:
            hx, hbc = hx_ref[0], hbc_ref[0]
        else:
            first = c == 0
            hx = jnp.where(first, hx_ref[0], tailx_ref[...])
            hbc = jnp.where(first, hbc_ref[0], tailbc_ref[...])
        xc = _silu(_causal_conv(xpre, hx, cwx_ref[...], cbx_ref[...]))
        bcc = _silu(_causal_conv(bcpre, hbc, cwbc_ref[...], cbbc_ref[...]))
    if n_chunks > 1:
        tailx_ref[...] = xpre[L - SUBLANES:, :]
        tailbc_ref[...] = bcpre[L - SUBLANES:, :]

    lane = _iota2((L, LANES), 1)
    dt = jnp.where(lane < SSD_HEADS, _softplus(sm_ref[:, :LANES] + dtb_ref[...]), 0.0)
    dta = dt * (-jnp.exp(alog_ref[...]))

    causal, tri, tri_t, last = _seq_masks(L, Ls)
    eye = _eye()
    cum = _sel_left(tri, dta)
    cum_t = _sel_right(_transpose_exact(dta, eye), tri_t)
    cum_last = _sel_left(last, cum)
    ehp = ehp_ref[...]
    xdt = xc * _expand_heads(dt, ehp)
    xdtw = xdt * jnp.exp(_expand_heads(cum_last - cum, ehp))
    ecum = jnp.exp(_expand_heads(cum, ehp))

    if n_chunks > 1:
        @pl.when(c == 0)
        def _():
            s_ref[...] = s0_ref[...]
        sprev_ref = s_ref
    else:
        sprev_ref = s0_ref

    GE = SSD_D_INNER // SSD_GROUPS
    HPG = SSD_HEADS // SSD_GROUPS
    shift = Ls.bit_length() - 1
    seq_of_row = _iota2((L, 1), 0) >> shift
    lane_lo = lane < SSD_HEAD_DIM
    ys, intra = [], []
    for g in range(SSD_GROUPS):
        bm = bcc[:, g * SSD_STATE:(g + 1) * SSD_STATE]
        cm = bcc[:, (SSD_GROUPS + g) * SSD_STATE:(SSD_GROUPS + g + 1) * SSD_STATE]
        cb = _dot_nt(cm, bm)
        acc = None
        for j in range(nseq):
            cmj = cm if nseq == 1 else jnp.where(seq_of_row == j, cm, 0.0)
            t = _dot_nt(cmj, sprev_ref[j, g * GE:(g + 1) * GE, :])
            acc = t if acc is None else acc + t
        ys.append(acc)
        for hp in range(HPG // 2):
            col0 = g * GE + hp * LANES
            xpair = xdt[:, col0:col0 + LANES].astype(bf16)
            for e in range(2):
                h = g * HPG + hp * 2 + e
                seg = cum[:, h:h + 1] - cum_t[h:h + 1, :]
                m = jnp.exp(jnp.where(causal, seg, NEG)) * cb
                intra.append(jnp.dot(m.astype(bf16), xpair, preferred_element_type=f32))
    ssq = jnp.zeros((L, 1), f32)
    for g in range(SSD_GROUPS):
        for hp in range(HPG // 2):
            col0 = g * GE + hp * LANES
            blk = slice(col0, col0 + LANES)
            o0, o1 = intra[col0 // LANES * 2], intra[col0 // LANES * 2 + 1]
            y = (jnp.where(lane_lo, o0, o1)
                 + ecum[:, blk] * ys[g][:, hp * LANES:(hp + 1) * LANES]
                 + dexp_ref[:, blk] * xc[:, blk])
            yz = y * _silu(z_ref[:, blk])
            ssq = ssq + jnp.sum(yz * yz, axis=-1, keepdims=True)
            yz_ref[:, blk] = yz
    r = lax.rsqrt(ssq * (1.0 / SSD_D_INNER) + EPS)
    y_ref[...] = ((yz_ref[...] * r) * nw_ref[...]).astype(bf16)

    ehpt = ehpt_ref[...]
    for j in range(nseq):
        tl = (j + 1) * Ls - 1
        dec = jnp.exp(jnp.broadcast_to(cum_t[:, tl:tl + 1], (LANES, LANES)))
        dec_rows = _sel_left(ehpt, dec)
        for g in range(SSD_GROUPS):
            bm = bcc[:, g * SSD_STATE:(g + 1) * SSD_STATE]
            xw = xdtw[:, g * GE:(g + 1) * GE]
            if nseq > 1:
                xw = jnp.where(seq_of_row == j, xw, 0.0)
            ds = _dot_tn(xw, bm)
            rs = slice(g * GE, (g + 1) * GE)
            s_ref[j, rs, :] = dec_rows[rs, :] * sprev_ref[j, rs, :] + ds


def _ssd(u, us, hx, hbc, s0, wts, *, B, T, L, Ls):
    nseq = L // Ls
    n_chunks = T // Ls if nseq == 1 else 1
    nblk = B // nseq
    rb = lambda i, c: i * n_chunks + c
    if nseq > 1:
        assert Ls == SUBLANES
        h_specs = [pl.BlockSpec((nseq, SSD_CONV - 1, SSD_D_INNER), lambda i, c: (i, 0, 0)),
                   pl.BlockSpec((nseq, SSD_CONV - 1, SSD_BC), lambda i, c: (i, 0, SSD_D_INNER // SSD_BC))]
    else:
        h_specs = [pl.BlockSpec((1, SUBLANES, SSD_D_INNER), lambda i, c: (i, 0, 0)),
                   pl.BlockSpec((1, SUBLANES, SSD_BC), lambda i, c: (i, 0, 0))]
    const = lambda shape: pl.BlockSpec(shape, lambda i, c: (0,) * len(shape))
    kern = functools.partial(_ssd_kernel, L=L, Ls=Ls, n_chunks=n_chunks)
    return pl.pallas_call(
        kern,
        grid=(nblk, n_chunks),
        in_specs=[pl.BlockSpec((L, SSD_D_INNER), lambda i, c: (rb(i, c), U_Z)),
                  pl.BlockSpec((L, SSD_D_INNER), lambda i, c: (rb(i, c), U_X)),
                  pl.BlockSpec((L, SSD_BC), lambda i, c: (rb(i, c), U_BC)),
                  pl.BlockSpec((L, U_SMALL), lambda i, c: (rb(i, c), 0)),
                  *h_specs,
                  const((SSD_CONV, SSD_D_INNER)), const((SSD_CONV, SSD_BC)),
                  const((1, SSD_D_INNER)), const((1, SSD_BC)),
                  const((1, LANES)), const((1, LANES)),
                  const((1, SSD_D_INNER)), const((1, SSD_D_INNER)),
                  const((LANES, SSD_D_INNER)), const((SSD_D_INNER, LANES)),
                  pl.BlockSpec((nseq, SSD_D_INNER, SSD_STATE), lambda i, c: (i, 0, 0))],
        out_specs=[pl.BlockSpec((L, SSD_D_INNER), lambda i, c: (rb(i, c), 0)),
                   pl.BlockSpec((nseq, SSD_D_INNER, SSD_STATE), lambda i, c: (i, 0, 0))],
        out_shape=[jax.ShapeDtypeStruct((B * T, SSD_D_INNER), bf16),
                   jax.ShapeDtypeStruct((B, SSD_D_INNER, SSD_STATE), f32)],
        scratch_shapes=[pltpu.VMEM((SUBLANES, SSD_D_INNER), f32),
                        pltpu.VMEM((SUBLANES, SSD_BC), f32),
                        pltpu.VMEM((L, SSD_D_INNER), f32)],
        compiler_params=_params(2),
        name="ssd",
    )(u, u, u, us, hx, hbc, wts["cw_x"], wts["cw_bc"], wts["cb_x"], wts["cb_bc"],
      wts["dt_bias"], wts["a_log"], wts["d_exp"], wts["ssd_norm_w"], wts["ehp"], wts["ehpt"], s0)


def _interleave(gens):
    while gens:
        alive = []
        for g in gens:
            try:
                next(g)
                alive.append(g)
            except StopIteration:
                pass
        gens = alive


def _group_views(refs, g, nseq):
    return [r.at[g] if kind == "tok" else r.at[pl.ds(g * nseq, nseq)] if kind == "state" else r
            for r, kind in refs]


def _mlstm_kernel(*refs, L, Ls, n_chunks, groups):
    kinds = ["tok"] * 5 + ["const"] * 3 + ["state"] * 3 + ["tok"] + ["state"] * 3
    _interleave([_mlstm_chunk(*_group_views(list(zip(refs, kinds)), g, L // Ls), L=L, Ls=Ls, n_chunks=n_chunks)
                 for g in range(groups)])


def _mlstm_chunk(q_ref, k_ref, v_ref, o_ref, sm_ref, ib_ref, fb_ref, nw_ref, c0_ref, n0_ref, m0_ref,
                 y_ref, c_ref, n_ref, m_ref, *, L, Ls, n_chunks):
    nseq = L // Ls
    c = pl.program_id(1)
    if n_chunks > 1:
        @pl.when(c == 0)
        def _():
            c_ref[...] = c0_ref[...]
            n_ref[...] = n0_ref[...]
            m_ref[...] = m0_ref[...]
        cprev_ref, nprev_ref, mprev_ref = c_ref, n_ref, m_ref
    else:
        cprev_ref, nprev_ref, mprev_ref = c0_ref, n0_ref, m0_ref

    per_tok = lambda a: jnp.broadcast_to(a, (nseq, Ls, LANES)).reshape(L, LANES)
    ig = sm_ref[:, :LANES] + ib_ref[...]
    fraw = sm_ref[:, LANES:] + fb_ref[...]
    lf = -_softplus(-fraw)
    causal, tri, _, last = _seq_masks(L, Ls)
    eye = _eye()
    F = _sel_left(tri, lf)
    FL = _sel_left(last, F)
    mp = per_tok(mprev_ref[:, 0:1, :])
    r_t = _transpose_exact(ig - F, eye)
    inter = F + mp
    lw = FL - F + ig
    segmax = jnp.max(lw.reshape(nseq, Ls, LANES), axis=1, keepdims=True)
    m_new = jnp.maximum(FL + mp, per_tok(segmax))
    sc = jnp.exp(lw - m_new)
    dec = jnp.exp(FL + mp - m_new)
    m_out = m_new.reshape(nseq, Ls, LANES)[:, 0:SUBLANES, :]
    yield

    shift = Ls.bit_length() - 1
    seq_of_row = _iota2((L, 1), 0) >> shift
    kscale = ML_QK_DIM ** -0.5
    heads = range(ML_HEADS)
    qcols = lambda h: slice(h * ML_QK_DIM, (h + 1) * ML_QK_DIM)
    vcols = lambda h: slice(h * ML_V_DIM, (h + 1) * ML_V_DIM)
    q = [q_ref[:, qcols(h)] for h in heads]
    k = [k_ref[:, qcols(h)] * kscale for h in heads]
    v = [v_ref[:, vcols(h)].astype(bf16) for h in heads]
    qk = [_dot_nt(q[h], k[h]) for h in heads]
    yield
    qc = []
    for h in heads:
        acc = None
        for j in range(nseq):
            qj = q[h] if nseq == 1 else jnp.where(seq_of_row == j, q[h], 0.0)
            t = _dot(qj, cprev_ref[j, qcols(h), :])
            acc = t if acc is None else acc + t
        qc.append(acc)
    yield
    rep = lambda col: jnp.broadcast_to(col, (L, LANES))
    twice = lambda a: jnp.concatenate([a, a], axis=1)
    qn = []
    for h in heads:
        acc = None
        for j in range(nseq):
            qj = q[h] if nseq == 1 else jnp.where(seq_of_row == j, q[h], 0.0)
            t = _dot_nt(qj, jnp.broadcast_to(nprev_ref[j, h:h + 1, :], (LANES, ML_QK_DIM)))
            acc = t if acc is None else acc + t
        qn.append(acc)
    yield
    dm, m_rep, inter_rep = [], [], []
    for h in heads:
        gl = GATE_LANE + h
        d = jnp.where(causal, F[:, gl:gl + 1] + r_t[gl:gl + 1, :], NEG)
        dm.append(d)
        inter_rep.append(rep(inter[:, gl:gl + 1]))
        m_rep.append(jnp.maximum(rep(jnp.max(d, axis=-1, keepdims=True)), inter_rep[h]))
    yield
    w = [jnp.exp(dm[h] - (m_rep[h] if L == LANES else m_rep[h][:, 0:1])) * qk[h] for h in heads]
    yield
    wv = [jnp.dot(w[h].astype(bf16), v[h], preferred_element_type=f32) for h in heads]
    wsum = [_rowsum_mxu(w[h]) for h in heads]
    yield
    hh = []
    for h in heads:
        wi = jnp.exp(inter_rep[h] - m_rep[h])
        den = wsum[h] + wi * qn[h]
        inv = 1.0 / jnp.maximum(jnp.abs(den), jnp.exp(-m_rep[h]))
        hh.append((wv[h] + twice(wi) * qc[h]) * twice(inv))
    yield
    ssq = [_rowsum_mxu(hh[h] * hh[h]) for h in heads]
    yield
    for h in heads:
        r = lax.rsqrt(ssq[h] * (1.0 / ML_V_DIM) + EPS)
        hn = (hh[h] * twice(r)) * nw_ref[:, vcols(h)]
        y_ref[:, vcols(h)] = (_sigmoid(o_ref[:, vcols(h)]) * hn).astype(bf16)
        if h % 2 == 1:
            yield
    for h in heads:
        gl = GATE_LANE + h
        ksc = k[h] * sc[:, gl:gl + 1]
        for j in range(nseq):
            r0 = j * Ls
            dj = dec[r0:r0 + 1, gl:gl + 1]
            kj = ksc if nseq == 1 else jnp.where(seq_of_row == j, ksc, 0.0)
            c_ref[j, qcols(h), :] = dj * cprev_ref[j, qcols(h), :] + _dot_tn(kj, v[h])
            n_ref[j, h:h + 1, :] = (dj * nprev_ref[j, h:h + 1, :]
                                    + jnp.sum(ksc[r0:r0 + Ls, :], axis=0, keepdims=True))
        if h % 2 == 1:
            yield
    m_ref[...] = m_out


def _scan_groups(nblk, nseq):
    return SCAN_GROUPS if nseq == 1 and nblk % SCAN_GROUPS == 0 else 1


def _mlstm(u, us, c0, n0, m0, wts, *, B, T, L, Ls):
    nseq = L // Ls
    n_chunks = T // Ls if nseq == 1 else 1
    nblk = B // nseq
    G = _scan_groups(nblk, nseq)
    rows = n_chunks * L
    u3, us3 = u.reshape(nblk, rows, U_MAIN), us.reshape(nblk, rows, U_SMALL)
    tok = lambda width, col: pl.BlockSpec((G, L, width), lambda i, c: (i, c, col))
    const = lambda shape: pl.BlockSpec(shape, lambda i, c: (0,) * len(shape))
    st = lambda shape: pl.BlockSpec((G * nseq,) + shape, lambda i, c: (i, 0, 0))
    kern = functools.partial(_mlstm_kernel, L=L, Ls=Ls, n_chunks=n_chunks, groups=G)
    y, c_new, n_new, m_new = pl.pallas_call(
        kern,
        grid=(nblk // G, n_chunks),
        in_specs=[tok(ML_QK_INNER, U_Q), tok(ML_QK_INNER, U_K), tok(ML_D_INNER, U_V), tok(ML_D_INNER, U_O),
                  tok(U_SMALL, 0),
                  const((1, LANES)), const((1, LANES)), const((1, ML_D_INNER)),
                  st((ML_QK_INNER, ML_V_DIM)), st((ML_HEADS, ML_QK_DIM)), st((SUBLANES, LANES))],
        out_specs=[tok(ML_D_INNER, 0),
                   st((ML_QK_INNER, ML_V_DIM)), st((ML_HEADS, ML_QK_DIM)), st((SUBLANES, LANES))],
        out_shape=[jax.ShapeDtypeStruct((nblk, rows, ML_D_INNER), bf16),
                   jax.ShapeDtypeStruct((B, ML_QK_INNER, ML_V_DIM), f32),
                   jax.ShapeDtypeStruct((B, ML_HEADS, ML_QK_DIM), f32),
                   jax.ShapeDtypeStruct((B, SUBLANES, LANES), f32)],
        compiler_params=_params(2),
        name="mlstm",
    )(u3, u3, u3, u3, us3, wts["i_bias"], wts["f_bias"], wts["ml_norm_w"], c0, n0, m0)
    return y.reshape(B * T, ML_D_INNER), c_new, n_new, m_new


def _outproj_kernel(ys_ref, ym_ref, ws_ref, wm_ref, h_ref, o_ref, wsb_ref, wmb_ref):
    @pl.when(pl.program_id(1) == 0)
    def _():
        wsb_ref[...] = ws_ref[...].astype(bf16)
        wmb_ref[...] = wm_ref[...].astype(bf16)

    o_ref[...] = (h_ref[...]
                  + jnp.dot(ys_ref[...], wsb_ref[...], preferred_element_type=f32)
                  + jnp.dot(ym_ref[...], wmb_ref[...], preferred_element_type=f32))


def _outproj(ys, ym, w_out, h):
    M = h.shape[0]
    tm = _row_tile(M)
    tn = OUTPROJ_TILE
    return pl.pallas_call(
        _outproj_kernel,
        grid=(D_MODEL // tn, M // tm),
        in_specs=[pl.BlockSpec((tm, SSD_D_INNER), lambda j, i: (i, 0)),
                  pl.BlockSpec((tm, ML_D_INNER), lambda j, i: (i, 0)),
                  pl.BlockSpec((SSD_D_INNER, tn), lambda j, i: (0, j)),
                  pl.BlockSpec((ML_D_INNER, tn), lambda j, i: (1, j)),
                  pl.BlockSpec((tm, tn), lambda j, i: (i, j))],
        out_specs=pl.BlockSpec((tm, tn), lambda j, i: (i, j)),
        out_shape=jax.ShapeDtypeStruct((M, D_MODEL), f32),
        scratch_shapes=[pltpu.VMEM((SSD_D_INNER, tn), bf16), pltpu.VMEM((ML_D_INNER, tn), bf16)],
        compiler_params=_params(2),
        name="outproj",
    )(ys, ym, w_out, w_out, h)


def _ffn_kernel(h_ref, nw_ref, wg_ref, wv_ref, cwg_ref, cwv_ref, cbg_ref, cbv_ref, wd_ref, fw_ref, *rest,
                tm, tf, multi, blocks_per_seq):
    if multi:
        hg_ref, hv_ref, y_ref, tg_ref, tv_ref, xn_ref = rest
    else:
        hg_ref, hv_ref, y_ref, tg_ref, tv_ref, xn_ref, carg_ref, carv_ref = rest
    i = pl.program_id(0)
    j = pl.program_id(1)

    @pl.when(j == 0)
    def _():
        xn_ref[...] = _rms(h_ref[...], nw_ref[...]).astype(bf16)
        y_ref[...] = jnp.zeros_like(y_ref)

    xn = xn_ref[...]
    subs = [slice(c0, c0 + MXU_COLS) for c0 in range(0, tf, MXU_COLS)]
    up_dots = lambda cs: [jnp.dot(xn, w_ref[:, cs], preferred_element_type=f32) for w_ref in (wg_ref, wv_ref)]
    ups_next = up_dots(subs[0])
    for n, cs in enumerate(subs):
        ups = ups_next
        if n + 1 < len(subs):
            ups_next = up_dots(subs[n + 1])
        convd = []
        for half, (up, cw_ref, cb_ref) in enumerate(zip(ups, (cwg_ref, cwv_ref), (cbg_ref, cbv_ref))):
            if multi:
                s_ref, t_ref = ((hg_ref, tg_ref), (hv_ref, tv_ref))[half]
                prev = [s_ref[:, k:k + 1, cs] for k in range(FFN_CONV - 1)]
                convd.append(_causal_conv_seqs(up, prev, cw_ref[:, cs], cb_ref[:, cs]))
                up3 = up.reshape(tm // SUBLANES, SUBLANES, MXU_COLS)
                t_ref[:, :, cs] = up3[:, SUBLANES - (FFN_CONV - 1):, :]
            else:
                h_ref_, car_ref, t_ref = ((hg_ref, carg_ref, tg_ref), (hv_ref, carv_ref, tv_ref))[half]
                tail = up[tm - SUBLANES:, :]
                if blocks_per_seq == 1:
                    hist = h_ref_[0, :, cs]
                else:
                    hist = jnp.where((i % blocks_per_seq) == 0, h_ref_[0, :, cs], car_ref[j, :, cs])
                    car_ref[j, :, cs] = tail
                t_ref[0, :, cs] = tail
                convd.append(_causal_conv(up, hist, cw_ref[:, cs], cb_ref[:, cs]))
        act = (_silu(convd[0]) * convd[1]).astype(bf16)
        y_ref[...] += jnp.dot(act, wd_ref[cs, :], preferred_element_type=f32)

    @pl.when(j == pl.num_programs(1) - 1)
    def _():
        y_ref[...] = _rms(h_ref[...] + y_ref[...], fw_ref[...])


def _ffn(h, ffn0, wts, *, B, T):
    M = h.shape[0]
    tm = _row_tile(M)
    tf = FFN_TILE
    n_ff = D_FF // tf
    multi = T < tm
    const = lambda shape: pl.BlockSpec(shape, lambda i, j: (0,) * len(shape))
    if multi:
        assert T == SUBLANES
        blocks_per_seq = 1
        nseq = tm // T
        hist = [ffn0, ffn0]
        h_specs = [pl.BlockSpec((nseq, FFN_CONV - 1, tf), lambda i, j: (i, 0, j)),
                   pl.BlockSpec((nseq, FFN_CONV - 1, tf), lambda i, j: (i, 0, n_ff + j))]
        t_specs = [pl.BlockSpec((nseq, FFN_CONV - 1, tf), lambda i, j: (i, 0, j))] * 2
        t_shapes = [jax.ShapeDtypeStruct((B, FFN_CONV - 1, D_FF), f32)] * 2
        scratch = []
    else:
        blocks_per_seq = T // tm
        pad = jnp.pad(ffn0, ((0, 0), (SUBLANES - (FFN_CONV - 1), 0), (0, 0)))
        hist = [pad, pad]
        h_specs = [pl.BlockSpec((1, SUBLANES, tf), lambda i, j: (i // blocks_per_seq, 0, j)),
                   pl.BlockSpec((1, SUBLANES, tf), lambda i, j: (i // blocks_per_seq, 0, n_ff + j))]
        t_specs = [pl.BlockSpec((1, SUBLANES, tf), lambda i, j: (i, 0, j))] * 2
        t_shapes = [jax.ShapeDtypeStruct((M // tm, SUBLANES, D_FF), f32)] * 2
        scratch = [pltpu.VMEM((n_ff, SUBLANES, tf), f32)] * 2
    kern = functools.partial(_ffn_kernel, tm=tm, tf=tf, multi=multi, blocks_per_seq=blocks_per_seq)
    y, *tails = pl.pallas_call(
        kern,
        grid=(M // tm, n_ff),
        in_specs=[pl.BlockSpec((tm, D_MODEL), lambda i, j: (i, 0), pipeline_mode=pl.Buffered(1)),
                  const((1, D_MODEL)),
                  pl.BlockSpec((D_MODEL, tf), lambda i, j: (0, j)),
                  pl.BlockSpec((D_MODEL, tf), lambda i, j: (0, n_ff + j)),
                  pl.BlockSpec((FFN_CONV, tf), lambda i, j: (0, j)),
                  pl.BlockSpec((FFN_CONV, tf), lambda i, j: (0, n_ff + j)),
                  pl.BlockSpec((1, tf), lambda i, j: (0, j)),
                  pl.BlockSpec((1, tf), lambda i, j: (0, n_ff + j)),
                  pl.BlockSpec((tf, D_MODEL), lambda i, j: (j, 0)),
                  const((1, D_MODEL)),
                  *h_specs],
        out_specs=[pl.BlockSpec((tm, D_MODEL), lambda i, j: (i, 0)), *t_specs],
        out_shape=[jax.ShapeDtypeStruct((M, D_MODEL), f32), *t_shapes],
        scratch_shapes=[pltpu.VMEM((tm, D_MODEL), bf16), *scratch],
        compiler_params=_params(2),
        name="ffn",
    )(h, wts["norm2_w"], wts["w_up"], wts["w_up"], wts["ffn_cw"], wts["ffn_cw"],
      wts["ffn_cb"], wts["ffn_cb"], wts["w_down"], wts["final_norm_w"], *hist)
    if multi:
        return y, jnp.concatenate(tails, axis=-1)
    last = jnp.concatenate(tails, axis=-1).reshape(B, blocks_per_seq, SUBLANES, 2 * D_FF)
    return y, last[:, blocks_per_seq - 1, SUBLANES - (FFN_CONV - 1):, :]


def _hist_tile(state):
    return jnp.pad(state, ((0, 0), (SUBLANES - state.shape[1], 0), (0, 0)))


def _layer(h, states, wts, *, B, T, L, Ls):
    conv0, s0, c0, n0, m0, ffn0 = states
    u, us = _inproj(h, wts["norm1_w"], wts["w_main"], wts["w_small"])
    if L // Ls > 1:
        hx = hbc = conv0
    else:
        hx, hbc = _hist_tile(conv0[:, :, :SSD_D_INNER]), _hist_tile(conv0[:, :, SSD_D_INNER:])
    y_ssd, s_new = _ssd(u, us, hx, hbc, s0.reshape(B, SSD_D_INNER, SSD_STATE), wts, B=B, T=T, L=L, Ls=Ls)
    m_pad = jnp.broadcast_to(
        jnp.pad(m0, ((0, 0), (GATE_LANE, LANES - GATE_LANE - ML_HEADS)))[:, None, :], (B, SUBLANES, LANES))
    y_ml, c_new, n_new, m_new = _mlstm(u, us, c0.reshape(B, ML_QK_INNER, ML_V_DIM), n0, m_pad, wts,
                                       B=B, T=T, L=L, Ls=Ls)
    h1 = _outproj(y_ssd, y_ml, wts["w_out"], h)
    y, ffn_new = _ffn(h1, ffn0, wts, B=B, T=T)
    ur = u.reshape(B, T, U_MAIN)[:, T - (SSD_CONV - 1):, :]
    conv_new = jnp.concatenate([ur[:, :, U_X * SSD_D_INNER:(U_X + 1) * SSD_D_INNER],
                                ur[:, :, U_BC * SSD_BC:(U_BC + 1) * SSD_BC]], axis=-1)
    new_states = (conv_new,
                  s_new.reshape(B, SSD_HEADS, SSD_HEAD_DIM, SSD_STATE),
                  c_new.reshape(B, ML_HEADS, ML_QK_DIM, ML_V_DIM),
                  n_new,
                  m_new[:, 0, GATE_LANE:GATE_LANE + ML_HEADS],
                  ffn_new)
    return y, new_states


def _repack_kernel(w_ref, o_ref):
    o_ref[...] = w_ref[...].astype(bf16)


def _repack_w_in(w_t, offs):
    rows = REPACK_ROWS
    bounds, dst = [], 0
    for name in ("z", "x", "v", "o", "q", "k", "bc"):
        src, width = offs[name]
        assert width % rows == 0 and dst % rows == 0
        bounds.append((dst // rows, src))
        dst += width

    unit = 2 * SUBLANES
    assert all(src % unit == 0 for _, src in bounds)

    def src_row(i):
        row = jnp.int32(0)
        for first_blk, src in bounds:
            row = jnp.where(i >= first_blk, src // unit + (i - first_blk) * (rows // unit), row)
        return row * unit

    return pl.pallas_call(
        _repack_kernel,
        grid=(U_MAIN // rows,),
        in_specs=[pl.BlockSpec((pl.Element(rows), pl.Element(D_MODEL)), lambda i: (src_row(i), 0))],
        out_specs=pl.BlockSpec((rows, D_MODEL), lambda i: (i, 0)),
        out_shape=jax.ShapeDtypeStruct((U_MAIN, D_MODEL), bf16),
        compiler_params=_params(1),
        name="repack_w_in",
    )(w_t)


def _prep_weights(norm1_w, w_in, ssd_conv_w, ssd_conv_b, ssd_dt_bias, ssd_A_log, ssd_D, ssd_norm_w,
                  ml_i_bias, ml_f_bias, ml_norm_w, w_out, norm2_w, w_up, ffn_conv_w, ffn_conv_b, w_down,
                  final_norm_w):
    w_t = w_in.T
    o = 0
    rows, offs = {}, {}
    for name, width in (("z", SSD_D_INNER), ("x", SSD_D_INNER), ("bc", SSD_BC), ("dt", SSD_HEADS),
                        ("q", ML_QK_INNER), ("k", ML_QK_INNER), ("v", ML_D_INNER), ("i", ML_HEADS),
                        ("f", ML_HEADS), ("o", ML_D_INNER)):
        offs[name] = (o, width)
        if name in ("dt", "i", "f"):
            rows[name] = lax.optimization_barrier(w_t[o:o + width, :]).astype(bf16)
        o += width
    w_main = _repack_w_in(w_t, offs)
    zpad = lambda n: jnp.zeros((n, D_MODEL), bf16)
    w_small = jnp.concatenate([rows["dt"], rows["i"], zpad(LANES - GATE_LANE - ML_HEADS),
                               zpad(GATE_LANE), rows["f"], zpad(LANES - GATE_LANE - ML_HEADS)], axis=0)
    lane_row = lambda v, off: jnp.pad(v.astype(f32), (off, LANES - off - v.shape[0]))[None, :]
    hp = jnp.arange(SSD_D_INNER) // SSD_HEAD_DIM
    ehp = (jnp.arange(LANES)[:, None] == hp[None, :]).astype(bf16)
    return dict(
        norm1_w=norm1_w[None, :], w_main=w_main, w_small=w_small,
        cw_x=ssd_conv_w[:, :SSD_D_INNER], cw_bc=ssd_conv_w[:, SSD_D_INNER:],
        cb_x=ssd_conv_b[None, :SSD_D_INNER], cb_bc=ssd_conv_b[None, SSD_D_INNER:],
        dt_bias=lane_row(ssd_dt_bias, 0), a_log=lane_row(ssd_A_log, 0),
        d_exp=jnp.repeat(ssd_D.astype(f32), SSD_HEAD_DIM)[None, :], ssd_norm_w=ssd_norm_w[None, :],
        ehp=ehp, ehpt=ehp.T,
        i_bias=lane_row(ml_i_bias, GATE_LANE), f_bias=lane_row(ml_f_bias, GATE_LANE),
        ml_norm_w=ml_norm_w[None, :],
        w_out=w_out, norm2_w=norm2_w[None, :], w_up=w_up.astype(bf16),
        ffn_cw=ffn_conv_w, ffn_cb=ffn_conv_b[None, :], w_down=w_down.astype(bf16),
        final_norm_w=final_norm_w[None, :])


def kernel(x_prompt, x_sample, state_ssd_conv, state_ssd, state_mlstm_C, state_mlstm_n, state_mlstm_m,
           state_ffn_conv, meta_tokens, norm1_w, w_in, ssd_conv_w, ssd_conv_b, ssd_dt_bias, ssd_A_log,
           ssd_D, ssd_norm_w, ml_i_bias, ml_f_bias, ml_norm_w, w_out, norm2_w, w_up, ffn_conv_w,
           ffn_conv_b, w_down, final_norm_w):
    depth = w_in.shape[0]
    assert depth == 1, "single-layer step"
    Bp, Tp, _ = x_prompt.shape
    Bs, Ts, _ = x_sample.shape
    wts = _prep_weights(norm1_w[0], w_in[0], ssd_conv_w[0], ssd_conv_b[0], ssd_dt_bias[0], ssd_A_log[0],
                        ssd_D[0], ssd_norm_w[0], ml_i_bias[0], ml_f_bias[0], ml_norm_w[0], w_out[0],
                        norm2_w[0], w_up[0], ffn_conv_w[0], ffn_conv_b[0], w_down[0], final_norm_w)
    zero_states = (jnp.zeros((1, SSD_CONV - 1, SSD_CONV_DIM), f32),
                   jnp.zeros((1, SSD_HEADS, SSD_HEAD_DIM, SSD_STATE), f32),
                   jnp.zeros((1, ML_HEADS, ML_QK_DIM, ML_V_DIM), f32),
                   jnp.zeros((1, ML_HEADS, ML_QK_DIM), f32),
                   jnp.zeros((1, ML_HEADS), f32),
                   jnp.zeros((1, FFN_CONV - 1, 2 * D_FF), f32))
    _, meta_states = _layer(meta_tokens.astype(f32), zero_states, wts, B=1, T=N_META, L=N_META, Ls=N_META)
    p_init = tuple(jnp.broadcast_to(s, (Bp,) + s.shape[1:]) for s in meta_states)
    yp, p_new = _layer(x_prompt.reshape(Bp * Tp, D_MODEL), p_init, wts, B=Bp, T=Tp, L=128, Ls=128)
    s_init = (state_ssd_conv[0], state_ssd[0], state_mlstm_C[0], state_mlstm_n[0], state_mlstm_m[0],
              state_ffn_conv[0])
    ys, s_new = _layer(x_sample.reshape(Bs * Ts, D_MODEL), s_init, wts, B=Bs, T=Ts, L=8 * Ts, Ls=Ts)
    return (yp.reshape(Bp, Tp, D_MODEL), ys.reshape(Bs, Ts, D_MODEL),
            *(s[None] for s in p_new), *(s[None] for s in s_new))
```

```python
import functools

import jax
import jax.numpy as jnp
from jax import lax
from jax.experimental import pallas as pl
from jax.experimental.pallas import tpu as pltpu

f32 = jnp.float32
bf16 = jnp.bfloat16

D_MODEL = 2048
N_META = 16
SSD_HEADS = 32
SSD_HEAD_DIM = 64
SSD_D_INNER = SSD_HEADS * SSD_HEAD_DIM
SSD_GROUPS = 2
SSD_STATE = 128
SSD_CONV = 4
SSD_BC = 2 * SSD_GROUPS * SSD_STATE
SSD_CONV_DIM = SSD_D_INNER + SSD_BC
ML_HEADS = 8
ML_QK_DIM = 128
ML_V_DIM = 256
ML_QK_INNER = ML_HEADS * ML_QK_DIM
ML_D_INNER = ML_HEADS * ML_V_DIM
D_FF = 5632
FFN_CONV = 3
EPS = 1e-6
NEG = -1e30

LANES = 128
SUBLANES = 8
VMEM_LIMIT = 56 * 1024 * 1024
MXU_COLS = 256
ROW_TILE = 1024
FFN_TILE = 512
INPROJ_TILE = 1536
OUTPROJ_TILE = 512
REPACK_ROWS = 256
SCAN_GROUPS = 2


def _row_tile(M):
    return ROW_TILE if M % ROW_TILE == 0 else M


U_Z, U_X, U_V, U_O = 0, 1, 2, 3
U_Q, U_K = 8, 9
U_BC = 20
U_MAIN = 4 * 2048 + 2 * 1024 + 512
GATE_LANE = 32
U_SMALL = 2 * LANES


def _dot(a, b):
    return jnp.dot(a.astype(bf16), b.astype(bf16), preferred_element_type=f32)


def _dot_nt(a, b):
    return lax.dot_general(a.astype(bf16), b.astype(bf16), (((1,), (1,)), ((), ())),
                           preferred_element_type=f32)


def _dot_tn(a, b):
    return lax.dot_general(a.astype(bf16), b.astype(bf16), (((0,), (0,)), ((), ())),
                           preferred_element_type=f32)


def _split3(a):
    hi = a.astype(bf16)
    r1 = a - hi.astype(f32)
    mid = r1.astype(bf16)
    lo = (r1 - mid.astype(f32)).astype(bf16)
    return hi, mid, lo


def _sel_right(a, e01):
    return jnp.dot(jnp.concatenate(_split3(a), axis=1), jnp.concatenate([e01] * 3, axis=0),
                   preferred_element_type=f32)


def _sel_left(e01, a):
    return jnp.dot(jnp.concatenate([e01] * 3, axis=1), jnp.concatenate(_split3(a), axis=0),
                   preferred_element_type=f32)


def _split2(a):
    hi = a.astype(bf16)
    mid = (a - hi.astype(f32)).astype(bf16)
    return jnp.concatenate([hi, mid], axis=1)


def _expand_heads(a, e01x2):
    return jnp.dot(_split2(a), e01x2, preferred_element_type=f32)


def _rowsum_mxu(a):
    return jnp.dot(_split2(a), jnp.ones((2 * a.shape[1], LANES), bf16), preferred_element_type=f32)


def _transpose_exact(a, eye):
    return lax.dot_general(jnp.concatenate([eye] * 3, axis=1), jnp.concatenate(_split3(a), axis=1),
                           (((1,), (1,)), ((), ())), preferred_element_type=f32)


def _iota2(shape, axis):
    return lax.broadcasted_iota(jnp.int32, shape, axis)


def _as01(m):
    return jnp.where(m, 1.0, 0.0).astype(bf16)


def _eye():
    return _as01(_iota2((LANES, LANES), 0) == _iota2((LANES, LANES), 1))


def _seq_masks(L, Ls):
    t = _iota2((L, L), 0)
    s = _iota2((L, L), 1)
    shift = Ls.bit_length() - 1
    same = (t >> shift) == (s >> shift)
    causal = same & (s <= t)
    causal_t = same & (t <= s)
    last = s == (t | (Ls - 1))
    return causal, _as01(causal), _as01(causal_t), _as01(last)


def _sigmoid(x):
    return 1.0 / (1.0 + jnp.exp(-x))


def _silu(x):
    return x * _sigmoid(x)


def _softplus(x):
    return jnp.maximum(x, 0.0) + jnp.log(1.0 + jnp.exp(-jnp.abs(x)))


def _rms(x, w):
    r = lax.rsqrt(jnp.mean(x * x, axis=-1, keepdims=True) + EPS)
    return (x * r) * w


def _causal_conv(x, hist, w, b):
    L, C = x.shape
    K = w.shape[0]
    r = _iota2((SUBLANES, C), 0)
    y = b + x * w[K - 1:K, :]
    for s in range(1, K):
        zt = jnp.where(r >= SUBLANES - s, hist, x[L - SUBLANES:, :])
        z = zt if L == SUBLANES else jnp.concatenate([x[:L - SUBLANES, :], zt], axis=0)
        y = y + pltpu.roll(z, s, 0) * w[K - 1 - s:K - s, :]
    return y


def _causal_conv_seqs(x, prev, w, b):
    L, C = x.shape
    K = w.shape[0]
    nseq = L // SUBLANES
    r = _iota2((L, C), 0) & (SUBLANES - 1)
    per_row = lambda a: jnp.broadcast_to(a, (nseq, SUBLANES, C)).reshape(L, C)
    y = b + x * w[K - 1:K, :]
    for s in range(1, K):
        head = per_row(prev[K - 1 - s])
        for rr in range(1, s):
            head = jnp.where(r == rr, per_row(prev[K - 1 - s + rr]), head)
        y = y + jnp.where(r >= s, pltpu.roll(x, s, 0), head) * w[K - 1 - s:K - s, :]
    return y


def _params(n_axes):
    return pltpu.CompilerParams(dimension_semantics=("arbitrary",) * n_axes,
                                vmem_limit_bytes=VMEM_LIMIT)


def _inproj_kernel(x_ref, nw_ref, w_ref, ws_ref, u_ref, us_ref, xn_ref):
    @pl.when(pl.program_id(1) == 0)
    def _():
        xn = _rms(x_ref[...], nw_ref[...]).astype(bf16)
        xn_ref[...] = xn
        us_ref[...] = _dot_nt(xn, ws_ref[...])

    u_ref[...] = _dot_nt(xn_ref[...], w_ref[...])


def _inproj(x, nw, w_main, w_small):
    M = x.shape[0]
    tm = _row_tile(M)
    tn = INPROJ_TILE
    return pl.pallas_call(
        _inproj_kernel,
        grid=(M // tm, U_MAIN // tn),
        in_specs=[pl.BlockSpec((tm, D_MODEL), lambda i, j: (i, 0)),
                  pl.BlockSpec((1, D_MODEL), lambda i, j: (0, 0)),
                  pl.BlockSpec((tn, D_MODEL), lambda i, j: (j, 0)),
                  pl.BlockSpec((U_SMALL, D_MODEL), lambda i, j: (0, 0))],
        out_specs=[pl.BlockSpec((tm, tn), lambda i, j: (i, j)),
                   pl.BlockSpec((tm, U_SMALL), lambda i, j: (i, 0))],
        out_shape=[jax.ShapeDtypeStruct((M, U_MAIN), f32),
                   jax.ShapeDtypeStruct((M, U_SMALL), f32)],
        scratch_shapes=[pltpu.VMEM((tm, D_MODEL), bf16)],
        compiler_params=_params(2),
        name="inproj",
    )(x, nw, w_main, w_small)


def _ssd_kernel(z_ref, x_ref, bc_ref, sm_ref, hx_ref, hbc_ref, cwx_ref, cwbc_ref, cbx_ref, cbbc_ref,
                dtb_ref, alog_ref, dexp_ref, nw_ref, ehp_ref, s0_ref,
                y_ref, s_ref, tailx_ref, tailbc_ref, yz_ref, *, L, Ls, n_chunks):
    nseq = L // Ls
    c = pl.program_id(1)
    xpre = x_ref[...]
    bcpre = bc_ref[...]
    if nseq > 1:
        prev_x = [hx_ref[:, k:k + 1, :] for k in range(SSD_CONV - 1)]
        prev_bc = [hbc_ref[:, k:k + 1, :] for k in range(SSD_CONV - 1)]
        xc = _silu(_causal_conv_seqs(xpre, prev_x, cwx_ref[...], cbx_ref[...]))
        bcc = _silu(_causal_conv_seqs(bcpre, prev_bc, cwbc_ref[...], cbbc_ref[...]))
    else:
        if n_chunks == 1:
            hx, hbc = hx_ref[0], hbc_ref[0]
        else:
            first = c == 0
            hx = jnp.where(first, hx_ref[0], tailx_ref[...])
            hbc = jnp.where(first, hbc_ref[0], tailbc_ref[...])
        xc = _silu(_causal_conv(xpre, hx, cwx_ref[...], cbx_ref[...]))
        bcc = _silu(_causal_conv(bcpre, hbc, cwbc_ref[...], cbbc_ref[...]))
    if n_chunks > 1:
        tailx_ref[...] = xpre[L - SUBLANES:, :]
        tailbc_ref[...] = bcpre[L - SUBLANES:, :]

    lane = _iota2((L, LANES), 1)
    dt = jnp.where(lane < SSD_HEADS, _softplus(sm_ref[:, :LANES] + dtb_ref[...]), 0.0)
    dta = dt * (-jnp.exp(alog_ref[...]))

    causal, tri, tri_t, last = _seq_masks(L, Ls)
    eye = _eye()
    cum = _sel_left(tri, dta)
    cum_t = _sel_right(_transpose_exact(dta, eye), tri_t)
    cum_last = _sel_left(last, cum)
    ehp = ehp_ref[...]
    xdt = xc * _expand_heads(dt, ehp)
    xdtw = xdt * jnp.exp(_expand_heads(cum_last - cum, ehp))
    ecum = jnp.exp(_expand_heads(cum, ehp))

    if n_chunks > 1:
        @pl.when(c == 0)
        def _():
            s_ref[...] = s0_ref[...]
        sprev_ref = s_ref
    else:
        sprev_ref = s0_ref

    GE = SSD_D_INNER // SSD_GROUPS
    HPG = SSD_HEADS // SSD_GROUPS
    shift = Ls.bit_length() - 1
    seq_of_row = _iota2((L, 1), 0) >> shift
    lane_lo = lane < SSD_HEAD_DIM
    ys, intra = [], []
    for g in range(SSD_GROUPS):
        bm = bcc[:, g * SSD_STATE:(g + 1) * SSD_STATE]
        cm = bcc[:, (SSD_GROUPS + g) * SSD_STATE:(SSD_GROUPS + g + 1) * SSD_STATE]
        cb = _dot_nt(cm, bm)
        acc = None
        for j in range(nseq):
            cmj = cm if nseq == 1 else jnp.where(seq_of_row == j, cm, 0.0)
            t = _dot_nt(cmj, sprev_ref[j, g * GE:(g + 1) * GE, :])
            acc = t if acc is None else acc + t
        ys.append(acc)
        for hp in range(HPG // 2):
            col0 = g * GE + hp * LANES
            xpair = xdt[:, col0:col0 + LANES].astype(bf16)
            for e in range(2):
                h = g * HPG + hp * 2 + e
                seg = cum[:, h:h + 1] - cum_t[h:h + 1, :]
                m = jnp.exp(jnp.where(causal, seg, NEG)) * cb
                intra.append(jnp.dot(m.astype(bf16), xpair, preferred_element_type=f32))
    ssq = jnp.zeros((L, 1), f32)
    for g in range(SSD_GROUPS):
        for hp in range(HPG // 2):
            col0 = g * GE + hp * LANES
            blk = slice(col0, col0 + LANES)
            o0, o1 = intra[col0 // LANES * 2], intra[col0 // LANES * 2 + 1]
            y = (jnp.where(lane_lo, o0, o1)
                 + ecum[:, blk] * ys[g][:, hp * LANES:(hp + 1) * LANES]
                 + dexp_ref[:, blk] * xc[:, blk])
            yz = y * _silu(z_ref[:, blk])
            ssq = ssq + jnp.sum(yz * yz, axis=-1, keepdims=True)
            yz_ref[:, blk] = yz
    r = lax.rsqrt(ssq * (1.0 / SSD_D_INNER) + EPS)
    y_ref[...] = ((yz_ref[...] * r) * nw_ref[...]).astype(bf16)

    for j in range(nseq):
        tl = (j + 1) * Ls - 1
        dec = jnp.exp(jnp.broadcast_to(cum_t[:, tl:tl + 1], (LANES, LANES)))
        for g in range(SSD_GROUPS):
            bm = bcc[:, g * SSD_STATE:(g + 1) * SSD_STATE]
            xw = xdtw[:, g * GE:(g + 1) * GE]
            if nseq > 1:
                xw = jnp.where(seq_of_row == j, xw, 0.0)
            ds = _dot_tn(xw, bm)
            for e in range(HPG):
                h = g * HPG + e
                rs = slice(h * SSD_HEAD_DIM, (h + 1) * SSD_HEAD_DIM)
                s_ref[j, rs, :] = (dec[h:h + 1, :] * sprev_ref[j, rs, :]
                                   + ds[e * SSD_HEAD_DIM:(e + 1) * SSD_HEAD_DIM, :])


def _ssd(u, us, hx, hbc, s0, wts, *, B, T, L, Ls):
    nseq = L // Ls
    n_chunks = T // Ls if nseq == 1 else 1
    nblk = B // nseq
    rb = lambda i, c: i * n_chunks + c
    if nseq > 1:
        assert Ls == SUBLANES
        h_specs = [pl.BlockSpec((nseq, SSD_CONV - 1, SSD_D_INNER), lambda i, c: (i, 0, 0)),
                   pl.BlockSpec((nseq, SSD_CONV - 1, SSD_BC), lambda i, c: (i, 0, SSD_D_INNER // SSD_BC))]
    else:
        h_specs = [pl.BlockSpec((1, SUBLANES, SSD_D_INNER), lambda i, c: (i, 0, 0)),
                   pl.BlockSpec((1, SUBLANES, SSD_BC), lambda i, c: (i, 0, 0))]
    const = lambda shape: pl.BlockSpec(shape, lambda i, c: (0,) * len(shape))
    kern = functools.partial(_ssd_kernel, L=L, Ls=Ls, n_chunks=n_chunks)
    return pl.pallas_call(
        kern,
        grid=(nblk, n_chunks),
        in_specs=[pl.BlockSpec((L, SSD_D_INNER), lambda i, c: (rb(i, c), U_Z)),
                  pl.BlockSpec((L, SSD_D_INNER), lambda i, c: (rb(i, c), U_X)),
                  pl.BlockSpec((L, SSD_BC), lambda i, c: (rb(i, c), U_BC)),
                  pl.BlockSpec((L, U_SMALL), lambda i, c: (rb(i, c), 0)),
                  *h_specs,
                  const((SSD_CONV, SSD_D_INNER)), const((SSD_CONV, SSD_BC)),
                  const((1, SSD_D_INNER)), const((1, SSD_BC)),
                  const((1, LANES)), const((1, LANES)),
                  const((1, SSD_D_INNER)), const((1, SSD_D_INNER)),
                  const((2 * LANES, SSD_D_INNER)),
                  pl.BlockSpec((nseq, SSD_D_INNER, SSD_STATE), lambda i, c: (i, 0, 0))],
        out_specs=[pl.BlockSpec((L, SSD_D_INNER), lambda i, c: (rb(i, c), 0)),
                   pl.BlockSpec((nseq, SSD_D_INNER, SSD_STATE), lambda i, c: (i, 0, 0))],
        out_shape=[jax.ShapeDtypeStruct((B * T, SSD_D_INNER), bf16),
                   jax.ShapeDtypeStruct((B, SSD_D_INNER, SSD_STATE), f32)],
        scratch_shapes=[pltpu.VMEM((SUBLANES, SSD_D_INNER), f32),
                        pltpu.VMEM((SUBLANES, SSD_BC), f32),
                        pltpu.VMEM((L, SSD_D_INNER), f32)],
        compiler_params=_params(2),
        name="ssd",
    )(u, u, u, us, hx, hbc, wts["cw_x"], wts["cw_bc"], wts["cb_x"], wts["cb_bc"],
      wts["dt_bias"], wts["a_log"], wts["d_exp"], wts["ssd_norm_w"], wts["ehp"], s0)


def _interleave(gens):
    while gens:
        alive = []
        for g in gens:
            try:
                next(g)
                alive.append(g)
            except StopIteration:
                pass
        gens = alive


def _group_views(refs, g, nseq):
    return [r.at[g] if kind == "tok" else r.at[pl.ds(g * nseq, nseq)] if kind == "state" else r
            for r, kind in refs]


def _mlstm_kernel(*refs, L, Ls, n_chunks, groups):
    kinds = ["tok"] * 5 + ["const"] * 3 + ["state"] * 3 + ["tok"] + ["state"] * 3
    _interleave([_mlstm_chunk(*_group_views(list(zip(refs, kinds)), g, L // Ls), L=L, Ls=Ls, n_chunks=n_chunks)
                 for g in range(groups)])


def _mlstm_chunk(q_ref, k_ref, v_ref, o_ref, sm_ref, ib_ref, fb_ref, nw_ref, c0_ref, n0_ref, m0_ref,
                 y_ref, c_ref, n_ref, m_ref, *, L, Ls, n_chunks):
    nseq = L // Ls
    c = pl.program_id(1)
    if n_chunks > 1:
        @pl.when(c == 0)
        def _():
            c_ref[...] = c0_ref[...]
            n_ref[...] = n0_ref[...]
            m_ref[...] = m0_ref[...]
        cprev_ref, nprev_ref, mprev_ref = c_ref, n_ref, m_ref
    else:
        cprev_ref, nprev_ref, mprev_ref = c0_ref, n0_ref, m0_ref

    per_tok = lambda a: jnp.broadcast_to(a, (nseq, Ls, LANES)).reshape(L, LANES)
    ig = sm_ref[:, :LANES] + ib_ref[...]
    fraw = sm_ref[:, LANES:] + fb_ref[...]
    lf = -_softplus(-fraw)
    causal, tri, _, last = _seq_masks(L, Ls)
    eye = _eye()
    F = _sel_left(tri, lf)
    FL = _sel_left(last, F)
    mp = per_tok(mprev_ref[:, 0:1, :])
    r_t = _transpose_exact(ig - F, eye)
    inter = F + mp
    lw = FL - F + ig
    segmax = jnp.max(lw.reshape(nseq, Ls, LANES), axis=1, keepdims=True)
    m_new = jnp.maximum(FL + mp, per_tok(segmax))
    sc = jnp.exp(lw - m_new)
    dec = jnp.exp(FL + mp - m_new)
    m_out = m_new.reshape(nseq, Ls, LANES)[:, 0:SUBLANES, :]
    yield

    shift = Ls.bit_length() - 1
    seq_of_row = _iota2((L, 1), 0) >> shift
    kscale = ML_QK_DIM ** -0.5
    heads = range(ML_HEADS)
    qcols = lambda h: slice(h * ML_QK_DIM, (h + 1) * ML_QK_DIM)
    vcols = lambda h: slice(h * ML_V_DIM, (h + 1) * ML_V_DIM)
    q = [q_ref[:, qcols(h)] for h in heads]
    k = [k_ref[:, qcols(h)] * kscale for h in heads]
    v = [v_ref[:, vcols(h)].astype(bf16) for h in heads]
    qk = [_dot_nt(q[h], k[h]) for h in heads]
    yield
    qc = []
    for h in heads:
        acc = None
        for j in range(nseq):
            qj = q[h] if nseq == 1 else jnp.where(seq_of_row == j, q[h], 0.0)
            t = _dot(qj, cprev_ref[j, qcols(h), :])
            acc = t if acc is None else acc + t
        qc.append(acc)
    yield
    rep = lambda col: jnp.broadcast_to(col, (L, LANES))
    twice = lambda a: jnp.concatenate([a, a], axis=1)
    qn = []
    for h in heads:
        acc = None
        for j in range(nseq):
            qj = q[h] if nseq == 1 else jnp.where(seq_of_row == j, q[h], 0.0)
            t = _dot_nt(qj, jnp.broadcast_to(nprev_ref[j, h:h + 1, :], (LANES, ML_QK_DIM)))
            acc = t if acc is None else acc + t
        qn.append(acc)
    yield
    dm, m_rep, inter_rep = [], [], []
    for h in heads:
        gl = GATE_LANE + h
        d = jnp.where(causal, F[:, gl:gl + 1] + r_t[gl:gl + 1, :], NEG)
        dm.append(d)
        inter_rep.append(rep(inter[:, gl:gl + 1]))
        m_rep.append(jnp.maximum(rep(jnp.max(d, axis=-1, keepdims=True)), inter_rep[h]))
    yield
    w = [jnp.exp(dm[h] - (m_rep[h] if L == LANES else m_rep[h][:, 0:1])) * qk[h] for h in heads]
    yield
    wv = [jnp.dot(w[h].astype(bf16), v[h], preferred_element_type=f32) for h in heads]
    wsum = [_rowsum_mxu(w[h]) for h in heads]
    yield
    hh = []
    for h in heads:
        wi = jnp.exp(inter_rep[h] - m_rep[h])
        den = wsum[h] + wi * qn[h]
        inv = 1.0 / jnp.maximum(jnp.abs(den), jnp.exp(-m_rep[h]))
        hh.append((wv[h] + twice(wi) * qc[h]) * twice(inv))
    yield
    ssq = [_rowsum_mxu(hh[h] * hh[h]) for h in heads]
    yield
    for h in heads:
        r = lax.rsqrt(ssq[h] * (1.0 / ML_V_DIM) + EPS)
        hn = (hh[h] * twice(r)) * nw_ref[:, vcols(h)]
        y_ref[:, vcols(h)] = (_sigmoid(o_ref[:, vcols(h)]) * hn).astype(bf16)
        if h % 2 == 1:
            yield
    for h in heads:
        gl = GATE_LANE + h
        ksc = k[h] * sc[:, gl:gl + 1]
        for j in range(nseq):
            r0 = j * Ls
            dj = dec[r0:r0 + 1, gl:gl + 1]
            kj = ksc if nseq == 1 else jnp.where(seq_of_row == j, ksc, 0.0)
            c_ref[j, qcols(h), :] = dj * cprev_ref[j, qcols(h), :] + _dot_tn(kj, v[h])
            n_ref[j, h:h + 1, :] = (dj * nprev_ref[j, h:h + 1, :]
                                    + jnp.sum(ksc[r0:r0 + Ls, :], axis=0, keepdims=True))
        if h % 2 == 1:
            yield
    m_ref[...] = m_out


def _scan_groups(nblk, nseq):
    return SCAN_GROUPS if nseq == 1 and nblk % SCAN_GROUPS == 0 else 1


def _mlstm(u, us, c0, n0, m0, wts, *, B, T, L, Ls):
    nseq = L // Ls
    n_chunks = T // Ls if nseq == 1 else 1
    nblk = B // nseq
    G = _scan_groups(nblk, nseq)
    rows = n_chunks * L
    u3, us3 = u.reshape(nblk, rows, U_MAIN), us.reshape(nblk, rows, U_SMALL)
    tok = lambda width, col: pl.BlockSpec((G, L, width), lambda i, c: (i, c, col))
    const = lambda shape: pl.BlockSpec(shape, lambda i, c: (0,) * len(shape))
    st = lambda shape: pl.BlockSpec((G * nseq,) + shape, lambda i, c: (i, 0, 0))
    kern = functools.partial(_mlstm_kernel, L=L, Ls=Ls, n_chunks=n_chunks, groups=G)
    y, c_new, n_new, m_new = pl.pallas_call(
        kern,
        grid=(nblk // G, n_chunks),
        in_specs=[tok(ML_QK_INNER, U_Q), tok(ML_QK_INNER, U_K), tok(ML_D_INNER, U_V), tok(ML_D_INNER, U_O),
                  tok(U_SMALL, 0),
                  const((1, LANES)), const((1, LANES)), const((1, ML_D_INNER)),
                  st((ML_QK_INNER, ML_V_DIM)), st((ML_HEADS, ML_QK_DIM)), st((SUBLANES, LANES))],
        out_specs=[tok(ML_D_INNER, 0),
                   st((ML_QK_INNER, ML_V_DIM)), st((ML_HEADS, ML_QK_DIM)), st((SUBLANES, LANES))],
        out_shape=[jax.ShapeDtypeStruct((nblk, rows, ML_D_INNER), bf16),
                   jax.ShapeDtypeStruct((B, ML_QK_INNER, ML_V_DIM), f32),
                   jax.ShapeDtypeStruct((B, ML_HEADS, ML_QK_DIM), f32),
                   jax.ShapeDtypeStruct((B, SUBLANES, LANES), f32)],
        compiler_params=_params(2),
        name="mlstm",
    )(u3, u3, u3, u3, us3, wts["i_bias"], wts["f_bias"], wts["ml_norm_w"], c0, n0, m0)
    return y.reshape(B * T, ML_D_INNER), c_new, n_new, m_new


def _outproj_kernel(ys_ref, ym_ref, ws_ref, wm_ref, h_ref, o_ref, wsb_ref, wmb_ref):
    @pl.when(pl.program_id(1) == 0)
    def _():
        wsb_ref[...] = ws_ref[...].astype(bf16)
        wmb_ref[...] = wm_ref[...].astype(bf16)

    o_ref[...] = (h_ref[...]
                  + jnp.dot(ys_ref[...], wsb_ref[...], preferred_element_type=f32)
                  + jnp.dot(ym_ref[...], wmb_ref[...], preferred_element_type=f32))


def _outproj(ys, ym, w_out, h):
    M = h.shape[0]
    tm = _row_tile(M)
    tn = OUTPROJ_TILE
    return pl.pallas_call(
        _outproj_kernel,
        grid=(D_MODEL // tn, M // tm),
        in_specs=[pl.BlockSpec((tm, SSD_D_INNER), lambda j, i: (i, 0)),
                  pl.BlockSpec((tm, ML_D_INNER), lambda j, i: (i, 0)),
                  pl.BlockSpec((SSD_D_INNER, tn), lambda j, i: (0, j)),
                  pl.BlockSpec((ML_D_INNER, tn), lambda j, i: (1, j)),
                  pl.BlockSpec((tm, tn), lambda j, i: (i, j))],
        out_specs=pl.BlockSpec((tm, tn), lambda j, i: (i, j)),
        out_shape=jax.ShapeDtypeStruct((M, D_MODEL), f32),
        scratch_shapes=[pltpu.VMEM((SSD_D_INNER, tn), bf16), pltpu.VMEM((ML_D_INNER, tn), bf16)],
        compiler_params=_params(2),
        name="outproj",
    )(ys, ym, w_out, w_out, h)


def _ffn_kernel(h_ref, nw_ref, wg_ref, wv_ref, cwg_ref, cwv_ref, cbg_ref, cbv_ref, wd_ref, fw_ref, *rest,
                tm, tf, multi, blocks_per_seq):
    if multi:
        hg_ref, hv_ref, y_ref, tg_ref, tv_ref, xn_ref = rest
    else:
        hg_ref, hv_ref, y_ref, tg_ref, tv_ref, xn_ref, carg_ref, carv_ref = rest
    i = pl.program_id(0)
    j = pl.program_id(1)

    @pl.when(j == 0)
    def _():
        xn_ref[...] = _rms(h_ref[...], nw_ref[...]).astype(bf16)
        y_ref[...] = jnp.zeros_like(y_ref)

    xn = xn_ref[...]
    subs = [slice(c0, c0 + MXU_COLS) for c0 in range(0, tf, MXU_COLS)]
    up_dots = lambda cs: [jnp.dot(xn, w_ref[:, cs], preferred_element_type=f32) for w_ref in (wg_ref, wv_ref)]
    ups_next = up_dots(subs[0])
    for n, cs in enumerate(subs):
        ups = ups_next
        if n + 1 < len(subs):
            ups_next = up_dots(subs[n + 1])
        convd = []
        for half, (up, cw_ref, cb_ref) in enumerate(zip(ups, (cwg_ref, cwv_ref), (cbg_ref, cbv_ref))):
            if multi:
                s_ref, t_ref = ((hg_ref, tg_ref), (hv_ref, tv_ref))[half]
                prev = [s_ref[:, k:k + 1, cs] for k in range(FFN_CONV - 1)]
                convd.append(_causal_conv_seqs(up, prev, cw_ref[:, cs], cb_ref[:, cs]))
                up3 = up.reshape(tm // SUBLANES, SUBLANES, MXU_COLS)
                t_ref[:, :, cs] = up3[:, SUBLANES - (FFN_CONV - 1):, :]
            else:
                h_ref_, car_ref, t_ref = ((hg_ref, carg_ref, tg_ref), (hv_ref, carv_ref, tv_ref))[half]
                tail = up[tm - SUBLANES:, :]
                if blocks_per_seq == 1:
                    hist = h_ref_[0, :, cs]
                else:
                    hist = jnp.where((i % blocks_per_seq) == 0, h_ref_[0, :, cs], car_ref[j, :, cs])
                    car_ref[j, :, cs] = tail
                t_ref[0, :, cs] = tail
                convd.append(_causal_conv(up, hist, cw_ref[:, cs], cb_ref[:, cs]))
        act = (_silu(convd[0]) * convd[1]).astype(bf16)
        y_ref[...] += jnp.dot(act, wd_ref[cs, :], preferred_element_type=f32)

    @pl.when(j == pl.num_programs(1) - 1)
    def _():
        y_ref[...] = _rms(h_ref[...] + y_ref[...], fw_ref[...])


def _ffn(h, ffn0, wts, *, B, T):
    M = h.shape[0]
    tm = _row_tile(M)
    tf = FFN_TILE
    n_ff = D_FF // tf
    multi = T < tm
    const = lambda shape: pl.BlockSpec(shape, lambda i, j: (0,) * len(shape))
    if multi:
        assert T == SUBLANES
        blocks_per_seq = 1
        nseq = tm // T
        hist = [ffn0, ffn0]
        h_specs = [pl.BlockSpec((nseq, FFN_CONV - 1, tf), lambda i, j: (i, 0, j)),
                   pl.BlockSpec((nseq, FFN_CONV - 1, tf), lambda i, j: (i, 0, n_ff + j))]
        t_specs = [pl.BlockSpec((nseq, FFN_CONV - 1, tf), lambda i, j: (i, 0, j))] * 2
        t_shapes = [jax.ShapeDtypeStruct((B, FFN_CONV - 1, D_FF), f32)] * 2
        scratch = []
    else:
        blocks_per_seq = T // tm
        pad = jnp.pad(ffn0, ((0, 0), (SUBLANES - (FFN_CONV - 1), 0), (0, 0)))
        hist = [pad, pad]
        h_specs = [pl.BlockSpec((1, SUBLANES, tf), lambda i, j: (i // blocks_per_seq, 0, j)),
                   pl.BlockSpec((1, SUBLANES, tf), lambda i, j: (i // blocks_per_seq, 0, n_ff + j))]
        t_specs = [pl.BlockSpec((1, SUBLANES, tf), lambda i, j: (i, 0, j))] * 2
        t_shapes = [jax.ShapeDtypeStruct((M // tm, SUBLANES, D_FF), f32)] * 2
        scratch = [pltpu.VMEM((n_ff, SUBLANES, tf), f32)] * 2
    kern = functools.partial(_ffn_kernel, tm=tm, tf=tf, multi=multi, blocks_per_seq=blocks_per_seq)
    y, *tails = pl.pallas_call(
        kern,
        grid=(M // tm, n_ff),
        in_specs=[pl.BlockSpec((tm, D_MODEL), lambda i, j: (i, 0), pipeline_mode=pl.Buffered(1)),
                  const((1, D_MODEL)),
                  pl.BlockSpec((D_MODEL, tf), lambda i, j: (0, j)),
                  pl.BlockSpec((D_MODEL, tf), lambda i, j: (0, n_ff + j)),
                  pl.BlockSpec((FFN_CONV, tf), lambda i, j: (0, j)),
                  pl.BlockSpec((FFN_CONV, tf), lambda i, j: (0, n_ff + j)),
                  pl.BlockSpec((1, tf), lambda i, j: (0, j)),
                  pl.BlockSpec((1, tf), lambda i, j: (0, n_ff + j)),
                  pl.BlockSpec((tf, D_MODEL), lambda i, j: (j, 0)),
                  const((1, D_MODEL)),
                  *h_specs],
        out_specs=[pl.BlockSpec((tm, D_MODEL), lambda i, j: (i, 0)), *t_specs],
        out_shape=[jax.ShapeDtypeStruct((M, D_MODEL), f32), *t_shapes],
        scratch_shapes=[pltpu.VMEM((tm, D_MODEL), bf16), *scratch],
        compiler_params=_params(2),
        name="ffn",
    )(h, wts["norm2_w"], wts["w_up"], wts["w_up"], wts["ffn_cw"], wts["ffn_cw"],
      wts["ffn_cb"], wts["ffn_cb"], wts["w_down"], wts["final_norm_w"], *hist)
    if multi:
        return y, jnp.concatenate(tails, axis=-1)
    last = jnp.concatenate(tails, axis=-1).reshape(B, blocks_per_seq, SUBLANES, 2 * D_FF)
    return y, last[:, blocks_per_seq - 1, SUBLANES - (FFN_CONV - 1):, :]


def _hist_tile(state):
    return jnp.pad(state, ((0, 0), (SUBLANES - state.shape[1], 0), (0, 0)))


def _layer(h, states, wts, *, B, T, L, Ls):
    conv0, s0, c0, n0, m0, ffn0 = states
    u, us = _inproj(h, wts["norm1_w"], wts["w_main"], wts["w_small"])
    if L // Ls > 1:
        hx = hbc = conv0
    else:
        hx, hbc = _hist_tile(conv0[:, :, :SSD_D_INNER]), _hist_tile(conv0[:, :, SSD_D_INNER:])
    y_ssd, s_new = _ssd(u, us, hx, hbc, s0.reshape(B, SSD_D_INNER, SSD_STATE), wts, B=B, T=T, L=L, Ls=Ls)
    m_pad = jnp.broadcast_to(
        jnp.pad(m0, ((0, 0), (GATE_LANE, LANES - GATE_LANE - ML_HEADS)))[:, None, :], (B, SUBLANES, LANES))
    y_ml, c_new, n_new, m_new = _mlstm(u, us, c0.reshape(B, ML_QK_INNER, ML_V_DIM), n0, m_pad, wts,
                                       B=B, T=T, L=L, Ls=Ls)
    h1 = _outproj(y_ssd, y_ml, wts["w_out"], h)
    y, ffn_new = _ffn(h1, ffn0, wts, B=B, T=T)
    ur = u.reshape(B, T, U_MAIN)[:, T - (SSD_CONV - 1):, :]
    conv_new = jnp.concatenate([ur[:, :, U_X * SSD_D_INNER:(U_X + 1) * SSD_D_INNER],
                                ur[:, :, U_BC * SSD_BC:(U_BC + 1) * SSD_BC]], axis=-1)
    new_states = (conv_new,
                  s_new.reshape(B, SSD_HEADS, SSD_HEAD_DIM, SSD_STATE),
                  c_new.reshape(B, ML_HEADS, ML_QK_DIM, ML_V_DIM),
                  n_new,
                  m_new[:, 0, GATE_LANE:GATE_LANE + ML_HEADS],
                  ffn_new)
    return y, new_states


def _repack_kernel(w_ref, o_ref):
    o_ref[...] = w_ref[...].astype(bf16)


def _repack_w_in(w_t, offs):
    rows = REPACK_ROWS
    bounds, dst = [], 0
    for name in ("z", "x", "v", "o", "q", "k", "bc"):
        src, width = offs[name]
        assert width % rows == 0 and dst % rows == 0
        bounds.append((dst // rows, src))
        dst += width

    unit = 2 * SUBLANES
    assert all(src % unit == 0 for _, src in bounds)

    def src_row(i):
        row = jnp.int32(0)
        for first_blk, src in bounds:
            row = jnp.where(i >= first_blk, src // unit + (i - first_blk) * (rows // unit), row)
        return row * unit

    return pl.pallas_call(
        _repack_kernel,
        grid=(U_MAIN // rows,),
        in_specs=[pl.BlockSpec((pl.Element(rows), pl.Element(D_MODEL)), lambda i: (src_row(i), 0))],
        out_specs=pl.BlockSpec((rows, D_MODEL), lambda i: (i, 0)),
        out_shape=jax.ShapeDtypeStruct((U_MAIN, D_MODEL), bf16),
        compiler_params=_params(1),
        name="repack_w_in",
    )(w_t)


def _prep_weights(norm1_w, w_in, ssd_conv_w, ssd_conv_b, ssd_dt_bias, ssd_A_log, ssd_D, ssd_norm_w,
                  ml_i_bias, ml_f_bias, ml_norm_w, w_out, norm2_w, w_up, ffn_conv_w, ffn_conv_b, w_down,
                  final_norm_w):
    w_t = w_in.T
    o = 0
    rows, offs = {}, {}
    for name, width in (("z", SSD_D_INNER), ("x", SSD_D_INNER), ("bc", SSD_BC), ("dt", SSD_HEADS),
                        ("q", ML_QK_INNER), ("k", ML_QK_INNER), ("v", ML_D_INNER), ("i", ML_HEADS),
                        ("f", ML_HEADS), ("o", ML_D_INNER)):
        offs[name] = (o, width)
        if name in ("dt", "i", "f"):
            rows[name] = lax.optimization_barrier(w_t[o:o + width, :]).astype(bf16)
        o += width
    w_main = _repack_w_in(w_t, offs)
    zpad = lambda n: jnp.zeros((n, D_MODEL), bf16)
    w_small = jnp.concatenate([rows["dt"], rows["i"], zpad(LANES - GATE_LANE - ML_HEADS),
                               zpad(GATE_LANE), rows["f"], zpad(LANES - GATE_LANE - ML_HEADS)], axis=0)
    lane_row = lambda v, off: jnp.pad(v.astype(f32), (off, LANES - off - v.shape[0]))[None, :]
    hp = jnp.arange(SSD_D_INNER) // SSD_HEAD_DIM
    ehp = (jnp.arange(LANES)[:, None] == hp[None, :]).astype(bf16)
    return dict(
        norm1_w=norm1_w[None, :], w_main=w_main, w_small=w_small,
        cw_x=ssd_conv_w[:, :SSD_D_INNER], cw_bc=ssd_conv_w[:, SSD_D_INNER:],
        cb_x=ssd_conv_b[None, :SSD_D_INNER], cb_bc=ssd_conv_b[None, SSD_D_INNER:],
        dt_bias=lane_row(ssd_dt_bias, 0), a_log=lane_row(ssd_A_log, 0),
        d_exp=jnp.repeat(ssd_D.astype(f32), SSD_HEAD_DIM)[None, :], ssd_norm_w=ssd_norm_w[None, :],
        ehp=jnp.concatenate([ehp, ehp], axis=0),
        i_bias=lane_row(ml_i_bias, GATE_LANE), f_bias=lane_row(ml_f_bias, GATE_LANE),
        ml_norm_w=ml_norm_w[None, :],
        w_out=w_out, norm2_w=norm2_w[None, :], w_up=w_up.astype(bf16),
        ffn_cw=ffn_conv_w, ffn_cb=ffn_conv_b[None, :], w_down=w_down.astype(bf16),
        final_norm_w=final_norm_w[None, :])


def kernel(x_prompt, x_sample, state_ssd_conv, state_ssd, state_mlstm_C, state_mlstm_n, state_mlstm_m,
           state_ffn_conv, meta_tokens, norm1_w, w_in, ssd_conv_w, ssd_conv_b, ssd_dt_bias, ssd_A_log,
           ssd_D, ssd_norm_w, ml_i_bias, ml_f_bias, ml_norm_w, w_out, norm2_w, w_up, ffn_conv_w,
           ffn_conv_b, w_down, final_norm_w):
    depth = w_in.shape[0]
    assert depth == 1, "single-layer step"
    Bp, Tp, _ = x_prompt.shape
    Bs, Ts, _ = x_sample.shape
    wts = _prep_weights(norm1_w[0], w_in[0], ssd_conv_w[0], ssd_conv_b[0], ssd_dt_bias[0], ssd_A_log[0],
                        ssd_D[0], ssd_norm_w[0], ml_i_bias[0], ml_f_bias[0], ml_norm_w[0], w_out[0],
                        norm2_w[0], w_up[0], ffn_conv_w[0], ffn_conv_b[0], w_down[0], final_norm_w)
    zero_states = (jnp.zeros((1, SSD_CONV - 1, SSD_CONV_DIM), f32),
                   jnp.zeros((1, SSD_HEADS, SSD_HEAD_DIM, SSD_STATE), f32),
                   jnp.zeros((1, ML_HEADS, ML_QK_DIM, ML_V_DIM), f32),
                   jnp.zeros((1, ML_HEADS, ML_QK_DIM), f32),
                   jnp.zeros((1, ML_HEADS), f32),
                   jnp.zeros((1, FFN_CONV - 1, 2 * D_FF), f32))
    _, meta_states = _layer(meta_tokens.astype(f32), zero_states, wts, B=1, T=N_META, L=N_META, Ls=N_META)
    p_init = tuple(jnp.broadcast_to(s, (Bp,) + s.shape[1:]) for s in meta_states)
    yp, p_new = _layer(x_prompt.reshape(Bp * Tp, D_MODEL), p_init, wts, B=Bp, T=Tp, L=128, Ls=128)
    s_init = (state_ssd_conv[0], state_ssd[0], state_mlstm_C[0], state_mlstm_n[0], state_mlstm_m[0],
              state_ffn_conv[0])
    ys, s_new = _layer(x_sample.reshape(Bs * Ts, D_MODEL), s_init, wts, B=Bs, T=Ts, L=8 * Ts, Ls=Ts)
    return (yp.reshape(Bp, Tp, D_MODEL), ys.reshape(Bs, Ts, D_MODEL),
            *(s[None] for s in p_new), *(s[None] for s in s_new))
```

```python
import functools

import jax
import jax.numpy as jnp
from jax import lax
from jax.experimental import pallas as pl
from jax.experimental.pallas import tpu as pltpu

f32 = jnp.float32
bf16 = jnp.bfloat16

D_MODEL = 2048
N_META = 16
SSD_HEADS = 32
SSD_HEAD_DIM = 64
SSD_D_INNER = SSD_HEADS * SSD_HEAD_DIM
SSD_GROUPS = 2
SSD_STATE = 128
SSD_CONV = 4
SSD_BC = 2 * SSD_GROUPS * SSD_STATE
SSD_CONV_DIM = SSD_D_INNER + SSD_BC
ML_HEADS = 8
ML_QK_DIM = 128
ML_V_DIM = 256
ML_QK_INNER = ML_HEADS * ML_QK_DIM
ML_D_INNER = ML_HEADS * ML_V_DIM
D_FF = 5632
FFN_CONV = 3
EPS = 1e-6
NEG = -1e30

LANES = 128
SUBLANES = 8
VMEM_LIMIT = 56 * 1024 * 1024
MXU_COLS = 256
ROW_TILE = 1024
FFN_TILE = 512
INPROJ_TILE = 1536
OUTPROJ_TILE = 1024
OUTPROJ_CAST_TILE = 512
INPROJ_CAST_TILE = 512
SCAN_GROUPS = 2


def _row_tile(M):
    return ROW_TILE if M % ROW_TILE == 0 else M


U_Z, U_X, U_V, U_O = 0, 1, 2, 3
U_Q, U_K = 8, 9
U_BC = 20
U_MAIN = 4 * 2048 + 2 * 1024 + 512
GATE_LANE = 32
U_SMALL = 2 * LANES


def _dot(a, b):
    return jnp.dot(a.astype(bf16), b.astype(bf16), preferred_element_type=f32)


def _dot_nt(a, b):
    return lax.dot_general(a.astype(bf16), b.astype(bf16), (((1,), (1,)), ((), ())),
                           preferred_element_type=f32)


def _dot_tn(a, b):
    return lax.dot_general(a.astype(bf16), b.astype(bf16), (((0,), (0,)), ((), ())),
                           preferred_element_type=f32)


def _split3(a):
    hi = a.astype(bf16)
    r1 = a - hi.astype(f32)
    mid = r1.astype(bf16)
    lo = (r1 - mid.astype(f32)).astype(bf16)
    return hi, mid, lo


def _sel_right(a, e01):
    return jnp.dot(jnp.concatenate(_split3(a), axis=1), jnp.concatenate([e01] * 3, axis=0),
                   preferred_element_type=f32)


def _sel_left(e01, a):
    return jnp.dot(jnp.concatenate([e01] * 3, axis=1), jnp.concatenate(_split3(a), axis=0),
                   preferred_element_type=f32)


def _split2(a):
    hi = a.astype(bf16)
    mid = (a - hi.astype(f32)).astype(bf16)
    return jnp.concatenate([hi, mid], axis=1)


def _expand_heads(a, e01x2):
    return jnp.dot(_split2(a), e01x2, preferred_element_type=f32)


def _rowsum_mxu(a):
    return jnp.dot(_split2(a), jnp.ones((2 * a.shape[1], LANES), bf16), preferred_element_type=f32)


def _transpose_exact(a, eye):
    return lax.dot_general(jnp.concatenate([eye] * 3, axis=1), jnp.concatenate(_split3(a), axis=1),
                           (((1,), (1,)), ((), ())), preferred_element_type=f32)


def _iota2(shape, axis):
    return lax.broadcasted_iota(jnp.int32, shape, axis)


def _as01(m):
    return jnp.where(m, 1.0, 0.0).astype(bf16)


def _eye():
    return _as01(_iota2((LANES, LANES), 0) == _iota2((LANES, LANES), 1))


def _seq_masks(L, Ls):
    t = _iota2((L, L), 0)
    s = _iota2((L, L), 1)
    shift = Ls.bit_length() - 1
    same = (t >> shift) == (s >> shift)
    causal = same & (s <= t)
    causal_t = same & (t <= s)
    last = s == (t | (Ls - 1))
    return causal, _as01(causal), _as01(causal_t), _as01(last)


def _sigmoid(x):
    return 1.0 / (1.0 + jnp.exp(-x))


def _silu(x):
    return x * _sigmoid(x)


def _softplus(x):
    return jnp.maximum(x, 0.0) + jnp.log(1.0 + jnp.exp(-jnp.abs(x)))


def _rms(x, w):
    r = lax.rsqrt(jnp.mean(x * x, axis=-1, keepdims=True) + EPS)
    return (x * r) * w


def _causal_conv(x, hist, w, b):
    L, C = x.shape
    K = w.shape[0]
    r = _iota2((SUBLANES, C), 0)
    y = b + x * w[K - 1:K, :]
    for s in range(1, K):
        zt = jnp.where(r >= SUBLANES - s, hist, x[L - SUBLANES:, :])
        z = zt if L == SUBLANES else jnp.concatenate([x[:L - SUBLANES, :], zt], axis=0)
        y = y + pltpu.roll(z, s, 0) * w[K - 1 - s:K - s, :]
    return y


def _causal_conv_seqs(x, prev, w, b):
    L, C = x.shape
    K = w.shape[0]
    nseq = L // SUBLANES
    r = _iota2((L, C), 0) & (SUBLANES - 1)
    per_row = lambda a: jnp.broadcast_to(a, (nseq, SUBLANES, C)).reshape(L, C)
    y = b + x * w[K - 1:K, :]
    for s in range(1, K):
        head = per_row(prev[K - 1 - s])
        for rr in range(1, s):
            head = jnp.where(r == rr, per_row(prev[K - 1 - s + rr]), head)
        y = y + jnp.where(r >= s, pltpu.roll(x, s, 0), head) * w[K - 1 - s:K - s, :]
    return y


def _params(n_axes):
    return pltpu.CompilerParams(dimension_semantics=("arbitrary",) * n_axes,
                                vmem_limit_bytes=VMEM_LIMIT)


def _inproj_kernel(x_ref, nw_ref, w_ref, ws_ref, u_ref, us_ref, xn_ref):
    @pl.when(pl.program_id(1) == 0)
    def _():
        xn = _rms(x_ref[...], nw_ref[...]).astype(bf16)
        xn_ref[...] = xn
        us_ref[...] = _dot_nt(xn, ws_ref[...])

    u_ref[...] = _dot_nt(xn_ref[...], w_ref[...])


def _inproj_cast_kernel(x_ref, nw_ref, w_ref, ws_ref, u_ref, us_ref, wb_ref, xn_ref):
    @pl.when(pl.program_id(1) == 0)
    def _():
        xn = _rms(x_ref[...], nw_ref[...]).astype(bf16)
        xn_ref[...] = xn
        us_ref[...] = _dot_nt(xn, ws_ref[...])

    wb_ref[...] = w_ref[...].astype(bf16)
    u_ref[...] = _dot_nt(xn_ref[...], wb_ref[...])


def _inproj(x, wts):
    M = x.shape[0]
    tm = _row_tile(M)
    const = lambda shape: pl.BlockSpec(shape, lambda i, j: (0,) * len(shape))
    out_shape = [jax.ShapeDtypeStruct((M, U_MAIN), f32), jax.ShapeDtypeStruct((M, U_SMALL), f32)]
    if "w_main" in wts:
        tn = INPROJ_TILE
        u, us = pl.pallas_call(
            _inproj_kernel,
            grid=(M // tm, U_MAIN // tn),
            in_specs=[pl.BlockSpec((tm, D_MODEL), lambda i, j: (i, 0)), const((1, D_MODEL)),
                      pl.BlockSpec((tn, D_MODEL), lambda i, j: (j, 0)), const((U_SMALL, D_MODEL))],
            out_specs=[pl.BlockSpec((tm, tn), lambda i, j: (i, j)),
                       pl.BlockSpec((tm, U_SMALL), lambda i, j: (i, 0))],
            out_shape=out_shape,
            scratch_shapes=[pltpu.VMEM((tm, D_MODEL), bf16)],
            compiler_params=_params(2),
            name="inproj",
        )(x, wts["norm1_w"], wts["w_main"], wts["w_small"])
        return u, us, {}
    assert M == tm, "the casting call must see every weight tile exactly once"
    tn = INPROJ_CAST_TILE
    bounds, dst = [], 0
    for name in ("z", "x", "v", "o", "q", "k", "bc"):
        src, width = wts["w_in_offs"][name]
        assert width % tn == 0 and dst % tn == 0
        bounds.append((dst // tn, src))
        dst += width
    unit = 2 * SUBLANES
    assert dst == U_MAIN and all(src % unit == 0 for _, src in bounds)

    def src_row(j):
        row = jnp.int32(0)
        for first_blk, src in bounds:
            row = jnp.where(j >= first_blk, src // unit + (j - first_blk) * (tn // unit), row)
        return row * unit

    u, us, w_main = pl.pallas_call(
        _inproj_cast_kernel,
        grid=(1, U_MAIN // tn),
        in_specs=[pl.BlockSpec((tm, D_MODEL), lambda i, j: (i, 0)), const((1, D_MODEL)),
                  pl.BlockSpec((pl.Element(tn), pl.Element(D_MODEL)), lambda i, j: (src_row(j), 0)),
                  const((U_SMALL, D_MODEL))],
        out_specs=[pl.BlockSpec((tm, tn), lambda i, j: (i, j)),
                   pl.BlockSpec((tm, U_SMALL), lambda i, j: (i, 0)),
                   pl.BlockSpec((tn, D_MODEL), lambda i, j: (j, 0))],
        out_shape=out_shape + [jax.ShapeDtypeStruct((U_MAIN, D_MODEL), bf16)],
        scratch_shapes=[pltpu.VMEM((tm, D_MODEL), bf16)],
        compiler_params=_params(2),
        name="inproj_cast",
    )(x, wts["norm1_w"], wts["w_in_t"], wts["w_small"])
    return u, us, dict(w_main=w_main)


def _ssd_kernel(z_ref, x_ref, bc_ref, sm_ref, hx_ref, hbc_ref, cwx_ref, cwbc_ref, cbx_ref, cbbc_ref,
                dtb_ref, alog_ref, dexp_ref, nw_ref, ehp_ref, s0_ref,
                y_ref, s_ref, tailx_ref, tailbc_ref, yz_ref, *, L, Ls, n_chunks):
    nseq = L // Ls
    c = pl.program_id(1)
    xpre = x_ref[...]
    bcpre = bc_ref[...]
    if nseq > 1:
        prev_x = [hx_ref[:, k:k + 1, :] for k in range(SSD_CONV - 1)]
        prev_bc = [hbc_ref[:, k:k + 1, :] for k in range(SSD_CONV - 1)]
        xc = _silu(_causal_conv_seqs(xpre, prev_x, cwx_ref[...], cbx_ref[...]))
        bcc = _silu(_causal_conv_seqs(bcpre, prev_bc, cwbc_ref[...], cbbc_ref[...]))
    else:
        if n_chunks == 1:
            hx, hbc = hx_ref[0], hbc_ref[0]
        else:
            first = c == 0
            hx = jnp.where(first, hx_ref[0], tailx_ref[...])
            hbc = jnp.where(first, hbc_ref[0], tailbc_ref[...])
        xc = _silu(_causal_conv(xpre, hx, cwx_ref[...], cbx_ref[...]))
        bcc = _silu(_causal_conv(bcpre, hbc, cwbc_ref[...], cbbc_ref[...]))
    if n_chunks > 1:
        tailx_ref[...] = xpre[L - SUBLANES:, :]
        tailbc_ref[...] = bcpre[L - SUBLANES:, :]

    lane = _iota2((L, LANES), 1)
    dt = jnp.where(lane < SSD_HEADS, _softplus(sm_ref[:, :LANES] + dtb_ref[...]), 0.0)
    dta = dt * (-jnp.exp(alog_ref[...]))

    causal, tri, tri_t, last = _seq_masks(L, Ls)
    eye = _eye()
    cum = _sel_left(tri, dta)
    cum_t = _sel_right(_transpose_exact(dta, eye), tri_t)
    cum_last = _sel_left(last, cum)
    ehp = ehp_ref[...]
    xdt = xc * _expand_heads(dt, ehp)
    xdtw = xdt * jnp.exp(_expand_heads(cum_last - cum, ehp))
    ecum = jnp.exp(_expand_heads(cum, ehp))

    if n_chunks > 1:
        @pl.when(c == 0)
        def _():
            s_ref[...] = s0_ref[...]
        sprev_ref = s_ref
    else:
        sprev_ref = s0_ref

    GE = SSD_D_INNER // SSD_GROUPS
    HPG = SSD_HEADS // SSD_GROUPS
    shift = Ls.bit_length() - 1
    seq_of_row = _iota2((L, 1), 0) >> shift
    lane_lo = lane < SSD_HEAD_DIM
    ys, intra = [], []
    for g in range(SSD_GROUPS):
        bm = bcc[:, g * SSD_STATE:(g + 1) * SSD_STATE]
        cm = bcc[:, (SSD_GROUPS + g) * SSD_STATE:(SSD_GROUPS + g + 1) * SSD_STATE]
        cb = _dot_nt(cm, bm)
        acc = None
        for j in range(nseq):
            cmj = cm if nseq == 1 else jnp.where(seq_of_row == j, cm, 0.0)
            t = _dot_nt(cmj, sprev_ref[j, g * GE:(g + 1) * GE, :])
            acc = t if acc is None else acc + t
        ys.append(acc)
        for hp in range(HPG // 2):
            col0 = g * GE + hp * LANES
            xpair = xdt[:, col0:col0 + LANES].astype(bf16)
            for e in range(2):
                h = g * HPG + hp * 2 + e
                seg = cum[:, h:h + 1] - cum_t[h:h + 1, :]
                m = jnp.exp(jnp.where(causal, seg, NEG)) * cb
                intra.append(jnp.dot(m.astype(bf16), xpair, preferred_element_type=f32))
    ssq = jnp.zeros((L, 1), f32)
    for g in range(SSD_GROUPS):
        for hp in range(HPG // 2):
            col0 = g * GE + hp * LANES
            blk = slice(col0, col0 + LANES)
            o0, o1 = intra[col0 // LANES * 2], intra[col0 // LANES * 2 + 1]
            y = (jnp.where(lane_lo, o0, o1)
                 + ecum[:, blk] * ys[g][:, hp * LANES:(hp + 1) * LANES]
                 + dexp_ref[:, blk] * xc[:, blk])
            yz = y * _silu(z_ref[:, blk])
            ssq = ssq + jnp.sum(yz * yz, axis=-1, keepdims=True)
            yz_ref[:, blk] = yz
    r = lax.rsqrt(ssq * (1.0 / SSD_D_INNER) + EPS)
    y_ref[...] = ((yz_ref[...] * r) * nw_ref[...]).astype(bf16)

    for j in range(nseq):
        tl = (j + 1) * Ls - 1
        dec = jnp.exp(jnp.broadcast_to(cum_t[:, tl:tl + 1], (LANES, LANES)))
        for g in range(SSD_GROUPS):
            bm = bcc[:, g * SSD_STATE:(g + 1) * SSD_STATE]
            xw = xdtw[:, g * GE:(g + 1) * GE]
            if nseq > 1:
                xw = jnp.where(seq_of_row == j, xw, 0.0)
            ds = _dot_tn(xw, bm)
            for e in range(HPG):
                h = g * HPG + e
                rs = slice(h * SSD_HEAD_DIM, (h + 1) * SSD_HEAD_DIM)
                s_ref[j, rs, :] = (dec[h:h + 1, :] * sprev_ref[j, rs, :]
                                   + ds[e * SSD_HEAD_DIM:(e + 1) * SSD_HEAD_DIM, :])


def _ssd(u, us, hx, hbc, s0, wts, *, B, T, L, Ls):
    nseq = L // Ls
    n_chunks = T // Ls if nseq == 1 else 1
    nblk = B // nseq
    rb = lambda i, c: i * n_chunks + c
    if nseq > 1:
        assert Ls == SUBLANES
        h_specs = [pl.BlockSpec((nseq, SSD_CONV - 1, SSD_D_INNER), lambda i, c: (i, 0, 0)),
                   pl.BlockSpec((nseq, SSD_CONV - 1, SSD_BC), lambda i, c: (i, 0, SSD_D_INNER // SSD_BC))]
    else:
        h_specs = [pl.BlockSpec((1, SUBLANES, SSD_D_INNER), lambda i, c: (i, 0, 0)),
                   pl.BlockSpec((1, SUBLANES, SSD_BC), lambda i, c: (i, 0, 0))]
    const = lambda shape: pl.BlockSpec(shape, lambda i, c: (0,) * len(shape))
    kern = functools.partial(_ssd_kernel, L=L, Ls=Ls, n_chunks=n_chunks)
    return pl.pallas_call(
        kern,
        grid=(nblk, n_chunks),
        in_specs=[pl.BlockSpec((L, SSD_D_INNER), lambda i, c: (rb(i, c), U_Z)),
                  pl.BlockSpec((L, SSD_D_INNER), lambda i, c: (rb(i, c), U_X)),
                  pl.BlockSpec((L, SSD_BC), lambda i, c: (rb(i, c), U_BC)),
                  pl.BlockSpec((L, U_SMALL), lambda i, c: (rb(i, c), 0)),
                  *h_specs,
                  const((SSD_CONV, SSD_D_INNER)), const((SSD_CONV, SSD_BC)),
                  const((1, SSD_D_INNER)), const((1, SSD_BC)),
                  const((1, LANES)), const((1, LANES)),
                  const((1, SSD_D_INNER)), const((1, SSD_D_INNER)),
                  const((2 * LANES, SSD_D_INNER)),
                  pl.BlockSpec((nseq, SSD_D_INNER, SSD_STATE), lambda i, c: (i, 0, 0))],
        out_specs=[pl.BlockSpec((L, SSD_D_INNER), lambda i, c: (rb(i, c), 0)),
                   pl.BlockSpec((nseq, SSD_D_INNER, SSD_STATE), lambda i, c: (i, 0, 0))],
        out_shape=[jax.ShapeDtypeStruct((B * T, SSD_D_INNER), bf16),
                   jax.ShapeDtypeStruct((B, SSD_D_INNER, SSD_STATE), f32)],
        scratch_shapes=[pltpu.VMEM((SUBLANES, SSD_D_INNER), f32),
                        pltpu.VMEM((SUBLANES, SSD_BC), f32),
                        pltpu.VMEM((L, SSD_D_INNER), f32)],
        compiler_params=_params(2),
        name="ssd",
    )(u, u, u, us, hx, hbc, wts["cw_x"], wts["cw_bc"], wts["cb_x"], wts["cb_bc"],
      wts["dt_bias"], wts["a_log"], wts["d_exp"], wts["ssd_norm_w"], wts["ehp"], s0)


def _interleave(gens):
    while gens:
        alive = []
        for g in gens:
            try:
                next(g)
                alive.append(g)
            except StopIteration:
                pass
        gens = alive


def _group_views(refs, g, nseq):
    return [r.at[g] if kind == "tok" else r.at[pl.ds(g * nseq, nseq)] if kind == "state" else r
            for r, kind in refs]


def _mlstm_kernel(*refs, L, Ls, n_chunks, groups):
    kinds = ["tok"] * 5 + ["const"] * 3 + ["state"] * 3 + ["tok"] + ["state"] * 3
    _interleave([_mlstm_chunk(*_group_views(list(zip(refs, kinds)), g, L // Ls), L=L, Ls=Ls, n_chunks=n_chunks)
                 for g in range(groups)])


def _mlstm_chunk(q_ref, k_ref, v_ref, o_ref, sm_ref, ib_ref, fb_ref, nw_ref, c0_ref, n0_ref, m0_ref,
                 y_ref, c_ref, n_ref, m_ref, *, L, Ls, n_chunks):
    nseq = L // Ls
    c = pl.program_id(1)
    if n_chunks > 1:
        @pl.when(c == 0)
        def _():
            c_ref[...] = c0_ref[...]
            n_ref[...] = n0_ref[...]
            m_ref[...] = m0_ref[...]
        cprev_ref, nprev_ref, mprev_ref = c_ref, n_ref, m_ref
    else:
        cprev_ref, nprev_ref, mprev_ref = c0_ref, n0_ref, m0_ref

    per_tok = lambda a: jnp.broadcast_to(a, (nseq, Ls, LANES)).reshape(L, LANES)
    ig = sm_ref[:, :LANES] + ib_ref[...]
    fraw = sm_ref[:, LANES:] + fb_ref[...]
    lf = -_softplus(-fraw)
    causal, tri, _, last = _seq_masks(L, Ls)
    eye = _eye()
    F = _sel_left(tri, lf)
    FL = _sel_left(last, F)
    mp = per_tok(mprev_ref[:, 0:1, :])
    r_t = _transpose_exact(ig - F, eye)
    inter = F + mp
    lw = FL - F + ig
    segmax = jnp.max(lw.reshape(nseq, Ls, LANES), axis=1, keepdims=True)
    m_new = jnp.maximum(FL + mp, per_tok(segmax))
    sc = jnp.exp(lw - m_new)
    dec = jnp.exp(FL + mp - m_new)
    m_out = m_new.reshape(nseq, Ls, LANES)[:, 0:SUBLANES, :]
    yield

    shift = Ls.bit_length() - 1
    seq_of_row = _iota2((L, 1), 0) >> shift
    kscale = ML_QK_DIM ** -0.5
    heads = range(ML_HEADS)
    qcols = lambda h: slice(h * ML_QK_DIM, (h + 1) * ML_QK_DIM)
    vcols = lambda h: slice(h * ML_V_DIM, (h + 1) * ML_V_DIM)
    q = [q_ref[:, qcols(h)] for h in heads]
    k = [k_ref[:, qcols(h)] * kscale for h in heads]
    v = [v_ref[:, vcols(h)].astype(bf16) for h in heads]
    qk = [_dot_nt(q[h], k[h]) for h in heads]
    yield
    qc = []
    for h in heads:
        acc = None
        for j in range(nseq):
            qj = q[h] if nseq == 1 else jnp.where(seq_of_row == j, q[h], 0.0)
            t = _dot(qj, cprev_ref[j, qcols(h), :])
            acc = t if acc is None else acc + t
        qc.append(acc)
    yield
    rep = lambda col: jnp.broadcast_to(col, (L, LANES))
    twice = lambda a: jnp.concatenate([a, a], axis=1)
    qn = []
    for h in heads:
        acc = None
        for j in range(nseq):
            qj = q[h] if nseq == 1 else jnp.where(seq_of_row == j, q[h], 0.0)
            t = _dot_nt(qj, jnp.broadcast_to(nprev_ref[j, h:h + 1, :], (LANES, ML_QK_DIM)))
            acc = t if acc is None else acc + t
        qn.append(acc)
    yield
    dm, m_rep, inter_rep = [], [], []
    for h in heads:
        gl = GATE_LANE + h
        d = jnp.where(causal, F[:, gl:gl + 1] + r_t[gl:gl + 1, :], NEG)
        dm.append(d)
        inter_rep.append(rep(inter[:, gl:gl + 1]))
        m_rep.append(jnp.maximum(rep(jnp.max(d, axis=-1, keepdims=True)), inter_rep[h]))
    yield
    w = [jnp.exp(dm[h] - (m_rep[h] if L == LANES else m_rep[h][:, 0:1])) * qk[h] for h in heads]
    yield
    wv = [jnp.dot(w[h].astype(bf16), v[h], preferred_element_type=f32) for h in heads]
    wsum = [_rowsum_mxu(w[h]) for h in heads]
    yield
    hh = []
    for h in heads:
        wi = jnp.exp(inter_rep[h] - m_rep[h])
        den = wsum[h] + wi * qn[h]
        inv = 1.0 / jnp.maximum(jnp.abs(den), jnp.exp(-m_rep[h]))
        hh.append((wv[h] + twice(wi) * qc[h]) * twice(inv))
    yield
    ssq = [_rowsum_mxu(hh[h] * hh[h]) for h in heads]
    yield
    for h in heads:
        r = lax.rsqrt(ssq[h] * (1.0 / ML_V_DIM) + EPS)
        hn = (hh[h] * twice(r)) * nw_ref[:, vcols(h)]
        y_ref[:, vcols(h)] = (_sigmoid(o_ref[:, vcols(h)]) * hn).astype(bf16)
        if h % 2 == 1:
            yield
    for h in heads:
        gl = GATE_LANE + h
        ksc = k[h] * sc[:, gl:gl + 1]
        for j in range(nseq):
            r0 = j * Ls
            dj = dec[r0:r0 + 1, gl:gl + 1]
            kj = ksc if nseq == 1 else jnp.where(seq_of_row == j, ksc, 0.0)
            c_ref[j, qcols(h), :] = dj * cprev_ref[j, qcols(h), :] + _dot_tn(kj, v[h])
            n_ref[j, h:h + 1, :] = (dj * nprev_ref[j, h:h + 1, :]
                                    + jnp.sum(ksc[r0:r0 + Ls, :], axis=0, keepdims=True))
        if h % 2 == 1:
            yield
    m_ref[...] = m_out


def _scan_groups(nblk, nseq):
    return SCAN_GROUPS if nseq == 1 and nblk % SCAN_GROUPS == 0 else 1


def _mlstm(u, us, c0, n0, m0, wts, *, B, T, L, Ls):
    nseq = L // Ls
    n_chunks = T // Ls if nseq == 1 else 1
    nblk = B // nseq
    G = _scan_groups(nblk, nseq)
    rows = n_chunks * L
    u3, us3 = u.reshape(nblk, rows, U_MAIN), us.reshape(nblk, rows, U_SMALL)
    tok = lambda width, col: pl.BlockSpec((G, L, width), lambda i, c: (i, c, col))
    const = lambda shape: pl.BlockSpec(shape, lambda i, c: (0,) * len(shape))
    st = lambda shape: pl.BlockSpec((G * nseq,) + shape, lambda i, c: (i, 0, 0))
    kern = functools.partial(_mlstm_kernel, L=L, Ls=Ls, n_chunks=n_chunks, groups=G)
    y, c_new, n_new, m_new = pl.pallas_call(
        kern,
        grid=(nblk // G, n_chunks),
        in_specs=[tok(ML_QK_INNER, U_Q), tok(ML_QK_INNER, U_K), tok(ML_D_INNER, U_V), tok(ML_D_INNER, U_O),
                  tok(U_SMALL, 0),
                  const((1, LANES)), const((1, LANES)), const((1, ML_D_INNER)),
                  st((ML_QK_INNER, ML_V_DIM)), st((ML_HEADS, ML_QK_DIM)), st((SUBLANES, LANES))],
        out_specs=[tok(ML_D_INNER, 0),
                   st((ML_QK_INNER, ML_V_DIM)), st((ML_HEADS, ML_QK_DIM)), st((SUBLANES, LANES))],
        out_shape=[jax.ShapeDtypeStruct((nblk, rows, ML_D_INNER), bf16),
                   jax.ShapeDtypeStruct((B, ML_QK_INNER, ML_V_DIM), f32),
                   jax.ShapeDtypeStruct((B, ML_HEADS, ML_QK_DIM), f32),
                   jax.ShapeDtypeStruct((B, SUBLANES, LANES), f32)],
        compiler_params=_params(2),
        name="mlstm",
    )(u3, u3, u3, u3, us3, wts["i_bias"], wts["f_bias"], wts["ml_norm_w"], c0, n0, m0)
    return y.reshape(B * T, ML_D_INNER), c_new, n_new, m_new


def _outproj_cast_kernel(ys_ref, ym_ref, ws_ref, wm_ref, h_ref, o_ref, wsb_ref, wmb_ref):
    @pl.when(pl.program_id(1) == 0)
    def _():
        wsb_ref[...] = ws_ref[...].astype(bf16)
        wmb_ref[...] = wm_ref[...].astype(bf16)

    o_ref[...] = (h_ref[...]
                  + jnp.dot(ys_ref[...], wsb_ref[...], preferred_element_type=f32)
                  + jnp.dot(ym_ref[...], wmb_ref[...], preferred_element_type=f32))


def _outproj_kernel(ys_ref, ym_ref, ws_ref, wm_ref, h_ref, o_ref):
    o_ref[...] = (h_ref[...]
                  + jnp.dot(ys_ref[...], ws_ref[...], preferred_element_type=f32)
                  + jnp.dot(ym_ref[...], wm_ref[...], preferred_element_type=f32))


def _outproj(ys, ym, wts, h):
    M = h.shape[0]
    tm = _row_tile(M)
    if "w_out_s" in wts:
        tn = OUTPROJ_TILE
        h1 = pl.pallas_call(
            _outproj_kernel,
            grid=(M // tm, D_MODEL // tn),
            in_specs=[pl.BlockSpec((tm, SSD_D_INNER), lambda i, j: (i, 0)),
                      pl.BlockSpec((tm, ML_D_INNER), lambda i, j: (i, 0)),
                      pl.BlockSpec((SSD_D_INNER, tn), lambda i, j: (0, j)),
                      pl.BlockSpec((ML_D_INNER, tn), lambda i, j: (0, j)),
                      pl.BlockSpec((tm, tn), lambda i, j: (i, j))],
            out_specs=pl.BlockSpec((tm, tn), lambda i, j: (i, j)),
            out_shape=jax.ShapeDtypeStruct((M, D_MODEL), f32),
            compiler_params=_params(2),
            name="outproj",
        )(ys, ym, wts["w_out_s"], wts["w_out_m"], h)
        return h1, {}
    tn = OUTPROJ_CAST_TILE
    wspec = pl.BlockSpec((SSD_D_INNER, tn), lambda j, i: (0, j))
    wshape = jax.ShapeDtypeStruct((SSD_D_INNER, D_MODEL), bf16)
    h1, ws, wm = pl.pallas_call(
        _outproj_cast_kernel,
        grid=(D_MODEL // tn, M // tm),
        in_specs=[pl.BlockSpec((tm, SSD_D_INNER), lambda j, i: (i, 0)),
                  pl.BlockSpec((tm, ML_D_INNER), lambda j, i: (i, 0)),
                  wspec,
                  pl.BlockSpec((ML_D_INNER, tn), lambda j, i: (1, j)),
                  pl.BlockSpec((tm, tn), lambda j, i: (i, j))],
        out_specs=[pl.BlockSpec((tm, tn), lambda j, i: (i, j)), wspec, wspec],
        out_shape=[jax.ShapeDtypeStruct((M, D_MODEL), f32), wshape, wshape],
        compiler_params=_params(2),
        name="outproj_cast",
    )(ys, ym, wts["w_out"], wts["w_out"], h)
    return h1, dict(w_out_s=ws, w_out_m=wm)


def _ffn_kernel(h_ref, nw_ref, wg_ref, wv_ref, cwg_ref, cwv_ref, cbg_ref, cbv_ref, wd_ref, fw_ref, *rest,
                tm, tf, multi, blocks_per_seq, cast):
    if multi:
        hg_ref, hv_ref, y_ref, tg_ref, tv_ref, xn_ref = rest
    elif cast:
        hg_ref, hv_ref, y_ref, tg_ref, tv_ref, wgb_ref, wvb_ref, wdb_ref, xn_ref, carg_ref, carv_ref = rest
    else:
        hg_ref, hv_ref, y_ref, tg_ref, tv_ref, xn_ref, carg_ref, carv_ref = rest
    i = pl.program_id(0)
    j = pl.program_id(1)

    @pl.when(j == 0)
    def _():
        xn_ref[...] = _rms(h_ref[...], nw_ref[...]).astype(bf16)
        y_ref[...] = jnp.zeros_like(y_ref)

    if cast:
        wgb_ref[...] = wg_ref[...].astype(bf16)
        wvb_ref[...] = wv_ref[...].astype(bf16)
        wdb_ref[...] = wd_ref[...].astype(bf16)
        wg_ref, wv_ref, wd_ref = wgb_ref, wvb_ref, wdb_ref

    xn = xn_ref[...]
    subs = [slice(c0, c0 + MXU_COLS) for c0 in range(0, tf, MXU_COLS)]
    up_dots = lambda cs: [jnp.dot(xn, w_ref[:, cs], preferred_element_type=f32) for w_ref in (wg_ref, wv_ref)]
    ups_next = up_dots(subs[0])
    for n, cs in enumerate(subs):
        ups = ups_next
        if n + 1 < len(subs):
            ups_next = up_dots(subs[n + 1])
        convd = []
        for half, (up, cw_ref, cb_ref) in enumerate(zip(ups, (cwg_ref, cwv_ref), (cbg_ref, cbv_ref))):
            if multi:
                s_ref, t_ref = ((hg_ref, tg_ref), (hv_ref, tv_ref))[half]
                prev = [s_ref[:, k:k + 1, cs] for k in range(FFN_CONV - 1)]
                convd.append(_causal_conv_seqs(up, prev, cw_ref[:, cs], cb_ref[:, cs]))
                up3 = up.reshape(tm // SUBLANES, SUBLANES, MXU_COLS)
                t_ref[:, :, cs] = up3[:, SUBLANES - (FFN_CONV - 1):, :]
            else:
                h_ref_, car_ref, t_ref = ((hg_ref, carg_ref, tg_ref), (hv_ref, carv_ref, tv_ref))[half]
                tail = up[tm - SUBLANES:, :]
                if blocks_per_seq == 1:
                    hist = h_ref_[0, :, cs]
                else:
                    hist = jnp.where((i % blocks_per_seq) == 0, h_ref_[0, :, cs], car_ref[j, :, cs])
                    car_ref[j, :, cs] = tail
                t_ref[0, :, cs] = tail
                convd.append(_causal_conv(up, hist, cw_ref[:, cs], cb_ref[:, cs]))
        act = (_silu(convd[0]) * convd[1]).astype(bf16)
        y_ref[...] += jnp.dot(act, wd_ref[cs, :], preferred_element_type=f32)

    @pl.when(j == pl.num_programs(1) - 1)
    def _():
        y_ref[...] = _rms(h_ref[...] + y_ref[...], fw_ref[...])


def _ffn(h, ffn0, wts, *, B, T):
    M = h.shape[0]
    tm = _row_tile(M)
    tf = FFN_TILE
    n_ff = D_FF // tf
    multi = T < tm
    const = lambda shape: pl.BlockSpec(shape, lambda i, j: (0,) * len(shape))
    if multi:
        assert T == SUBLANES
        blocks_per_seq = 1
        nseq = tm // T
        hist = [ffn0, ffn0]
        h_specs = [pl.BlockSpec((nseq, FFN_CONV - 1, tf), lambda i, j: (i, 0, j)),
                   pl.BlockSpec((nseq, FFN_CONV - 1, tf), lambda i, j: (i, 0, n_ff + j))]
        t_specs = [pl.BlockSpec((nseq, FFN_CONV - 1, tf), lambda i, j: (i, 0, j))] * 2
        t_shapes = [jax.ShapeDtypeStruct((B, FFN_CONV - 1, D_FF), f32)] * 2
        scratch = []
    else:
        blocks_per_seq = T // tm
        pad = jnp.pad(ffn0, ((0, 0), (SUBLANES - (FFN_CONV - 1), 0), (0, 0)))
        hist = [pad, pad]
        h_specs = [pl.BlockSpec((1, SUBLANES, tf), lambda i, j: (i // blocks_per_seq, 0, j)),
                   pl.BlockSpec((1, SUBLANES, tf), lambda i, j: (i // blocks_per_seq, 0, n_ff + j))]
        t_specs = [pl.BlockSpec((1, SUBLANES, tf), lambda i, j: (i, 0, j))] * 2
        t_shapes = [jax.ShapeDtypeStruct((M // tm, SUBLANES, D_FF), f32)] * 2
        scratch = [pltpu.VMEM((n_ff, SUBLANES, tf), f32)] * 2
    cast = "w_up_g" not in wts
    assert not (cast and (multi or M != tm)), "the casting call must see every weight tile exactly once"
    up_spec = pl.BlockSpec((D_MODEL, tf), lambda i, j: (0, j))
    down_spec = pl.BlockSpec((tf, D_MODEL), lambda i, j: (j, 0))
    if cast:
        w_specs = [up_spec, pl.BlockSpec((D_MODEL, tf), lambda i, j: (0, n_ff + j)), down_spec]
        w_args = [wts["w_up"], wts["w_up"], wts["w_down"]]
        wb_specs = [up_spec, up_spec, down_spec]
        wb_shapes = [jax.ShapeDtypeStruct((D_MODEL, D_FF), bf16)] * 2 + [jax.ShapeDtypeStruct((D_FF, D_MODEL), bf16)]
    else:
        w_specs = [up_spec, up_spec, down_spec]
        w_args = [wts["w_up_g"], wts["w_up_v"], wts["w_down_b"]]
        wb_specs, wb_shapes = [], []
    kern = functools.partial(_ffn_kernel, tm=tm, tf=tf, multi=multi, blocks_per_seq=blocks_per_seq, cast=cast)
    y, *outs = pl.pallas_call(
        kern,
        grid=(M // tm, n_ff),
        in_specs=[pl.BlockSpec((tm, D_MODEL), lambda i, j: (i, 0), pipeline_mode=pl.Buffered(1)),
                  const((1, D_MODEL)),
                  w_specs[0], w_specs[1],
                  pl.BlockSpec((FFN_CONV, tf), lambda i, j: (0, j)),
                  pl.BlockSpec((FFN_CONV, tf), lambda i, j: (0, n_ff + j)),
                  pl.BlockSpec((1, tf), lambda i, j: (0, j)),
                  pl.BlockSpec((1, tf), lambda i, j: (0, n_ff + j)),
                  w_specs[2],
                  const((1, D_MODEL)),
                  *h_specs],
        out_specs=[pl.BlockSpec((tm, D_MODEL), lambda i, j: (i, 0)), *t_specs, *wb_specs],
        out_shape=[jax.ShapeDtypeStruct((M, D_MODEL), f32), *t_shapes, *wb_shapes],
        scratch_shapes=[pltpu.VMEM((tm, D_MODEL), bf16), *scratch],
        compiler_params=_params(2),
        name="ffn_cast" if cast else "ffn",
    )(h, wts["norm2_w"], w_args[0], w_args[1], wts["ffn_cw"], wts["ffn_cw"],
      wts["ffn_cb"], wts["ffn_cb"], w_args[2], wts["final_norm_w"], *hist)
    tails, new_w = outs[:2], {}
    if cast:
        new_w = dict(w_up_g=outs[2], w_up_v=outs[3], w_down_b=outs[4])
    if multi:
        return y, jnp.concatenate(tails, axis=-1), new_w
    last = jnp.concatenate(tails, axis=-1).reshape(B, blocks_per_seq, SUBLANES, 2 * D_FF)
    return y, last[:, blocks_per_seq - 1, SUBLANES - (FFN_CONV - 1):, :], new_w


def _hist_tile(state):
    return jnp.pad(state, ((0, 0), (SUBLANES - state.shape[1], 0), (0, 0)))


def _layer(h, states, wts, *, B, T, L, Ls):
    conv0, s0, c0, n0, m0, ffn0 = states
    u, us, cast_w = _inproj(h, wts)
    wts = {**wts, **cast_w}
    if L // Ls > 1:
        hx = hbc = conv0
    else:
        hx, hbc = _hist_tile(conv0[:, :, :SSD_D_INNER]), _hist_tile(conv0[:, :, SSD_D_INNER:])
    y_ssd, s_new = _ssd(u, us, hx, hbc, s0.reshape(B, SSD_D_INNER, SSD_STATE), wts, B=B, T=T, L=L, Ls=Ls)
    m_pad = jnp.broadcast_to(
        jnp.pad(m0, ((0, 0), (GATE_LANE, LANES - GATE_LANE - ML_HEADS)))[:, None, :], (B, SUBLANES, LANES))
    y_ml, c_new, n_new, m_new = _mlstm(u, us, c0.reshape(B, ML_QK_INNER, ML_V_DIM), n0, m_pad, wts,
                                       B=B, T=T, L=L, Ls=Ls)
    h1, cast_w = _outproj(y_ssd, y_ml, wts, h)
    wts = {**wts, **cast_w}
    y, ffn_new, cast_w = _ffn(h1, ffn0, wts, B=B, T=T)
    wts = {**wts, **cast_w}
    ur = u.reshape(B, T, U_MAIN)[:, T - (SSD_CONV - 1):, :]
    conv_new = jnp.concatenate([ur[:, :, U_X * SSD_D_INNER:(U_X + 1) * SSD_D_INNER],
                                ur[:, :, U_BC * SSD_BC:(U_BC + 1) * SSD_BC]], axis=-1)
    new_states = (conv_new,
                  s_new.reshape(B, SSD_HEADS, SSD_HEAD_DIM, SSD_STATE),
                  c_new.reshape(B, ML_HEADS, ML_QK_DIM, ML_V_DIM),
                  n_new,
                  m_new[:, 0, GATE_LANE:GATE_LANE + ML_HEADS],
                  ffn_new)
    return y, new_states, wts


def _prep_weights(norm1_w, w_in, ssd_conv_w, ssd_conv_b, ssd_dt_bias, ssd_A_log, ssd_D, ssd_norm_w,
                  ml_i_bias, ml_f_bias, ml_norm_w, w_out, norm2_w, w_up, ffn_conv_w, ffn_conv_b, w_down,
                  final_norm_w):
    w_t = w_in.T
    o = 0
    rows, offs = {}, {}
    for name, width in (("z", SSD_D_INNER), ("x", SSD_D_INNER), ("bc", SSD_BC), ("dt", SSD_HEADS),
                        ("q", ML_QK_INNER), ("k", ML_QK_INNER), ("v", ML_D_INNER), ("i", ML_HEADS),
                        ("f", ML_HEADS), ("o", ML_D_INNER)):
        offs[name] = (o, width)
        if name in ("dt", "i", "f"):
            rows[name] = lax.optimization_barrier(w_t[o:o + width, :]).astype(bf16)
        o += width
    zpad = lambda n: jnp.zeros((n, D_MODEL), bf16)
    w_small = jnp.concatenate([rows["dt"], rows["i"], zpad(LANES - GATE_LANE - ML_HEADS),
                               zpad(GATE_LANE), rows["f"], zpad(LANES - GATE_LANE - ML_HEADS)], axis=0)
    lane_row = lambda v, off: jnp.pad(v.astype(f32), (off, LANES - off - v.shape[0]))[None, :]
    hp = jnp.arange(SSD_D_INNER) // SSD_HEAD_DIM
    ehp = (jnp.arange(LANES)[:, None] == hp[None, :]).astype(bf16)
    return dict(
        norm1_w=norm1_w[None, :], w_in_t=w_t, w_in_offs=offs, w_small=w_small,
        cw_x=ssd_conv_w[:, :SSD_D_INNER], cw_bc=ssd_conv_w[:, SSD_D_INNER:],
        cb_x=ssd_conv_b[None, :SSD_D_INNER], cb_bc=ssd_conv_b[None, SSD_D_INNER:],
        dt_bias=lane_row(ssd_dt_bias, 0), a_log=lane_row(ssd_A_log, 0),
        d_exp=jnp.repeat(ssd_D.astype(f32), SSD_HEAD_DIM)[None, :], ssd_norm_w=ssd_norm_w[None, :],
        ehp=jnp.concatenate([ehp, ehp], axis=0),
        i_bias=lane_row(ml_i_bias, GATE_LANE), f_bias=lane_row(ml_f_bias, GATE_LANE),
        ml_norm_w=ml_norm_w[None, :],
        w_out=w_out, norm2_w=norm2_w[None, :], w_up=w_up,
        ffn_cw=ffn_conv_w, ffn_cb=ffn_conv_b[None, :], w_down=w_down,
        final_norm_w=final_norm_w[None, :])


def kernel(x_prompt, x_sample, state_ssd_conv, state_ssd, state_mlstm_C, state_mlstm_n, state_mlstm_m,
           state_ffn_conv, meta_tokens, norm1_w, w_in, ssd_conv_w, ssd_conv_b, ssd_dt_bias, ssd_A_log,
           ssd_D, ssd_norm_w, ml_i_bias, ml_f_bias, ml_norm_w, w_out, norm2_w, w_up, ffn_conv_w,
           ffn_conv_b, w_down, final_norm_w):
    depth = w_in.shape[0]
    assert depth == 1, "single-layer step"
    Bp, Tp, _ = x_prompt.shape
    Bs, Ts, _ = x_sample.shape
    wts = _prep_weights(norm1_w[0], w_in[0], ssd_conv_w[0], ssd_conv_b[0], ssd_dt_bias[0], ssd_A_log[0],
                        ssd_D[0], ssd_norm_w[0], ml_i_bias[0], ml_f_bias[0], ml_norm_w[0], w_out[0],
                        norm2_w[0], w_up[0], ffn_conv_w[0], ffn_conv_b[0], w_down[0], final_norm_w)
    zero_states = (jnp.zeros((1, SSD_CONV - 1, SSD_CONV_DIM), f32),
                   jnp.zeros((1, SSD_HEADS, SSD_HEAD_DIM, SSD_STATE), f32),
                   jnp.zeros((1, ML_HEADS, ML_QK_DIM, ML_V_DIM), f32),
                   jnp.zeros((1, ML_HEADS, ML_QK_DIM), f32),
                   jnp.zeros((1, ML_HEADS), f32),
                   jnp.zeros((1, FFN_CONV - 1, 2 * D_FF), f32))
    _, meta_states, wts = _layer(meta_tokens.astype(f32), zero_states, wts, B=1, T=N_META, L=N_META, Ls=N_META)
    p_init = tuple(jnp.broadcast_to(s, (Bp,) + s.shape[1:]) for s in meta_states)
    yp, p_new, _ = _layer(x_prompt.reshape(Bp * Tp, D_MODEL), p_init, wts, B=Bp, T=Tp, L=128, Ls=128)
    s_init = (state_ssd_conv[0], state_ssd[0], state_mlstm_C[0], state_mlstm_n[0], state_mlstm_m[0],
              state_ffn_conv[0])
    ys, s_new, _ = _layer(x_sample.reshape(Bs * Ts, D_MODEL), s_init, wts, B=Bs, T=Ts, L=8 * Ts, Ls=Ts)
    return (yp.reshape(Bp, Tp, D_MODEL), ys.reshape(Bs, Ts, D_MODEL),
            *(s[None] for s in p_new), *(s[None] for s in s_new))
```

```python
import functools

import jax
import jax.numpy as jnp
from jax import lax
from jax.experimental import pallas as pl
from jax.experimental.pallas import tpu as pltpu

f32 = jnp.float32
bf16 = jnp.bfloat16

D_MODEL = 2048
N_META = 16
SSD_HEADS = 32
SSD_HEAD_DIM = 64
SSD_D_INNER = SSD_HEADS * SSD_HEAD_DIM
SSD_GROUPS = 2
SSD_STATE = 128
SSD_CONV = 4
SSD_BC = 2 * SSD_GROUPS * SSD_STATE
SSD_CONV_DIM = SSD_D_INNER + SSD_BC
ML_HEADS = 8
ML_QK_DIM = 128
ML_V_DIM = 256
ML_QK_INNER = ML_HEADS * ML_QK_DIM
ML_D_INNER = ML_HEADS * ML_V_DIM
D_FF = 5632
FFN_CONV = 3
EPS = 1e-6
NEG = -1e30

LANES = 128
SUBLANES = 8
VMEM_LIMIT = 56 * 1024 * 1024
MXU_COLS = 256
ROW_TILE = 1024
FFN_TILE = 512
INPROJ_TILE = 1536
OUTPROJ_TILE = 1024
OUTPROJ_CAST_TILE = 512
INPROJ_CAST_TILE = 512
SCAN_GROUPS = 2


def _row_tile(M):
    return ROW_TILE if M % ROW_TILE == 0 else M


U_Z, U_X, U_V, U_O = 0, 1, 2, 3
U_Q, U_K = 8, 9
U_BC = 20
U_MAIN = 4 * 2048 + 2 * 1024 + 512
GATE_LANE = 32
U_SMALL = 2 * LANES


def _dot(a, b):
    return jnp.dot(a.astype(bf16), b.astype(bf16), preferred_element_type=f32)


def _dot_nt(a, b):
    return lax.dot_general(a.astype(bf16), b.astype(bf16), (((1,), (1,)), ((), ())),
                           preferred_element_type=f32)


def _dot_tn(a, b):
    return lax.dot_general(a.astype(bf16), b.astype(bf16), (((0,), (0,)), ((), ())),
                           preferred_element_type=f32)


def _split3(a):
    hi = a.astype(bf16)
    r1 = a - hi.astype(f32)
    mid = r1.astype(bf16)
    lo = (r1 - mid.astype(f32)).astype(bf16)
    return hi, mid, lo


def _sel_right(a, e01):
    return jnp.dot(jnp.concatenate(_split3(a), axis=1), jnp.concatenate([e01] * 3, axis=0),
                   preferred_element_type=f32)


def _sel_left(e01, a):
    return jnp.dot(jnp.concatenate([e01] * 3, axis=1), jnp.concatenate(_split3(a), axis=0),
                   preferred_element_type=f32)


def _split2(a):
    hi = a.astype(bf16)
    mid = (a - hi.astype(f32)).astype(bf16)
    return jnp.concatenate([hi, mid], axis=1)


def _expand_heads(a, e01x2):
    return jnp.dot(_split2(a), e01x2, preferred_element_type=f32)


def _rowsum_mxu(a):
    return jnp.dot(_split2(a), jnp.ones((2 * a.shape[1], LANES), bf16), preferred_element_type=f32)


def _transpose_exact(a, eye):
    return lax.dot_general(jnp.concatenate([eye] * 3, axis=1), jnp.concatenate(_split3(a), axis=1),
                           (((1,), (1,)), ((), ())), preferred_element_type=f32)


def _iota2(shape, axis):
    return lax.broadcasted_iota(jnp.int32, shape, axis)


def _as01(m):
    return jnp.where(m, 1.0, 0.0).astype(bf16)


def _eye():
    return _as01(_iota2((LANES, LANES), 0) == _iota2((LANES, LANES), 1))


def _seq_masks(L, Ls):
    t = _iota2((L, L), 0)
    s = _iota2((L, L), 1)
    shift = Ls.bit_length() - 1
    same = (t >> shift) == (s >> shift)
    causal = same & (s <= t)
    causal_t = same & (t <= s)
    last = s == (t | (Ls - 1))
    return causal, _as01(causal), _as01(causal_t), _as01(last)


def _sigmoid(x):
    return 1.0 / (1.0 + jnp.exp(-x))


def _silu(x):
    return x * _sigmoid(x)


def _softplus(x):
    return jnp.maximum(x, 0.0) + jnp.log(1.0 + jnp.exp(-jnp.abs(x)))


def _rms(x, w):
    r = lax.rsqrt(jnp.mean(x * x, axis=-1, keepdims=True) + EPS)
    return (x * r) * w


def _causal_conv(x, hist, w, b):
    L, C = x.shape
    K = w.shape[0]
    r = _iota2((SUBLANES, C), 0)
    y = b + x * w[K - 1:K, :]
    for s in range(1, K):
        zt = jnp.where(r >= SUBLANES - s, hist, x[L - SUBLANES:, :])
        z = zt if L == SUBLANES else jnp.concatenate([x[:L - SUBLANES, :], zt], axis=0)
        y = y + pltpu.roll(z, s, 0) * w[K - 1 - s:K - s, :]
    return y


def _causal_conv_seqs(x, prev, w, b):
    L, C = x.shape
    K = w.shape[0]
    nseq = L // SUBLANES
    r = _iota2((L, C), 0) & (SUBLANES - 1)
    per_row = lambda a: jnp.broadcast_to(a, (nseq, SUBLANES, C)).reshape(L, C)
    y = b + x * w[K - 1:K, :]
    for s in range(1, K):
        head = per_row(prev[K - 1 - s])
        for rr in range(1, s):
            head = jnp.where(r == rr, per_row(prev[K - 1 - s + rr]), head)
        y = y + jnp.where(r >= s, pltpu.roll(x, s, 0), head) * w[K - 1 - s:K - s, :]
    return y


def _params(n_axes):
    return pltpu.CompilerParams(dimension_semantics=("arbitrary",) * n_axes,
                                vmem_limit_bytes=VMEM_LIMIT)


def _inproj_kernel(x_ref, nw_ref, w_ref, ws_ref, u_ref, us_ref, xn_ref):
    @pl.when(pl.program_id(1) == 0)
    def _():
        xn = _rms(x_ref[...], nw_ref[...]).astype(bf16)
        xn_ref[...] = xn
        us_ref[...] = _dot_nt(xn, ws_ref[...])

    u_ref[...] = _dot_nt(xn_ref[...], w_ref[...])


def _inproj_cast_kernel(x_ref, nw_ref, w_ref, ws_ref, u_ref, us_ref, wb_ref, xn_ref):
    @pl.when(pl.program_id(1) == 0)
    def _():
        xn = _rms(x_ref[...], nw_ref[...]).astype(bf16)
        xn_ref[...] = xn
        us_ref[...] = _dot_nt(xn, ws_ref[...])

    wb_ref[...] = w_ref[...].astype(bf16)
    u_ref[...] = _dot_nt(xn_ref[...], wb_ref[...])


def _inproj(x, wts):
    M = x.shape[0]
    tm = _row_tile(M)
    const = lambda shape: pl.BlockSpec(shape, lambda i, j: (0,) * len(shape))
    out_shape = [jax.ShapeDtypeStruct((M, U_MAIN), f32), jax.ShapeDtypeStruct((M, U_SMALL), f32)]
    if "w_main" in wts:
        tn = INPROJ_TILE
        u, us = pl.pallas_call(
            _inproj_kernel,
            grid=(M // tm, U_MAIN // tn),
            in_specs=[pl.BlockSpec((tm, D_MODEL), lambda i, j: (i, 0)), const((1, D_MODEL)),
                      pl.BlockSpec((tn, D_MODEL), lambda i, j: (j, 0)), const((U_SMALL, D_MODEL))],
            out_specs=[pl.BlockSpec((tm, tn), lambda i, j: (i, j)),
                       pl.BlockSpec((tm, U_SMALL), lambda i, j: (i, 0))],
            out_shape=out_shape,
            scratch_shapes=[pltpu.VMEM((tm, D_MODEL), bf16)],
            compiler_params=_params(2),
            name="inproj",
        )(x, wts["norm1_w"], wts["w_main"], wts["w_small"])
        return u, us, {}
    assert M == tm, "the casting call must see every weight tile exactly once"
    tn = INPROJ_CAST_TILE
    bounds, dst = [], 0
    for name in ("z", "x", "v", "o", "q", "k", "bc"):
        src, width = wts["w_in_offs"][name]
        assert width % tn == 0 and dst % tn == 0
        bounds.append((dst // tn, src))
        dst += width
    unit = 2 * SUBLANES
    assert dst == U_MAIN and all(src % unit == 0 for _, src in bounds)

    def src_row(j):
        row = jnp.int32(0)
        for first_blk, src in bounds:
            row = jnp.where(j >= first_blk, src // unit + (j - first_blk) * (tn // unit), row)
        return row * unit

    u, us, w_main = pl.pallas_call(
        _inproj_cast_kernel,
        grid=(1, U_MAIN // tn),
        in_specs=[pl.BlockSpec((tm, D_MODEL), lambda i, j: (i, 0)), const((1, D_MODEL)),
                  pl.BlockSpec((pl.Element(tn), pl.Element(D_MODEL)), lambda i, j: (src_row(j), 0)),
                  const((U_SMALL, D_MODEL))],
        out_specs=[pl.BlockSpec((tm, tn), lambda i, j: (i, j)),
                   pl.BlockSpec((tm, U_SMALL), lambda i, j: (i, 0)),
                   pl.BlockSpec((tn, D_MODEL), lambda i, j: (j, 0))],
        out_shape=out_shape + [jax.ShapeDtypeStruct((U_MAIN, D_MODEL), bf16)],
        scratch_shapes=[pltpu.VMEM((tm, D_MODEL), bf16)],
        compiler_params=_params(2),
        name="inproj_cast",
    )(x, wts["norm1_w"], wts["w_in_t"], wts["w_small"])
    return u, us, dict(w_main=w_main)


def _ssd_kernel(z_ref, x_ref, bc_ref, sm_ref, hx_ref, hbc_ref, cwx_ref, cwbc_ref, cbx_ref, cbbc_ref,
                dtb_ref, alog_ref, dexp_ref, nw_ref, ehp_ref, s0_ref,
                y_ref, s_ref, tailx_ref, tailbc_ref, yz_ref, *, L, Ls, n_chunks):
    nseq = L // Ls
    c = pl.program_id(1)
    xpre = x_ref[...]
    bcpre = bc_ref[...]
    if nseq > 1:
        prev_x = [hx_ref[:, k:k + 1, :] for k in range(SSD_CONV - 1)]
        prev_bc = [hbc_ref[:, k:k + 1, :] for k in range(SSD_CONV - 1)]
        xc = _silu(_causal_conv_seqs(xpre, prev_x, cwx_ref[...], cbx_ref[...]))
        bcc = _silu(_causal_conv_seqs(bcpre, prev_bc, cwbc_ref[...], cbbc_ref[...]))
    else:
        if n_chunks == 1:
            hx, hbc = hx_ref[0], hbc_ref[0]
        else:
            first = c == 0
            hx = jnp.where(first, hx_ref[0], tailx_ref[...])
            hbc = jnp.where(first, hbc_ref[0], tailbc_ref[...])
        xc = _silu(_causal_conv(xpre, hx, cwx_ref[...], cbx_ref[...]))
        bcc = _silu(_causal_conv(bcpre, hbc, cwbc_ref[...], cbbc_ref[...]))
    if n_chunks > 1:
        tailx_ref[...] = xpre[L - SUBLANES:, :]
        tailbc_ref[...] = bcpre[L - SUBLANES:, :]

    lane = _iota2((L, LANES), 1)
    dt = jnp.where(lane < SSD_HEADS, _softplus(sm_ref[:, :LANES] + dtb_ref[...]), 0.0)
    dta = dt * (-jnp.exp(alog_ref[...]))

    causal, tri, tri_t, last = _seq_masks(L, Ls)
    eye = _eye()
    cum = _sel_left(tri, dta)
    cum_t = _sel_right(_transpose_exact(dta, eye), tri_t)
    cum_last = _sel_left(last, cum)
    ehp = ehp_ref[...]
    xdt = xc * _expand_heads(dt, ehp)
    xdtw = xdt * jnp.exp(_expand_heads(cum_last - cum, ehp))
    ecum = jnp.exp(_expand_heads(cum, ehp))

    if n_chunks > 1:
        @pl.when(c == 0)
        def _():
            s_ref[...] = s0_ref[...]
        sprev_ref = s_ref
    else:
        sprev_ref = s0_ref

    GE = SSD_D_INNER // SSD_GROUPS
    HPG = SSD_HEADS // SSD_GROUPS
    shift = Ls.bit_length() - 1
    seq_of_row = _iota2((L, 1), 0) >> shift
    lane_lo = lane < SSD_HEAD_DIM
    ys, intra = [], []
    for g in range(SSD_GROUPS):
        bm = bcc[:, g * SSD_STATE:(g + 1) * SSD_STATE]
        cm = bcc[:, (SSD_GROUPS + g) * SSD_STATE:(SSD_GROUPS + g + 1) * SSD_STATE]
        cb = _dot_nt(cm, bm)
        acc = None
        for j in range(nseq):
            cmj = cm if nseq == 1 else jnp.where(seq_of_row == j, cm, 0.0)
            t = _dot_nt(cmj, sprev_ref[j, g * GE:(g + 1) * GE, :])
            acc = t if acc is None else acc + t
        ys.append(acc)
        for hp in range(HPG // 2):
            col0 = g * GE + hp * LANES
            xpair = xdt[:, col0:col0 + LANES].astype(bf16)
            for e in range(2):
                h = g * HPG + hp * 2 + e
                seg = cum[:, h:h + 1] - cum_t[h:h + 1, :]
                m = jnp.exp(jnp.where(causal, seg, NEG)) * cb
                intra.append(jnp.dot(m.astype(bf16), xpair, preferred_element_type=f32))
    ssq = jnp.zeros((L, 1), f32)
    for g in range(SSD_GROUPS):
        for hp in range(HPG // 2):
            col0 = g * GE + hp * LANES
            blk = slice(col0, col0 + LANES)
            o0, o1 = intra[col0 // LANES * 2], intra[col0 // LANES * 2 + 1]
            y = (jnp.where(lane_lo, o0, o1)
                 + ecum[:, blk] * ys[g][:, hp * LANES:(hp + 1) * LANES]
                 + dexp_ref[:, blk] * xc[:, blk])
            yz = y * _silu(z_ref[:, blk])
            ssq = ssq + jnp.sum(yz * yz, axis=-1, keepdims=True)
            yz_ref[:, blk] = yz
    r = lax.rsqrt(ssq * (1.0 / SSD_D_INNER) + EPS)
    y_ref[...] = ((yz_ref[...] * r) * nw_ref[...]).astype(bf16)

    for j in range(nseq):
        tl = (j + 1) * Ls - 1
        dec = jnp.exp(jnp.broadcast_to(cum_t[:, tl:tl + 1], (LANES, LANES)))
        for g in range(SSD_GROUPS):
            bm = bcc[:, g * SSD_STATE:(g + 1) * SSD_STATE]
            xw = xdtw[:, g * GE:(g + 1) * GE]
            if nseq > 1:
                xw = jnp.where(seq_of_row == j, xw, 0.0)
            ds = _dot_tn(xw, bm)
            for e in range(HPG):
                h = g * HPG + e
                rs = slice(h * SSD_HEAD_DIM, (h + 1) * SSD_HEAD_DIM)
                s_ref[j, rs, :] = (dec[h:h + 1, :] * sprev_ref[j, rs, :]
                                   + ds[e * SSD_HEAD_DIM:(e + 1) * SSD_HEAD_DIM, :])


def _ssd(u, us, hx, hbc, s0, wts, *, B, T, L, Ls):
    nseq = L // Ls
    n_chunks = T // Ls if nseq == 1 else 1
    nblk = B // nseq
    rb = lambda i, c: i * n_chunks + c
    if nseq > 1:
        assert Ls == SUBLANES
        h_specs = [pl.BlockSpec((nseq, SSD_CONV - 1, SSD_D_INNER), lambda i, c: (i, 0, 0)),
                   pl.BlockSpec((nseq, SSD_CONV - 1, SSD_BC), lambda i, c: (i, 0, SSD_D_INNER // SSD_BC))]
    else:
        h_specs = [pl.BlockSpec((1, SUBLANES, SSD_D_INNER), lambda i, c: (i, 0, 0)),
                   pl.BlockSpec((1, SUBLANES, SSD_BC), lambda i, c: (i, 0, 0))]
    const = lambda shape: pl.BlockSpec(shape, lambda i, c: (0,) * len(shape))
    kern = functools.partial(_ssd_kernel, L=L, Ls=Ls, n_chunks=n_chunks)
    return pl.pallas_call(
        kern,
        grid=(nblk, n_chunks),
        in_specs=[pl.BlockSpec((L, SSD_D_INNER), lambda i, c: (rb(i, c), U_Z)),
                  pl.BlockSpec((L, SSD_D_INNER), lambda i, c: (rb(i, c), U_X)),
                  pl.BlockSpec((L, SSD_BC), lambda i, c: (rb(i, c), U_BC)),
                  pl.BlockSpec((L, U_SMALL), lambda i, c: (rb(i, c), 0)),
                  *h_specs,
                  const((SSD_CONV, SSD_D_INNER)), const((SSD_CONV, SSD_BC)),
                  const((1, SSD_D_INNER)), const((1, SSD_BC)),
                  const((1, LANES)), const((1, LANES)),
                  const((1, SSD_D_INNER)), const((1, SSD_D_INNER)),
                  const((2 * LANES, SSD_D_INNER)),
                  pl.BlockSpec((nseq, SSD_D_INNER, SSD_STATE), lambda i, c: (i, 0, 0))],
        out_specs=[pl.BlockSpec((L, SSD_D_INNER), lambda i, c: (rb(i, c), 0)),
                   pl.BlockSpec((nseq, SSD_D_INNER, SSD_STATE), lambda i, c: (i, 0, 0))],
        out_shape=[jax.ShapeDtypeStruct((B * T, SSD_D_INNER), bf16),
                   jax.ShapeDtypeStruct((B, SSD_D_INNER, SSD_STATE), f32)],
        scratch_shapes=[pltpu.VMEM((SUBLANES, SSD_D_INNER), f32),
                        pltpu.VMEM((SUBLANES, SSD_BC), f32),
                        pltpu.VMEM((L, SSD_D_INNER), f32)],
        compiler_params=_params(2),
        name="ssd",
    )(u, u, u, us, hx, hbc, wts["cw_x"], wts["cw_bc"], wts["cb_x"], wts["cb_bc"],
      wts["dt_bias"], wts["a_log"], wts["d_exp"], wts["ssd_norm_w"], wts["ehp"], s0)


def _interleave(gens):
    while gens:
        alive = []
        for g in gens:
            try:
                next(g)
                alive.append(g)
            except StopIteration:
                pass
        gens = alive


def _group_views(refs, g, nseq):
    return [r.at[g] if kind == "tok" else r.at[pl.ds(g * nseq, nseq)] if kind == "state" else r
            for r, kind in refs]


def _mlstm_kernel(*refs, L, Ls, n_chunks, groups):
    kinds = ["tok"] * 5 + ["const"] * 3 + ["state"] * 3 + ["tok"] + ["state"] * 3
    _interleave([_mlstm_chunk(*_group_views(list(zip(refs, kinds)), g, L // Ls), L=L, Ls=Ls, n_chunks=n_chunks)
                 for g in range(groups)])


def _mlstm_chunk(q_ref, k_ref, v_ref, o_ref, sm_ref, ib_ref, fb_ref, nw_ref, c0_ref, n0_ref, m0_ref,
                 y_ref, c_ref, n_ref, m_ref, *, L, Ls, n_chunks):
    nseq = L // Ls
    c = pl.program_id(1)
    if n_chunks > 1:
        @pl.when(c == 0)
        def _():
            c_ref[...] = c0_ref[...]
            n_ref[...] = n0_ref[...]
            m_ref[...] = m0_ref[...]
        cprev_ref, nprev_ref, mprev_ref = c_ref, n_ref, m_ref
    else:
        cprev_ref, nprev_ref, mprev_ref = c0_ref, n0_ref, m0_ref

    per_tok = lambda a: jnp.broadcast_to(a, (nseq, Ls, LANES)).reshape(L, LANES)
    ig = sm_ref[:, :LANES] + ib_ref[...]
    fraw = sm_ref[:, LANES:] + fb_ref[...]
    lf = -_softplus(-fraw)
    causal, tri, _, last = _seq_masks(L, Ls)
    eye = _eye()
    F = _sel_left(tri, lf)
    FL = _sel_left(last, F)
    mp = per_tok(mprev_ref[:, 0:1, :])
    r_t = _transpose_exact(ig - F, eye)
    inter = F + mp
    lw = FL - F + ig
    segmax = jnp.max(lw.reshape(nseq, Ls, LANES), axis=1, keepdims=True)
    m_new = jnp.maximum(FL + mp, per_tok(segmax))
    sc = jnp.exp(lw - m_new)
    dec = jnp.exp(FL + mp - m_new)
    m_out = m_new.reshape(nseq, Ls, LANES)[:, 0:SUBLANES, :]
    yield

    shift = Ls.bit_length() - 1
    seq_of_row = _iota2((L, 1), 0) >> shift
    kscale = ML_QK_DIM ** -0.5
    heads = range(ML_HEADS)
    qcols = lambda h: slice(h * ML_QK_DIM, (h + 1) * ML_QK_DIM)
    vcols = lambda h: slice(h * ML_V_DIM, (h + 1) * ML_V_DIM)
    q = [q_ref[:, qcols(h)] for h in heads]
    k = [k_ref[:, qcols(h)] * kscale for h in heads]
    v = [v_ref[:, vcols(h)].astype(bf16) for h in heads]
    qk = [_dot_nt(q[h], k[h]) for h in heads]
    yield
    qc = []
    for h in heads:
        acc = None
        for j in range(nseq):
            qj = q[h] if nseq == 1 else jnp.where(seq_of_row == j, q[h], 0.0)
            t = _dot(qj, cprev_ref[j, qcols(h), :])
            acc = t if acc is None else acc + t
        qc.append(acc)
    yield
    rep = lambda col: jnp.broadcast_to(col, (L, LANES))
    twice = lambda a: jnp.concatenate([a, a], axis=1)
    qn = []
    for h in heads:
        acc = None
        for j in range(nseq):
            qj = q[h] if nseq == 1 else jnp.where(seq_of_row == j, q[h], 0.0)
            t = _dot_nt(qj, jnp.broadcast_to(nprev_ref[j, h:h + 1, :], (LANES, ML_QK_DIM)))
            acc = t if acc is None else acc + t
        qn.append(acc)
    yield
    dm, m_rep, inter_rep = [], [], []
    for h in heads:
        gl = GATE_LANE + h
        d = jnp.where(causal, F[:, gl:gl + 1] + r_t[gl:gl + 1, :], NEG)
        dm.append(d)
        inter_rep.append(rep(inter[:, gl:gl + 1]))
        m_rep.append(jnp.maximum(rep(jnp.max(d, axis=-1, keepdims=True)), inter_rep[h]))
    yield
    w = [jnp.exp(dm[h] - (m_rep[h] if L == LANES else m_rep[h][:, 0:1])) * qk[h] for h in heads]
    yield
    wv = [jnp.dot(w[h].astype(bf16), v[h], preferred_element_type=f32) for h in heads]
    wsum = [_rowsum_mxu(w[h]) for h in heads]
    yield
    hh = []
    for h in heads:
        wi = jnp.exp(inter_rep[h] - m_rep[h])
        den = wsum[h] + wi * qn[h]
        inv = 1.0 / jnp.maximum(jnp.abs(den), jnp.exp(-m_rep[h]))
        hh.append((wv[h] + twice(wi) * qc[h]) * twice(inv))
    yield
    ssq = [_rowsum_mxu(hh[h] * hh[h]) for h in heads]
    yield
    for h in heads:
        r = lax.rsqrt(ssq[h] * (1.0 / ML_V_DIM) + EPS)
        hn = (hh[h] * twice(r)) * nw_ref[:, vcols(h)]
        y_ref[:, vcols(h)] = (_sigmoid(o_ref[:, vcols(h)]) * hn).astype(bf16)
        if h % 2 == 1:
            yield
    for h in heads:
        gl = GATE_LANE + h
        ksc = k[h] * sc[:, gl:gl + 1]
        for j in range(nseq):
            r0 = j * Ls
            dj = dec[r0:r0 + 1, gl:gl + 1]
            kj = ksc if nseq == 1 else jnp.where(seq_of_row == j, ksc, 0.0)
            c_ref[j, qcols(h), :] = dj * cprev_ref[j, qcols(h), :] + _dot_tn(kj, v[h])
            n_ref[j, h:h + 1, :] = (dj * nprev_ref[j, h:h + 1, :]
                                    + jnp.sum(ksc[r0:r0 + Ls, :], axis=0, keepdims=True))
        if h % 2 == 1:
            yield
    m_ref[...] = m_out


def _scan_groups(nblk, nseq):
    return SCAN_GROUPS if nseq == 1 and nblk % SCAN_GROUPS == 0 else 1


def _mlstm(u, us, c0, n0, m0, wts, *, B, T, L, Ls):
    nseq = L // Ls
    n_chunks = T // Ls if nseq == 1 else 1
    nblk = B // nseq
    G = _scan_groups(nblk, nseq)
    rows = n_chunks * L
    u3, us3 = u.reshape(nblk, rows, U_MAIN), us.reshape(nblk, rows, U_SMALL)
    tok = lambda width, col: pl.BlockSpec((G, L, width), lambda i, c: (i, c, col))
    const = lambda shape: pl.BlockSpec(shape, lambda i, c: (0,) * len(shape))
    st = lambda shape: pl.BlockSpec((G * nseq,) + shape, lambda i, c: (i, 0, 0))
    kern = functools.partial(_mlstm_kernel, L=L, Ls=Ls, n_chunks=n_chunks, groups=G)
    y, c_new, n_new, m_new = pl.pallas_call(
        kern,
        grid=(nblk // G, n_chunks),
        in_specs=[tok(ML_QK_INNER, U_Q), tok(ML_QK_INNER, U_K), tok(ML_D_INNER, U_V), tok(ML_D_INNER, U_O),
                  tok(U_SMALL, 0),
                  const((1, LANES)), const((1, LANES)), const((1, ML_D_INNER)),
                  st((ML_QK_INNER, ML_V_DIM)), st((ML_HEADS, ML_QK_DIM)), st((SUBLANES, LANES))],
        out_specs=[tok(ML_D_INNER, 0),
                   st((ML_QK_INNER, ML_V_DIM)), st((ML_HEADS, ML_QK_DIM)), st((SUBLANES, LANES))],
        out_shape=[jax.ShapeDtypeStruct((nblk, rows, ML_D_INNER), bf16),
                   jax.ShapeDtypeStruct((B, ML_QK_INNER, ML_V_DIM), f32),
                   jax.ShapeDtypeStruct((B, ML_HEADS, ML_QK_DIM), f32),
                   jax.ShapeDtypeStruct((B, SUBLANES, LANES), f32)],
        compiler_params=_params(2),
        name="mlstm",
    )(u3, u3, u3, u3, us3, wts["i_bias"], wts["f_bias"], wts["ml_norm_w"], c0, n0, m0)
    return y.reshape(B * T, ML_D_INNER), c_new, n_new, m_new


def _outproj_cast_kernel(ys_ref, ym_ref, ws_ref, wm_ref, h_ref, o_ref, wsb_ref, wmb_ref):
    @pl.when(pl.program_id(1) == 0)
    def _():
        wsb_ref[...] = ws_ref[...].astype(bf16)
        wmb_ref[...] = wm_ref[...].astype(bf16)

    o_ref[...] = (h_ref[...]
                  + jnp.dot(ys_ref[...], wsb_ref[...], preferred_element_type=f32)
                  + jnp.dot(ym_ref[...], wmb_ref[...], preferred_element_type=f32))


def _outproj_kernel(ys_ref, ym_ref, ws_ref, wm_ref, h_ref, o_ref):
    o_ref[...] = (h_ref[...]
                  + jnp.dot(ys_ref[...], ws_ref[...], preferred_element_type=f32)
                  + jnp.dot(ym_ref[...], wm_ref[...], preferred_element_type=f32))


def _outproj(ys, ym, wts, h):
    M = h.shape[0]
    tm = _row_tile(M)
    if "w_out_s" in wts:
        tn = OUTPROJ_TILE
        h1 = pl.pallas_call(
            _outproj_kernel,
            grid=(M // tm, D_MODEL // tn),
            in_specs=[pl.BlockSpec((tm, SSD_D_INNER), lambda i, j: (i, 0)),
                      pl.BlockSpec((tm, ML_D_INNER), lambda i, j: (i, 0)),
                      pl.BlockSpec((SSD_D_INNER, tn), lambda i, j: (0, j)),
                      pl.BlockSpec((ML_D_INNER, tn), lambda i, j: (0, j)),
                      pl.BlockSpec((tm, tn), lambda i, j: (i, j))],
            out_specs=pl.BlockSpec((tm, tn), lambda i, j: (i, j)),
            out_shape=jax.ShapeDtypeStruct((M, D_MODEL), f32),
            compiler_params=_params(2),
            name="outproj",
        )(ys, ym, wts["w_out_s"], wts["w_out_m"], h)
        return h1, {}
    tn = OUTPROJ_CAST_TILE
    wspec = pl.BlockSpec((SSD_D_INNER, tn), lambda j, i: (0, j))
    wshape = jax.ShapeDtypeStruct((SSD_D_INNER, D_MODEL), bf16)
    h1, ws, wm = pl.pallas_call(
        _outproj_cast_kernel,
        grid=(D_MODEL // tn, M // tm),
        in_specs=[pl.BlockSpec((tm, SSD_D_INNER), lambda j, i: (i, 0)),
                  pl.BlockSpec((tm, ML_D_INNER), lambda j, i: (i, 0)),
                  wspec,
                  pl.BlockSpec((ML_D_INNER, tn), lambda j, i: (1, j)),
                  pl.BlockSpec((tm, tn), lambda j, i: (i, j))],
        out_specs=[pl.BlockSpec((tm, tn), lambda j, i: (i, j)), wspec, wspec],
        out_shape=[jax.ShapeDtypeStruct((M, D_MODEL), f32), wshape, wshape],
        compiler_params=_params(2),
        name="outproj_cast",
    )(ys, ym, wts["w_out"], wts["w_out"], h)
    return h1, dict(w_out_s=ws, w_out_m=wm)


def _ffn_kernel(h_ref, nw_ref, wg_ref, wv_ref, cwg_ref, cwv_ref, cbg_ref, cbv_ref, wd_ref, fw_ref, *rest,
                tm, tf, multi, blocks_per_seq, cast):
    if multi:
        hg_ref, hv_ref, y_ref, tg_ref, tv_ref, xn_ref = rest
    elif cast:
        hg_ref, hv_ref, y_ref, tg_ref, tv_ref, wgb_ref, wvb_ref, wdb_ref, xn_ref, carg_ref, carv_ref = rest
    else:
        hg_ref, hv_ref, y_ref, tg_ref, tv_ref, xn_ref, carg_ref, carv_ref = rest
    i = pl.program_id(0)
    j = pl.program_id(1)

    @pl.when(j == 0)
    def _():
        xn_ref[...] = _rms(h_ref[...], nw_ref[...]).astype(bf16)
        y_ref[...] = jnp.zeros_like(y_ref)

    if cast:
        wgb_ref[...] = wg_ref[...].astype(bf16)
        wvb_ref[...] = wv_ref[...].astype(bf16)
        wdb_ref[...] = wd_ref[...].astype(bf16)
        wg_ref, wv_ref, wd_ref = wgb_ref, wvb_ref, wdb_ref

    xn = xn_ref[...]
    subs = [slice(c0, c0 + MXU_COLS) for c0 in range(0, tf, MXU_COLS)]
    up_dots = lambda cs: [jnp.dot(xn, w_ref[:, cs], preferred_element_type=f32) for w_ref in (wg_ref, wv_ref)]
    ups_next = up_dots(subs[0])
    for n, cs in enumerate(subs):
        ups = ups_next
        if n + 1 < len(subs):
            ups_next = up_dots(subs[n + 1])
        convd = []
        for half, (up, cw_ref, cb_ref) in enumerate(zip(ups, (cwg_ref, cwv_ref), (cbg_ref, cbv_ref))):
            if multi:
                s_ref, t_ref = ((hg_ref, tg_ref), (hv_ref, tv_ref))[half]
                prev = [s_ref[:, k:k + 1, cs] for k in range(FFN_CONV - 1)]
                convd.append(_causal_conv_seqs(up, prev, cw_ref[:, cs], cb_ref[:, cs]))
                up3 = up.reshape(tm // SUBLANES, SUBLANES, MXU_COLS)
                t_ref[:, :, cs] = up3[:, SUBLANES - (FFN_CONV - 1):, :]
            else:
                h_ref_, car_ref, t_ref = ((hg_ref, carg_ref, tg_ref), (hv_ref, carv_ref, tv_ref))[half]
                tail = up[tm - SUBLANES:, :]
                if blocks_per_seq == 1:
                    hist = h_ref_[0, :, cs]
                else:
                    hist = jnp.where((i % blocks_per_seq) == 0, h_ref_[0, :, cs], car_ref[j, :, cs])
                    car_ref[j, :, cs] = tail
                t_ref[0, :, cs] = tail
                convd.append(_causal_conv(up, hist, cw_ref[:, cs], cb_ref[:, cs]))
        act = (_silu(convd[0]) * convd[1]).astype(bf16)
        y_ref[...] += jnp.dot(act, wd_ref[cs, :], preferred_element_type=f32)

    @pl.when(j == pl.num_programs(1) - 1)
    def _():
        y_ref[...] = _rms(h_ref[...] + y_ref[...], fw_ref[...])


def _ffn(h, ffn0, wts, *, B, T):
    M = h.shape[0]
    tm = _row_tile(M)
    tf = FFN_TILE
    n_ff = D_FF // tf
    multi = T < tm
    const = lambda shape: pl.BlockSpec(shape, lambda i, j: (0,) * len(shape))
    if multi:
        assert T == SUBLANES
        blocks_per_seq = 1
        nseq = tm // T
        hist = [ffn0, ffn0]
        h_specs = [pl.BlockSpec((nseq, FFN_CONV - 1, tf), lambda i, j: (i, 0, j)),
                   pl.BlockSpec((nseq, FFN_CONV - 1, tf), lambda i, j: (i, 0, n_ff + j))]
        t_specs = [pl.BlockSpec((nseq, FFN_CONV - 1, tf), lambda i, j: (i, 0, j))] * 2
        t_shapes = [jax.ShapeDtypeStruct((B, FFN_CONV - 1, D_FF), f32)] * 2
        scratch = []
    else:
        blocks_per_seq = T // tm
        pad = jnp.pad(ffn0, ((0, 0), (SUBLANES - (FFN_CONV - 1), 0), (0, 0)))
        hist = [pad, pad]
        h_specs = [pl.BlockSpec((1, SUBLANES, tf), lambda i, j: (i // blocks_per_seq, 0, j)),
                   pl.BlockSpec((1, SUBLANES, tf), lambda i, j: (i // blocks_per_seq, 0, n_ff + j))]
        t_specs = [pl.BlockSpec((1, SUBLANES, tf), lambda i, j: (i, 0, j))] * 2
        t_shapes = [jax.ShapeDtypeStruct((M // tm, SUBLANES, D_FF), f32)] * 2
        scratch = [pltpu.VMEM((n_ff, SUBLANES, tf), f32)] * 2
    cast = "w_up_g" not in wts
    assert not (cast and (multi or M != tm)), "the casting call must see every weight tile exactly once"
    up_spec = pl.BlockSpec((D_MODEL, tf), lambda i, j: (0, j))
    down_spec = pl.BlockSpec((tf, D_MODEL), lambda i, j: (j, 0))
    if cast:
        w_specs = [up_spec, pl.BlockSpec((D_MODEL, tf), lambda i, j: (0, n_ff + j)), down_spec]
        w_args = [wts["w_up"], wts["w_up"], wts["w_down"]]
        wb_specs = [up_spec, up_spec, down_spec]
        wb_shapes = [jax.ShapeDtypeStruct((D_MODEL, D_FF), bf16)] * 2 + [jax.ShapeDtypeStruct((D_FF, D_MODEL), bf16)]
    else:
        w_specs = [up_spec, up_spec, down_spec]
        w_args = [wts["w_up_g"], wts["w_up_v"], wts["w_down_b"]]
        wb_specs, wb_shapes = [], []
    kern = functools.partial(_ffn_kernel, tm=tm, tf=tf, multi=multi, blocks_per_seq=blocks_per_seq, cast=cast)
    y, *outs = pl.pallas_call(
        kern,
        grid=(M // tm, n_ff),
        in_specs=[pl.BlockSpec((tm, D_MODEL), lambda i, j: (i, 0), pipeline_mode=pl.Buffered(1)),
                  const((1, D_MODEL)),
                  w_specs[0], w_specs[1],
                  pl.BlockSpec((FFN_CONV, tf), lambda i, j: (0, j)),
                  pl.BlockSpec((FFN_CONV, tf), lambda i, j: (0, n_ff + j)),
                  pl.BlockSpec((1, tf), lambda i, j: (0, j)),
                  pl.BlockSpec((1, tf), lambda i, j: (0, n_ff + j)),
                  w_specs[2],
                  const((1, D_MODEL)),
                  *h_specs],
        out_specs=[pl.BlockSpec((tm, D_MODEL), lambda i, j: (i, 0)), *t_specs, *wb_specs],
        out_shape=[jax.ShapeDtypeStruct((M, D_MODEL), f32), *t_shapes, *wb_shapes],
        scratch_shapes=[pltpu.VMEM((tm, D_MODEL), bf16), *scratch],
        compiler_params=_params(2),
        name="ffn_cast" if cast else "ffn",
    )(h, wts["norm2_w"], w_args[0], w_args[1], wts["ffn_cw"], wts["ffn_cw"],
      wts["ffn_cb"], wts["ffn_cb"], w_args[2], wts["final_norm_w"], *hist)
    tails, new_w = outs[:2], {}
    if cast:
        new_w = dict(w_up_g=outs[2], w_up_v=outs[3], w_down_b=outs[4])
    if multi:
        return y, jnp.concatenate(tails, axis=-1), new_w
    last = jnp.concatenate(tails, axis=-1).reshape(B, blocks_per_seq, SUBLANES, 2 * D_FF)
    return y, last[:, blocks_per_seq - 1, SUBLANES - (FFN_CONV - 1):, :], new_w


def _hist_tile(state):
    return jnp.pad(state, ((0, 0), (SUBLANES - state.shape[1], 0), (0, 0)))


def _mixer(h, states, wts, *, B, T, L, Ls):
    conv0, s0, c0, n0, m0 = states
    u, us, cast_w = _inproj(h, wts)
    wts = {**wts, **cast_w}
    if L // Ls > 1:
        hx = hbc = conv0
    else:
        hx, hbc = _hist_tile(conv0[:, :, :SSD_D_INNER]), _hist_tile(conv0[:, :, SSD_D_INNER:])
    y_ssd, s_new = _ssd(u, us, hx, hbc, s0.reshape(B, SSD_D_INNER, SSD_STATE), wts, B=B, T=T, L=L, Ls=Ls)
    m_pad = jnp.broadcast_to(
        jnp.pad(m0, ((0, 0), (GATE_LANE, LANES - GATE_LANE - ML_HEADS)))[:, None, :], (B, SUBLANES, LANES))
    y_ml, c_new, n_new, m_new = _mlstm(u, us, c0.reshape(B, ML_QK_INNER, ML_V_DIM), n0, m_pad, wts,
                                       B=B, T=T, L=L, Ls=Ls)
    h1, cast_w = _outproj(y_ssd, y_ml, wts, h)
    wts = {**wts, **cast_w}
    ur = u.reshape(B, T, U_MAIN)[:, T - (SSD_CONV - 1):, :]
    conv_new = jnp.concatenate([ur[:, :, U_X * SSD_D_INNER:(U_X + 1) * SSD_D_INNER],
                                ur[:, :, U_BC * SSD_BC:(U_BC + 1) * SSD_BC]], axis=-1)
    new_states = (conv_new,
                  s_new.reshape(B, SSD_HEADS, SSD_HEAD_DIM, SSD_STATE),
                  c_new.reshape(B, ML_HEADS, ML_QK_DIM, ML_V_DIM),
                  n_new,
                  m_new[:, 0, GATE_LANE:GATE_LANE + ML_HEADS])
    return h1, new_states, wts


def _prep_weights(norm1_w, w_in, ssd_conv_w, ssd_conv_b, ssd_dt_bias, ssd_A_log, ssd_D, ssd_norm_w,
                  ml_i_bias, ml_f_bias, ml_norm_w, w_out, norm2_w, w_up, ffn_conv_w, ffn_conv_b, w_down,
                  final_norm_w):
    w_t = w_in.T
    o = 0
    rows, offs = {}, {}
    for name, width in (("z", SSD_D_INNER), ("x", SSD_D_INNER), ("bc", SSD_BC), ("dt", SSD_HEADS),
                        ("q", ML_QK_INNER), ("k", ML_QK_INNER), ("v", ML_D_INNER), ("i", ML_HEADS),
                        ("f", ML_HEADS), ("o", ML_D_INNER)):
        offs[name] = (o, width)
        if name in ("dt", "i", "f"):
            rows[name] = lax.optimization_barrier(w_t[o:o + width, :]).astype(bf16)
        o += width
    zpad = lambda n: jnp.zeros((n, D_MODEL), bf16)
    w_small = jnp.concatenate([rows["dt"], rows["i"], zpad(LANES - GATE_LANE - ML_HEADS),
                               zpad(GATE_LANE), rows["f"], zpad(LANES - GATE_LANE - ML_HEADS)], axis=0)
    lane_row = lambda v, off: jnp.pad(v.astype(f32), (off, LANES - off - v.shape[0]))[None, :]
    hp = jnp.arange(SSD_D_INNER) // SSD_HEAD_DIM
    ehp = (jnp.arange(LANES)[:, None] == hp[None, :]).astype(bf16)
    return dict(
        norm1_w=norm1_w[None, :], w_in_t=w_t, w_in_offs=offs, w_small=w_small,
        cw_x=ssd_conv_w[:, :SSD_D_INNER], cw_bc=ssd_conv_w[:, SSD_D_INNER:],
        cb_x=ssd_conv_b[None, :SSD_D_INNER], cb_bc=ssd_conv_b[None, SSD_D_INNER:],
        dt_bias=lane_row(ssd_dt_bias, 0), a_log=lane_row(ssd_A_log, 0),
        d_exp=jnp.repeat(ssd_D.astype(f32), SSD_HEAD_DIM)[None, :], ssd_norm_w=ssd_norm_w[None, :],
        ehp=jnp.concatenate([ehp, ehp], axis=0),
        i_bias=lane_row(ml_i_bias, GATE_LANE), f_bias=lane_row(ml_f_bias, GATE_LANE),
        ml_norm_w=ml_norm_w[None, :],
        w_out=w_out, norm2_w=norm2_w[None, :], w_up=w_up,
        ffn_cw=ffn_conv_w, ffn_cb=ffn_conv_b[None, :], w_down=w_down,
        final_norm_w=final_norm_w[None, :])


def kernel(x_prompt, x_sample, state_ssd_conv, state_ssd, state_mlstm_C, state_mlstm_n, state_mlstm_m,
           state_ffn_conv, meta_tokens, norm1_w, w_in, ssd_conv_w, ssd_conv_b, ssd_dt_bias, ssd_A_log,
           ssd_D, ssd_norm_w, ml_i_bias, ml_f_bias, ml_norm_w, w_out, norm2_w, w_up, ffn_conv_w,
           ffn_conv_b, w_down, final_norm_w):
    depth = w_in.shape[0]
    assert depth == 1, "single-layer step"
    Bp, Tp, _ = x_prompt.shape
    Bs, Ts, _ = x_sample.shape
    wts = _prep_weights(norm1_w[0], w_in[0], ssd_conv_w[0], ssd_conv_b[0], ssd_dt_bias[0], ssd_A_log[0],
                        ssd_D[0], ssd_norm_w[0], ml_i_bias[0], ml_f_bias[0], ml_norm_w[0], w_out[0],
                        norm2_w[0], w_up[0], ffn_conv_w[0], ffn_conv_b[0], w_down[0], final_norm_w)
    s_init = (state_ssd_conv[0], state_ssd[0], state_mlstm_C[0], state_mlstm_n[0], state_mlstm_m[0])
    hs, s_new, wts = _mixer(x_sample.reshape(Bs * Ts, D_MODEL), s_init, wts, B=Bs, T=Ts, L=8 * Ts, Ls=Ts)
    zero_states = (jnp.zeros((1, SSD_CONV - 1, SSD_CONV_DIM), f32),
                   jnp.zeros((1, SSD_HEADS, SSD_HEAD_DIM, SSD_STATE), f32),
                   jnp.zeros((1, ML_HEADS, ML_QK_DIM, ML_V_DIM), f32),
                   jnp.zeros((1, ML_HEADS, ML_QK_DIM), f32),
                   jnp.zeros((1, ML_HEADS), f32))
    hm, m_new, wts = _mixer(meta_tokens.astype(f32), zero_states, wts, B=1, T=N_META, L=N_META, Ls=N_META)
    _, m_ffn, cast_w = _ffn(hm, jnp.zeros((1, FFN_CONV - 1, 2 * D_FF), f32), wts, B=1, T=N_META)
    wts = {**wts, **cast_w}
    ys, s_ffn, _ = _ffn(hs, state_ffn_conv[0], wts, B=Bs, T=Ts)
    p_init = tuple(jnp.broadcast_to(s, (Bp,) + s.shape[1:]) for s in (*m_new, m_ffn))
    hp, p_new, wts = _mixer(x_prompt.reshape(Bp * Tp, D_MODEL), p_init[:5], wts, B=Bp, T=Tp, L=128, Ls=128)
    yp, p_ffn, _ = _ffn(hp, p_init[5], wts, B=Bp, T=Tp)
    return (yp.reshape(Bp, Tp, D_MODEL), ys.reshape(Bs, Ts, D_MODEL),
            *(s[None] for s in (*p_new, p_ffn)), *(s[None] for s in (*s_new, s_ffn)))
```

```python
import functools

import jax
import jax.numpy as jnp
from jax import lax
from jax.experimental import pallas as pl
from jax.experimental.pallas import tpu as pltpu

f32 = jnp.float32
bf16 = jnp.bfloat16

D_MODEL = 2048
N_META = 16
SSD_HEADS = 32
SSD_HEAD_DIM = 64
SSD_D_INNER = SSD_HEADS * SSD_HEAD_DIM
SSD_GROUPS = 2
SSD_STATE = 128
SSD_CONV = 4
SSD_BC = 2 * SSD_GROUPS * SSD_STATE
SSD_CONV_DIM = SSD_D_INNER + SSD_BC
ML_HEADS = 8
ML_QK_DIM = 128
ML_V_DIM = 256
ML_QK_INNER = ML_HEADS * ML_QK_DIM
ML_D_INNER = ML_HEADS * ML_V_DIM
D_FF = 5632
FFN_CONV = 3
EPS = 1e-6
NEG = -1e30

LANES = 128
SUBLANES = 8
VMEM_LIMIT = 56 * 1024 * 1024
MXU_COLS = 256
ROW_TILE = 1024
FFN_TILE = 512
INPROJ_TILE = 1536
OUTPROJ_TILE = 1024
OUTPROJ_CAST_TILE = 512
INPROJ_CAST_TILE = 512
SCAN_GROUPS = 2


def _row_tile(M):
    return ROW_TILE if M % ROW_TILE == 0 else M


U_Z, U_X, U_V, U_O = 0, 1, 2, 3
U_Q, U_K = 8, 9
U_BC = 20
U_MAIN = 4 * 2048 + 2 * 1024 + 512
GATE_LANE = 32
U_SMALL = 2 * LANES


def _dot(a, b):
    return jnp.dot(a.astype(bf16), b.astype(bf16), preferred_element_type=f32)


def _dot_nt(a, b):
    return lax.dot_general(a.astype(bf16), b.astype(bf16), (((1,), (1,)), ((), ())),
                           preferred_element_type=f32)


def _dot_tn(a, b):
    return lax.dot_general(a.astype(bf16), b.astype(bf16), (((0,), (0,)), ((), ())),
                           preferred_element_type=f32)


def _split3(a):
    hi = a.astype(bf16)
    r1 = a - hi.astype(f32)
    mid = r1.astype(bf16)
    lo = (r1 - mid.astype(f32)).astype(bf16)
    return hi, mid, lo


def _sel_right(a, e01):
    return jnp.dot(jnp.concatenate(_split3(a), axis=1), jnp.concatenate([e01] * 3, axis=0),
                   preferred_element_type=f32)


def _sel_left(e01, a):
    return jnp.dot(jnp.concatenate([e01] * 3, axis=1), jnp.concatenate(_split3(a), axis=0),
                   preferred_element_type=f32)


def _split2(a):
    hi = a.astype(bf16)
    mid = (a - hi.astype(f32)).astype(bf16)
    return jnp.concatenate([hi, mid], axis=1)


def _expand_heads(a, e01x2):
    return jnp.dot(_split2(a), e01x2, preferred_element_type=f32)


def _rowsum_mxu(a):
    return jnp.dot(_split2(a), jnp.ones((2 * a.shape[1], LANES), bf16), preferred_element_type=f32)


def _transpose_exact(a, eye):
    return lax.dot_general(jnp.concatenate([eye] * 3, axis=1), jnp.concatenate(_split3(a), axis=1),
                           (((1,), (1,)), ((), ())), preferred_element_type=f32)


def _iota2(shape, axis):
    return lax.broadcasted_iota(jnp.int32, shape, axis)


def _as01(m):
    return jnp.where(m, 1.0, 0.0).astype(bf16)


def _eye():
    return _as01(_iota2((LANES, LANES), 0) == _iota2((LANES, LANES), 1))


def _seq_masks(L, Ls):
    t = _iota2((L, L), 0)
    s = _iota2((L, L), 1)
    shift = Ls.bit_length() - 1
    same = (t >> shift) == (s >> shift)
    causal = same & (s <= t)
    causal_t = same & (t <= s)
    last = s == (t | (Ls - 1))
    return causal, _as01(causal), _as01(causal_t), _as01(last)


def _sigmoid(x):
    return 1.0 / (1.0 + jnp.exp(-x))


def _silu(x):
    return x * _sigmoid(x)


def _softplus(x):
    return jnp.maximum(x, 0.0) + jnp.log(1.0 + jnp.exp(-jnp.abs(x)))


def _rms(x, w):
    r = lax.rsqrt(jnp.mean(x * x, axis=-1, keepdims=True) + EPS)
    return (x * r) * w


def _causal_conv(x, hist, w, b):
    L, C = x.shape
    K = w.shape[0]
    r = _iota2((SUBLANES, C), 0)
    y = b + x * w[K - 1:K, :]
    for s in range(1, K):
        zt = jnp.where(r >= SUBLANES - s, hist, x[L - SUBLANES:, :])
        z = zt if L == SUBLANES else jnp.concatenate([x[:L - SUBLANES, :], zt], axis=0)
        y = y + pltpu.roll(z, s, 0) * w[K - 1 - s:K - s, :]
    return y


def _causal_conv_seqs(x, prev, w, b):
    L, C = x.shape
    K = w.shape[0]
    nseq = L // SUBLANES
    r = _iota2((L, C), 0) & (SUBLANES - 1)
    per_row = lambda a: jnp.broadcast_to(a, (nseq, SUBLANES, C)).reshape(L, C)
    y = b + x * w[K - 1:K, :]
    for s in range(1, K):
        head = per_row(prev[K - 1 - s])
        for rr in range(1, s):
            head = jnp.where(r == rr, per_row(prev[K - 1 - s + rr]), head)
        y = y + jnp.where(r >= s, pltpu.roll(x, s, 0), head) * w[K - 1 - s:K - s, :]
    return y


def _params(n_axes):
    return pltpu.CompilerParams(dimension_semantics=("arbitrary",) * n_axes,
                                vmem_limit_bytes=VMEM_LIMIT)


def _inproj_kernel(x_ref, nw_ref, w_ref, ws_ref, u_ref, us_ref, xn_ref):
    @pl.when(pl.program_id(1) == 0)
    def _():
        xn = _rms(x_ref[...], nw_ref[...]).astype(bf16)
        xn_ref[...] = xn
        us_ref[...] = _dot_nt(xn, ws_ref[...])
        u_ref[...] = _dot_nt(xn, w_ref[...])

    @pl.when(pl.program_id(1) > 0)
    def _():
        u_ref[...] = _dot_nt(xn_ref[...], w_ref[...])


def _inproj_cast_kernel(x_ref, nw_ref, w_ref, ws_ref, u_ref, us_ref, wb_ref, xn_ref):
    @pl.when(pl.program_id(1) == 0)
    def _():
        xn = _rms(x_ref[...], nw_ref[...]).astype(bf16)
        xn_ref[...] = xn
        us_ref[...] = _dot_nt(xn, ws_ref[...])

    wb_ref[...] = w_ref[...].astype(bf16)
    u_ref[...] = _dot_nt(xn_ref[...], wb_ref[...])


def _inproj(x, wts):
    M = x.shape[0]
    tm = _row_tile(M)
    const = lambda shape: pl.BlockSpec(shape, lambda i, j: (0,) * len(shape))
    out_shape = [jax.ShapeDtypeStruct((M, U_MAIN), f32), jax.ShapeDtypeStruct((M, U_SMALL), f32)]
    if "w_main" in wts:
        tn = INPROJ_TILE
        u, us = pl.pallas_call(
            _inproj_kernel,
            grid=(M // tm, U_MAIN // tn),
            in_specs=[pl.BlockSpec((tm, D_MODEL), lambda i, j: (i, 0)), const((1, D_MODEL)),
                      pl.BlockSpec((tn, D_MODEL), lambda i, j: (j, 0)), const((U_SMALL, D_MODEL))],
            out_specs=[pl.BlockSpec((tm, tn), lambda i, j: (i, j)),
                       pl.BlockSpec((tm, U_SMALL), lambda i, j: (i, 0))],
            out_shape=out_shape,
            scratch_shapes=[pltpu.VMEM((tm, D_MODEL), bf16)],
            compiler_params=_params(2),
            name="inproj",
        )(x, wts["norm1_w"], wts["w_main"], wts["w_small"])
        return u, us, {}
    assert M == tm, "the casting call must see every weight tile exactly once"
    tn = INPROJ_CAST_TILE
    bounds, dst = [], 0
    for name in ("z", "x", "v", "o", "q", "k", "bc"):
        src, width = wts["w_in_offs"][name]
        assert width % tn == 0 and dst % tn == 0
        bounds.append((dst // tn, src))
        dst += width
    unit = 2 * SUBLANES
    assert dst == U_MAIN and all(src % unit == 0 for _, src in bounds)

    def src_row(j):
        row = jnp.int32(0)
        for first_blk, src in bounds:
            row = jnp.where(j >= first_blk, src // unit + (j - first_blk) * (tn // unit), row)
        return row * unit

    u, us, w_main = pl.pallas_call(
        _inproj_cast_kernel,
        grid=(1, U_MAIN // tn),
        in_specs=[pl.BlockSpec((tm, D_MODEL), lambda i, j: (i, 0)), const((1, D_MODEL)),
                  pl.BlockSpec((pl.Element(tn), pl.Element(D_MODEL)), lambda i, j: (src_row(j), 0)),
                  const((U_SMALL, D_MODEL))],
        out_specs=[pl.BlockSpec((tm, tn), lambda i, j: (i, j)),
                   pl.BlockSpec((tm, U_SMALL), lambda i, j: (i, 0)),
                   pl.BlockSpec((tn, D_MODEL), lambda i, j: (j, 0))],
        out_shape=out_shape + [jax.ShapeDtypeStruct((U_MAIN, D_MODEL), bf16)],
        scratch_shapes=[pltpu.VMEM((tm, D_MODEL), bf16)],
        compiler_params=_params(2),
        name="inproj_cast",
    )(x, wts["norm1_w"], wts["w_in_t"], wts["w_small"])
    return u, us, dict(w_main=w_main)


def _ssd_kernel(z_ref, x_ref, bc_ref, sm_ref, hx_ref, hbc_ref, cwx_ref, cwbc_ref, cbx_ref, cbbc_ref,
                dtb_ref, alog_ref, dexp_ref, nw_ref, ehp_ref, s0_ref,
                y_ref, s_ref, tailx_ref, tailbc_ref, yz_ref, *, L, Ls, n_chunks):
    nseq = L // Ls
    c = pl.program_id(1)
    xpre = x_ref[...]
    bcpre = bc_ref[...]
    if nseq > 1:
        prev_x = [hx_ref[:, k:k + 1, :] for k in range(SSD_CONV - 1)]
        prev_bc = [hbc_ref[:, k:k + 1, :] for k in range(SSD_CONV - 1)]
        xc = _silu(_causal_conv_seqs(xpre, prev_x, cwx_ref[...], cbx_ref[...]))
        bcc = _silu(_causal_conv_seqs(bcpre, prev_bc, cwbc_ref[...], cbbc_ref[...]))
    else:
        if n_chunks == 1:
            hx, hbc = hx_ref[0], hbc_ref[0]
        else:
            first = c == 0
            hx = jnp.where(first, hx_ref[0], tailx_ref[...])
            hbc = jnp.where(first, hbc_ref[0], tailbc_ref[...])
        xc = _silu(_causal_conv(xpre, hx, cwx_ref[...], cbx_ref[...]))
        bcc = _silu(_causal_conv(bcpre, hbc, cwbc_ref[...], cbbc_ref[...]))
    if n_chunks > 1:
        tailx_ref[...] = xpre[L - SUBLANES:, :]
        tailbc_ref[...] = bcpre[L - SUBLANES:, :]

    lane = _iota2((L, LANES), 1)
    dt = jnp.where(lane < SSD_HEADS, _softplus(sm_ref[:, :LANES] + dtb_ref[...]), 0.0)
    dta = dt * (-jnp.exp(alog_ref[...]))

    causal, tri, tri_t, last = _seq_masks(L, Ls)
    eye = _eye()
    cum = _sel_left(tri, dta)
    cum_t = _sel_right(_transpose_exact(dta, eye), tri_t)
    cum_last = _sel_left(last, cum)
    ehp = ehp_ref[...]
    xdt = xc * _expand_heads(dt, ehp)
    xdtw = xdt * jnp.exp(_expand_heads(cum_last - cum, ehp))
    ecum = jnp.exp(_expand_heads(cum, ehp))

    if n_chunks > 1:
        @pl.when(c == 0)
        def _():
            s_ref[...] = s0_ref[...]
        sprev_ref = s_ref
    else:
        sprev_ref = s0_ref

    GE = SSD_D_INNER // SSD_GROUPS
    HPG = SSD_HEADS // SSD_GROUPS
    shift = Ls.bit_length() - 1
    seq_of_row = _iota2((L, 1), 0) >> shift
    lane_lo = lane < SSD_HEAD_DIM
    ys, intra = [], []
    for g in range(SSD_GROUPS):
        bm = bcc[:, g * SSD_STATE:(g + 1) * SSD_STATE]
        cm = bcc[:, (SSD_GROUPS + g) * SSD_STATE:(SSD_GROUPS + g + 1) * SSD_STATE]
        cb = _dot_nt(cm, bm)
        acc = None
        for j in range(nseq):
            cmj = cm if nseq == 1 else jnp.where(seq_of_row == j, cm, 0.0)
            t = _dot_nt(cmj, sprev_ref[j, g * GE:(g + 1) * GE, :])
            acc = t if acc is None else acc + t
        ys.append(acc)
        for hp in range(HPG // 2):
            col0 = g * GE + hp * LANES
            xpair = xdt[:, col0:col0 + LANES].astype(bf16)
            for e in range(2):
                h = g * HPG + hp * 2 + e
                seg = cum[:, h:h + 1] - cum_t[h:h + 1, :]
                m = jnp.exp(jnp.where(causal, seg, NEG)) * cb
                intra.append(jnp.dot(m.astype(bf16), xpair, preferred_element_type=f32))
    ssq = jnp.zeros((L, 1), f32)
    for g in range(SSD_GROUPS):
        for hp in range(HPG // 2):
            col0 = g * GE + hp * LANES
            blk = slice(col0, col0 + LANES)
            o0, o1 = intra[col0 // LANES * 2], intra[col0 // LANES * 2 + 1]
            y = (jnp.where(lane_lo, o0, o1)
                 + ecum[:, blk] * ys[g][:, hp * LANES:(hp + 1) * LANES]
                 + dexp_ref[:, blk] * xc[:, blk])
            yz = y * _silu(z_ref[:, blk])
            ssq = ssq + jnp.sum(yz * yz, axis=-1, keepdims=True)
            yz_ref[:, blk] = yz
    r = lax.rsqrt(ssq * (1.0 / SSD_D_INNER) + EPS)
    y_ref[...] = ((yz_ref[...] * r) * nw_ref[...]).astype(bf16)

    for j in range(nseq):
        tl = (j + 1) * Ls - 1
        dec = jnp.exp(jnp.broadcast_to(cum_t[:, tl:tl + 1], (LANES, LANES)))
        for g in range(SSD_GROUPS):
            bm = bcc[:, g * SSD_STATE:(g + 1) * SSD_STATE]
            xw = xdtw[:, g * GE:(g + 1) * GE]
            if nseq > 1:
                xw = jnp.where(seq_of_row == j, xw, 0.0)
            ds = _dot_tn(xw, bm)
            for e in range(HPG):
                h = g * HPG + e
                rs = slice(h * SSD_HEAD_DIM, (h + 1) * SSD_HEAD_DIM)
                s_ref[j, rs, :] = (dec[h:h + 1, :] * sprev_ref[j, rs, :]
                                   + ds[e * SSD_HEAD_DIM:(e + 1) * SSD_HEAD_DIM, :])


def _ssd(u, us, hx, hbc, s0, wts, *, B, T, L, Ls):
    nseq = L // Ls
    n_chunks = T // Ls if nseq == 1 else 1
    nblk = B // nseq
    rb = lambda i, c: i * n_chunks + c
    if nseq > 1:
        assert Ls == SUBLANES
        h_specs = [pl.BlockSpec((nseq, SSD_CONV - 1, SSD_D_INNER), lambda i, c: (i, 0, 0)),
                   pl.BlockSpec((nseq, SSD_CONV - 1, SSD_BC), lambda i, c: (i, 0, SSD_D_INNER // SSD_BC))]
    else:
        h_specs = [pl.BlockSpec((1, SUBLANES, SSD_D_INNER), lambda i, c: (i, 0, 0)),
                   pl.BlockSpec((1, SUBLANES, SSD_BC), lambda i, c: (i, 0, 0))]
    const = lambda shape: pl.BlockSpec(shape, lambda i, c: (0,) * len(shape))
    kern = functools.partial(_ssd_kernel, L=L, Ls=Ls, n_chunks=n_chunks)
    return pl.pallas_call(
        kern,
        grid=(nblk, n_chunks),
        in_specs=[pl.BlockSpec((L, SSD_D_INNER), lambda i, c: (rb(i, c), U_Z)),
                  pl.BlockSpec((L, SSD_D_INNER), lambda i, c: (rb(i, c), U_X)),
                  pl.BlockSpec((L, SSD_BC), lambda i, c: (rb(i, c), U_BC)),
                  pl.BlockSpec((L, U_SMALL), lambda i, c: (rb(i, c), 0)),
                  *h_specs,
                  const((SSD_CONV, SSD_D_INNER)), const((SSD_CONV, SSD_BC)),
                  const((1, SSD_D_INNER)), const((1, SSD_BC)),
                  const((1, LANES)), const((1, LANES)),
                  const((1, SSD_D_INNER)), const((1, SSD_D_INNER)),
                  const((2 * LANES, SSD_D_INNER)),
                  pl.BlockSpec((nseq, SSD_D_INNER, SSD_STATE), lambda i, c: (i, 0, 0))],
        out_specs=[pl.BlockSpec((L, SSD_D_INNER), lambda i, c: (rb(i, c), 0)),
                   pl.BlockSpec((nseq, SSD_D_INNER, SSD_STATE), lambda i, c: (i, 0, 0))],
        out_shape=[jax.ShapeDtypeStruct((B * T, SSD_D_INNER), bf16),
                   jax.ShapeDtypeStruct((B, SSD_D_INNER, SSD_STATE), f32)],
        scratch_shapes=[pltpu.VMEM((SUBLANES, SSD_D_INNER), f32),
                        pltpu.VMEM((SUBLANES, SSD_BC), f32),
                        pltpu.VMEM((L, SSD_D_INNER), f32)],
        compiler_params=_params(2),
        name="ssd",
    )(u, u, u, us, hx, hbc, wts["cw_x"], wts["cw_bc"], wts["cb_x"], wts["cb_bc"],
      wts["dt_bias"], wts["a_log"], wts["d_exp"], wts["ssd_norm_w"], wts["ehp"], s0)


def _interleave(gens):
    while gens:
        alive = []
        for g in gens:
            try:
                next(g)
                alive.append(g)
            except StopIteration:
                pass
        gens = alive


def _group_views(refs, g, nseq):
    return [r.at[g] if kind == "tok" else r.at[pl.ds(g * nseq, nseq)] if kind == "state" else r
            for r, kind in refs]


def _mlstm_kernel(*refs, L, Ls, n_chunks, groups):
    kinds = ["tok"] * 5 + ["const"] * 3 + ["state"] * 3 + ["tok"] + ["state"] * 3
    _interleave([_mlstm_chunk(*_group_views(list(zip(refs, kinds)), g, L // Ls), L=L, Ls=Ls, n_chunks=n_chunks)
                 for g in range(groups)])


def _mlstm_chunk(q_ref, k_ref, v_ref, o_ref, sm_ref, ib_ref, fb_ref, nw_ref, c0_ref, n0_ref, m0_ref,
                 y_ref, c_ref, n_ref, m_ref, *, L, Ls, n_chunks):
    nseq = L // Ls
    c = pl.program_id(1)
    if n_chunks > 1:
        @pl.when(c == 0)
        def _():
            c_ref[...] = c0_ref[...]
            n_ref[...] = n0_ref[...]
            m_ref[...] = m0_ref[...]
        cprev_ref, nprev_ref, mprev_ref = c_ref, n_ref, m_ref
    else:
        cprev_ref, nprev_ref, mprev_ref = c0_ref, n0_ref, m0_ref

    per_tok = lambda a: jnp.broadcast_to(a, (nseq, Ls, LANES)).reshape(L, LANES)
    ig = sm_ref[:, :LANES] + ib_ref[...]
    fraw = sm_ref[:, LANES:] + fb_ref[...]
    lf = -_softplus(-fraw)
    causal, tri, _, last = _seq_masks(L, Ls)
    eye = _eye()
    F = _sel_left(tri, lf)
    FL = _sel_left(last, F)
    mp = per_tok(mprev_ref[:, 0:1, :])
    r_t = _transpose_exact(ig - F, eye)
    inter = F + mp
    lw = FL - F + ig
    segmax = jnp.max(lw.reshape(nseq, Ls, LANES), axis=1, keepdims=True)
    m_new = jnp.maximum(FL + mp, per_tok(segmax))
    sc = jnp.exp(lw - m_new)
    dec = jnp.exp(FL + mp - m_new)
    m_out = m_new.reshape(nseq, Ls, LANES)[:, 0:SUBLANES, :]
    yield

    shift = Ls.bit_length() - 1
    seq_of_row = _iota2((L, 1), 0) >> shift
    kscale = ML_QK_DIM ** -0.5
    heads = range(ML_HEADS)
    qcols = lambda h: slice(h * ML_QK_DIM, (h + 1) * ML_QK_DIM)
    vcols = lambda h: slice(h * ML_V_DIM, (h + 1) * ML_V_DIM)
    q = [q_ref[:, qcols(h)] for h in heads]
    k = [k_ref[:, qcols(h)] * kscale for h in heads]
    v = [v_ref[:, vcols(h)].astype(bf16) for h in heads]
    qk = [_dot_nt(q[h], k[h]) for h in heads]
    yield
    qc = []
    for h in heads:
        acc = None
        for j in range(nseq):
            qj = q[h] if nseq == 1 else jnp.where(seq_of_row == j, q[h], 0.0)
            t = _dot(qj, cprev_ref[j, qcols(h), :])
            acc = t if acc is None else acc + t
        qc.append(acc)
    yield
    rep = lambda col: jnp.broadcast_to(col, (L, LANES))
    twice = lambda a: jnp.concatenate([a, a], axis=1)
    qn = []
    for h in heads:
        acc = None
        for j in range(nseq):
            qj = q[h] if nseq == 1 else jnp.where(seq_of_row == j, q[h], 0.0)
            t = _dot_nt(qj, jnp.broadcast_to(nprev_ref[j, h:h + 1, :], (LANES, ML_QK_DIM)))
            acc = t if acc is None else acc + t
        qn.append(acc)
    yield
    dm, m_rep, inter_rep = [], [], []
    for h in heads:
        gl = GATE_LANE + h
        d = jnp.where(causal, F[:, gl:gl + 1] + r_t[gl:gl + 1, :], NEG)
        dm.append(d)
        inter_rep.append(rep(inter[:, gl:gl + 1]))
        m_rep.append(jnp.maximum(rep(jnp.max(d, axis=-1, keepdims=True)), inter_rep[h]))
    yield
    w = [jnp.exp(dm[h] - (m_rep[h] if L == LANES else m_rep[h][:, 0:1])) * qk[h] for h in heads]
    yield
    wv = [jnp.dot(w[h].astype(bf16), v[h], preferred_element_type=f32) for h in heads]
    wsum = [_rowsum_mxu(w[h]) for h in heads]
    yield
    hh = []
    for h in heads:
        wi = jnp.exp(inter_rep[h] - m_rep[h])
        den = wsum[h] + wi * qn[h]
        inv = 1.0 / jnp.maximum(jnp.abs(den), jnp.exp(-m_rep[h]))
        hh.append((wv[h] + twice(wi) * qc[h]) * twice(inv))
    yield
    ssq = [_rowsum_mxu(hh[h] * hh[h]) for h in heads]
    yield
    for h in heads:
        r = lax.rsqrt(ssq[h] * (1.0 / ML_V_DIM) + EPS)
        hn = (hh[h] * twice(r)) * nw_ref[:, vcols(h)]
        y_ref[:, vcols(h)] = (_sigmoid(o_ref[:, vcols(h)]) * hn).astype(bf16)
        if h % 2 == 1:
            yield
    for h in heads:
        gl = GATE_LANE + h
        ksc = k[h] * sc[:, gl:gl + 1]
        for j in range(nseq):
            r0 = j * Ls
            dj = dec[r0:r0 + 1, gl:gl + 1]
            kj = ksc if nseq == 1 else jnp.where(seq_of_row == j, ksc, 0.0)
            c_ref[j, qcols(h), :] = dj * cprev_ref[j, qcols(h), :] + _dot_tn(kj, v[h])
            n_ref[j, h:h + 1, :] = (dj * nprev_ref[j, h:h + 1, :]
                                    + jnp.sum(ksc[r0:r0 + Ls, :], axis=0, keepdims=True))
        if h % 2 == 1:
            yield
    m_ref[...] = m_out


def _scan_groups(nblk, nseq):
    return SCAN_GROUPS if nseq == 1 and nblk % SCAN_GROUPS == 0 else 1


def _mlstm(u, us, c0, n0, m0, wts, *, B, T, L, Ls):
    nseq = L // Ls
    n_chunks = T // Ls if nseq == 1 else 1
    nblk = B // nseq
    G = _scan_groups(nblk, nseq)
    rows = n_chunks * L
    u3, us3 = u.reshape(nblk, rows, U_MAIN), us.reshape(nblk, rows, U_SMALL)
    tok = lambda width, col: pl.BlockSpec((G, L, width), lambda i, c: (i, c, col))
    const = lambda shape: pl.BlockSpec(shape, lambda i, c: (0,) * len(shape))
    st = lambda shape: pl.BlockSpec((G * nseq,) + shape, lambda i, c: (i, 0, 0))
    kern = functools.partial(_mlstm_kernel, L=L, Ls=Ls, n_chunks=n_chunks, groups=G)
    y, c_new, n_new, m_new = pl.pallas_call(
        kern,
        grid=(nblk // G, n_chunks),
        in_specs=[tok(ML_QK_INNER, U_Q), tok(ML_QK_INNER, U_K), tok(ML_D_INNER, U_V), tok(ML_D_INNER, U_O),
                  tok(U_SMALL, 0),
                  const((1, LANES)), const((1, LANES)), const((1, ML_D_INNER)),
                  st((ML_QK_INNER, ML_V_DIM)), st((ML_HEADS, ML_QK_DIM)), st((SUBLANES, LANES))],
        out_specs=[tok(ML_D_INNER, 0),
                   st((ML_QK_INNER, ML_V_DIM)), st((ML_HEADS, ML_QK_DIM)), st((SUBLANES, LANES))],
        out_shape=[jax.ShapeDtypeStruct((nblk, rows, ML_D_INNER), bf16),
                   jax.ShapeDtypeStruct((B, ML_QK_INNER, ML_V_DIM), f32),
                   jax.ShapeDtypeStruct((B, ML_HEADS, ML_QK_DIM), f32),
                   jax.ShapeDtypeStruct((B, SUBLANES, LANES), f32)],
        compiler_params=_params(2),
        name="mlstm",
    )(u3, u3, u3, u3, us3, wts["i_bias"], wts["f_bias"], wts["ml_norm_w"], c0, n0, m0)
    return y.reshape(B * T, ML_D_INNER), c_new, n_new, m_new


def _outproj_cast_kernel(ys_ref, ym_ref, ws_ref, wm_ref, h_ref, o_ref, wsb_ref, wmb_ref):
    @pl.when(pl.program_id(1) == 0)
    def _():
        wsb_ref[...] = ws_ref[...].astype(bf16)
        wmb_ref[...] = wm_ref[...].astype(bf16)

    o_ref[...] = (h_ref[...]
                  + jnp.dot(ys_ref[...], wsb_ref[...], preferred_element_type=f32)
                  + jnp.dot(ym_ref[...], wmb_ref[...], preferred_element_type=f32))


def _outproj_kernel(ys_ref, ym_ref, ws_ref, wm_ref, h_ref, o_ref):
    o_ref[...] = (h_ref[...]
                  + jnp.dot(ys_ref[...], ws_ref[...], preferred_element_type=f32)
                  + jnp.dot(ym_ref[...], wm_ref[...], preferred_element_type=f32))


def _outproj(ys, ym, wts, h):
    M = h.shape[0]
    tm = _row_tile(M)
    if "w_out_s" in wts:
        tn = OUTPROJ_TILE
        h1 = pl.pallas_call(
            _outproj_kernel,
            grid=(M // tm, D_MODEL // tn),
            in_specs=[pl.BlockSpec((tm, SSD_D_INNER), lambda i, j: (i, 0)),
                      pl.BlockSpec((tm, ML_D_INNER), lambda i, j: (i, 0)),
                      pl.BlockSpec((SSD_D_INNER, tn), lambda i, j: (0, j)),
                      pl.BlockSpec((ML_D_INNER, tn), lambda i, j: (0, j)),
                      pl.BlockSpec((tm, tn), lambda i, j: (i, j))],
            out_specs=pl.BlockSpec((tm, tn), lambda i, j: (i, j)),
            out_shape=jax.ShapeDtypeStruct((M, D_MODEL), f32),
            compiler_params=_params(2),
            name="outproj",
        )(ys, ym, wts["w_out_s"], wts["w_out_m"], h)
        return h1, {}
    tn = OUTPROJ_CAST_TILE
    wspec = pl.BlockSpec((SSD_D_INNER, tn), lambda j, i: (0, j))
    wshape = jax.ShapeDtypeStruct((SSD_D_INNER, D_MODEL), bf16)
    h1, ws, wm = pl.pallas_call(
        _outproj_cast_kernel,
        grid=(D_MODEL // tn, M // tm),
        in_specs=[pl.BlockSpec((tm, SSD_D_INNER), lambda j, i: (i, 0)),
                  pl.BlockSpec((tm, ML_D_INNER), lambda j, i: (i, 0)),
                  wspec,
                  pl.BlockSpec((ML_D_INNER, tn), lambda j, i: (1, j)),
                  pl.BlockSpec((tm, tn), lambda j, i: (i, j))],
        out_specs=[pl.BlockSpec((tm, tn), lambda j, i: (i, j)), wspec, wspec],
        out_shape=[jax.ShapeDtypeStruct((M, D_MODEL), f32), wshape, wshape],
        compiler_params=_params(2),
        name="outproj_cast",
    )(ys, ym, wts["w_out"], wts["w_out"], h)
    return h1, dict(w_out_s=ws, w_out_m=wm)


def _ffn_kernel(h_ref, nw_ref, wg_ref, wv_ref, cwg_ref, cwv_ref, cbg_ref, cbv_ref, wd_ref, fw_ref, *rest,
                tm, tf, multi, blocks_per_seq, cast):
    if multi:
        hg_ref, hv_ref, y_ref, tg_ref, tv_ref, xn_ref = rest
    elif cast:
        hg_ref, hv_ref, y_ref, tg_ref, tv_ref, wgb_ref, wvb_ref, wdb_ref, xn_ref, carg_ref, carv_ref = rest
    else:
        hg_ref, hv_ref, y_ref, tg_ref, tv_ref, xn_ref, carg_ref, carv_ref = rest
    i = pl.program_id(0)
    j = pl.program_id(1)

    if cast:
        wgb_ref[...] = wg_ref[...].astype(bf16)
        wvb_ref[...] = wv_ref[...].astype(bf16)
        wdb_ref[...] = wd_ref[...].astype(bf16)
        wg_ref, wv_ref, wd_ref = wgb_ref, wvb_ref, wdb_ref
    subs = [slice(c0, c0 + MXU_COLS) for c0 in range(0, tf, MXU_COLS)]

    def tile(xn, first):
        up_dots = lambda cs: [jnp.dot(xn, w_ref[:, cs], preferred_element_type=f32) for w_ref in (wg_ref, wv_ref)]
        ups_next = up_dots(subs[0])
        for n, cs in enumerate(subs):
            ups = ups_next
            if n + 1 < len(subs):
                ups_next = up_dots(subs[n + 1])
            convd = []
            for half, (up, cw_ref, cb_ref) in enumerate(zip(ups, (cwg_ref, cwv_ref), (cbg_ref, cbv_ref))):
                if multi:
                    s_ref, t_ref = ((hg_ref, tg_ref), (hv_ref, tv_ref))[half]
                    prev = [s_ref[:, k:k + 1, cs] for k in range(FFN_CONV - 1)]
                    convd.append(_causal_conv_seqs(up, prev, cw_ref[:, cs], cb_ref[:, cs]))
                    up3 = up.reshape(tm // SUBLANES, SUBLANES, MXU_COLS)
                    t_ref[:, :, cs] = up3[:, SUBLANES - (FFN_CONV - 1):, :]
                else:
                    h_ref_, car_ref, t_ref = ((hg_ref, carg_ref, tg_ref), (hv_ref, carv_ref, tv_ref))[half]
                    tail = up[tm - SUBLANES:, :]
                    if blocks_per_seq == 1:
                        hist = h_ref_[0, :, cs]
                    else:
                        hist = jnp.where((i % blocks_per_seq) == 0, h_ref_[0, :, cs], car_ref[j, :, cs])
                        car_ref[j, :, cs] = tail
                    t_ref[0, :, cs] = tail
                    convd.append(_causal_conv(up, hist, cw_ref[:, cs], cb_ref[:, cs]))
            act = (_silu(convd[0]) * convd[1]).astype(bf16)
            down = jnp.dot(act, wd_ref[cs, :], preferred_element_type=f32)
            if first and n == 0:
                y_ref[...] = down
            else:
                y_ref[...] += down

    @pl.when(j == 0)
    def _():
        xn = _rms(h_ref[...], nw_ref[...]).astype(bf16)
        xn_ref[...] = xn
        tile(xn, True)

    @pl.when(j > 0)
    def _():
        tile(xn_ref[...], False)

    @pl.when(j == pl.num_programs(1) - 1)
    def _():
        y_ref[...] = _rms(h_ref[...] + y_ref[...], fw_ref[...])


def _ffn(h, ffn0, wts, *, B, T):
    M = h.shape[0]
    tm = _row_tile(M)
    tf = FFN_TILE
    n_ff = D_FF // tf
    multi = T < tm
    const = lambda shape: pl.BlockSpec(shape, lambda i, j: (0,) * len(shape))
    if multi:
        assert T == SUBLANES
        blocks_per_seq = 1
        nseq = tm // T
        hist = [ffn0, ffn0]
        h_specs = [pl.BlockSpec((nseq, FFN_CONV - 1, tf), lambda i, j: (i, 0, j)),
                   pl.BlockSpec((nseq, FFN_CONV - 1, tf), lambda i, j: (i, 0, n_ff + j))]
        t_specs = [pl.BlockSpec((nseq, FFN_CONV - 1, tf), lambda i, j: (i, 0, j))] * 2
        t_shapes = [jax.ShapeDtypeStruct((B, FFN_CONV - 1, D_FF), f32)] * 2
        scratch = []
    else:
        blocks_per_seq = T // tm
        pad = jnp.pad(ffn0, ((0, 0), (SUBLANES - (FFN_CONV - 1), 0), (0, 0)))
        hist = [pad, pad]
        h_specs = [pl.BlockSpec((1, SUBLANES, tf), lambda i, j: (i // blocks_per_seq, 0, j)),
                   pl.BlockSpec((1, SUBLANES, tf), lambda i, j: (i // blocks_per_seq, 0, n_ff + j))]
        t_specs = [pl.BlockSpec((1, SUBLANES, tf), lambda i, j: (i, 0, j))] * 2
        t_shapes = [jax.ShapeDtypeStruct((M // tm, SUBLANES, D_FF), f32)] * 2
        scratch = [pltpu.VMEM((n_ff, SUBLANES, tf), f32)] * 2
    cast = "w_up_g" not in wts
    assert not (cast and (multi or M != tm)), "the casting call must see every weight tile exactly once"
    up_spec = pl.BlockSpec((D_MODEL, tf), lambda i, j: (0, j))
    down_spec = pl.BlockSpec((tf, D_MODEL), lambda i, j: (j, 0))
    if cast:
        w_specs = [up_spec, pl.BlockSpec((D_MODEL, tf), lambda i, j: (0, n_ff + j)), down_spec]
        w_args = [wts["w_up"], wts["w_up"], wts["w_down"]]
        wb_specs = [up_spec, up_spec, down_spec]
        wb_shapes = [jax.ShapeDtypeStruct((D_MODEL, D_FF), bf16)] * 2 + [jax.ShapeDtypeStruct((D_FF, D_MODEL), bf16)]
    else:
        w_specs = [up_spec, up_spec, down_spec]
        w_args = [wts["w_up_g"], wts["w_up_v"], wts["w_down_b"]]
        wb_specs, wb_shapes = [], []
    kern = functools.partial(_ffn_kernel, tm=tm, tf=tf, multi=multi, blocks_per_seq=blocks_per_seq, cast=cast)
    y, *outs = pl.pallas_call(
        kern,
        grid=(M // tm, n_ff),
        in_specs=[pl.BlockSpec((tm, D_MODEL), lambda i, j: (i, 0), pipeline_mode=pl.Buffered(1)),
                  const((1, D_MODEL)),
                  w_specs[0], w_specs[1],
                  pl.BlockSpec((FFN_CONV, tf), lambda i, j: (0, j)),
                  pl.BlockSpec((FFN_CONV, tf), lambda i, j: (0, n_ff + j)),
                  pl.BlockSpec((1, tf), lambda i, j: (0, j)),
                  pl.BlockSpec((1, tf), lambda i, j: (0, n_ff + j)),
                  w_specs[2],
                  const((1, D_MODEL)),
                  *h_specs],
        out_specs=[pl.BlockSpec((tm, D_MODEL), lambda i, j: (i, 0)), *t_specs, *wb_specs],
        out_shape=[jax.ShapeDtypeStruct((M, D_MODEL), f32), *t_shapes, *wb_shapes],
        scratch_shapes=[pltpu.VMEM((tm, D_MODEL), bf16), *scratch],
        compiler_params=_params(2),
        name="ffn_cast" if cast else "ffn",
    )(h, wts["norm2_w"], w_args[0], w_args[1], wts["ffn_cw"], wts["ffn_cw"],
      wts["ffn_cb"], wts["ffn_cb"], w_args[2], wts["final_norm_w"], *hist)
    tails, new_w = outs[:2], {}
    if cast:
        new_w = dict(w_up_g=outs[2], w_up_v=outs[3], w_down_b=outs[4])
    if multi:
        return y, jnp.concatenate(tails, axis=-1), new_w
    last = jnp.concatenate(tails, axis=-1).reshape(B, blocks_per_seq, SUBLANES, 2 * D_FF)
    return y, last[:, blocks_per_seq - 1, SUBLANES - (FFN_CONV - 1):, :], new_w


def _hist_tile(state):
    return jnp.pad(state, ((0, 0), (SUBLANES - state.shape[1], 0), (0, 0)))


def _mixer(h, states, wts, *, B, T, L, Ls):
    conv0, s0, c0, n0, m0 = states
    u, us, cast_w = _inproj(h, wts)
    wts = {**wts, **cast_w}
    if L // Ls > 1:
        hx = hbc = conv0
    else:
        hx, hbc = _hist_tile(conv0[:, :, :SSD_D_INNER]), _hist_tile(conv0[:, :, SSD_D_INNER:])
    y_ssd, s_new = _ssd(u, us, hx, hbc, s0.reshape(B, SSD_D_INNER, SSD_STATE), wts, B=B, T=T, L=L, Ls=Ls)
    m_pad = jnp.broadcast_to(
        jnp.pad(m0, ((0, 0), (GATE_LANE, LANES - GATE_LANE - ML_HEADS)))[:, None, :], (B, SUBLANES, LANES))
    y_ml, c_new, n_new, m_new = _mlstm(u, us, c0.reshape(B, ML_QK_INNER, ML_V_DIM), n0, m_pad, wts,
                                       B=B, T=T, L=L, Ls=Ls)
    h1, cast_w = _outproj(y_ssd, y_ml, wts, h)
    wts = {**wts, **cast_w}
    ur = u.reshape(B, T, U_MAIN)[:, T - (SSD_CONV - 1):, :]
    conv_new = jnp.concatenate([ur[:, :, U_X * SSD_D_INNER:(U_X + 1) * SSD_D_INNER],
                                ur[:, :, U_BC * SSD_BC:(U_BC + 1) * SSD_BC]], axis=-1)
    new_states = (conv_new,
                  s_new.reshape(B, SSD_HEADS, SSD_HEAD_DIM, SSD_STATE),
                  c_new.reshape(B, ML_HEADS, ML_QK_DIM, ML_V_DIM),
                  n_new,
                  m_new[:, 0, GATE_LANE:GATE_LANE + ML_HEADS])
    return h1, new_states, wts


def _prep_weights(norm1_w, w_in, ssd_conv_w, ssd_conv_b, ssd_dt_bias, ssd_A_log, ssd_D, ssd_norm_w,
                  ml_i_bias, ml_f_bias, ml_norm_w, w_out, norm2_w, w_up, ffn_conv_w, ffn_conv_b, w_down,
                  final_norm_w):
    w_t = w_in.T
    o = 0
    rows, offs = {}, {}
    for name, width in (("z", SSD_D_INNER), ("x", SSD_D_INNER), ("bc", SSD_BC), ("dt", SSD_HEADS),
                        ("q", ML_QK_INNER), ("k", ML_QK_INNER), ("v", ML_D_INNER), ("i", ML_HEADS),
                        ("f", ML_HEADS), ("o", ML_D_INNER)):
        offs[name] = (o, width)
        if name in ("dt", "i", "f"):
            rows[name] = lax.optimization_barrier(w_t[o:o + width, :]).astype(bf16)
        o += width
    zpad = lambda n: jnp.zeros((n, D_MODEL), bf16)
    w_small = jnp.concatenate([rows["dt"], rows["i"], zpad(LANES - GATE_LANE - ML_HEADS),
                               zpad(GATE_LANE), rows["f"], zpad(LANES - GATE_LANE - ML_HEADS)], axis=0)
    lane_row = lambda v, off: jnp.pad(v.astype(f32), (off, LANES - off - v.shape[0]))[None, :]
    hp = jnp.arange(SSD_D_INNER) // SSD_HEAD_DIM
    ehp = (jnp.arange(LANES)[:, None] == hp[None, :]).astype(bf16)
    return dict(
        norm1_w=norm1_w[None, :], w_in_t=w_t, w_in_offs=offs, w_small=w_small,
        cw_x=ssd_conv_w[:, :SSD_D_INNER], cw_bc=ssd_conv_w[:, SSD_D_INNER:],
        cb_x=ssd_conv_b[None, :SSD_D_INNER], cb_bc=ssd_conv_b[None, SSD_D_INNER:],
        dt_bias=lane_row(ssd_dt_bias, 0), a_log=lane_row(ssd_A_log, 0),
        d_exp=jnp.repeat(ssd_D.astype(f32), SSD_HEAD_DIM)[None, :], ssd_norm_w=ssd_norm_w[None, :],
        ehp=jnp.concatenate([ehp, ehp], axis=0),
        i_bias=lane_row(ml_i_bias, GATE_LANE), f_bias=lane_row(ml_f_bias, GATE_LANE),
        ml_norm_w=ml_norm_w[None, :],
        w_out=w_out, norm2_w=norm2_w[None, :], w_up=w_up,
        ffn_cw=ffn_conv_w, ffn_cb=ffn_conv_b[None, :], w_down=w_down,
        final_norm_w=final_norm_w[None, :])


def kernel(x_prompt, x_sample, state_ssd_conv, state_ssd, state_mlstm_C, state_mlstm_n, state_mlstm_m,
           state_ffn_conv, meta_tokens, norm1_w, w_in, ssd_conv_w, ssd_conv_b, ssd_dt_bias, ssd_A_log,
           ssd_D, ssd_norm_w, ml_i_bias, ml_f_bias, ml_norm_w, w_out, norm2_w, w_up, ffn_conv_w,
           ffn_conv_b, w_down, final_norm_w):
    depth = w_in.shape[0]
    assert depth == 1, "single-layer step"
    Bp, Tp, _ = x_prompt.shape
    Bs, Ts, _ = x_sample.shape
    wts = _prep_weights(norm1_w[0], w_in[0], ssd_conv_w[0], ssd_conv_b[0], ssd_dt_bias[0], ssd_A_log[0],
                        ssd_D[0], ssd_norm_w[0], ml_i_bias[0], ml_f_bias[0], ml_norm_w[0], w_out[0],
                        norm2_w[0], w_up[0], ffn_conv_w[0], ffn_conv_b[0], w_down[0], final_norm_w)
    s_init = (state_ssd_conv[0], state_ssd[0], state_mlstm_C[0], state_mlstm_n[0], state_mlstm_m[0])
    hs, s_new, wts = _mixer(x_sample.reshape(Bs * Ts, D_MODEL), s_init, wts, B=Bs, T=Ts, L=8 * Ts, Ls=Ts)
    zero_states = (jnp.zeros((1, SSD_CONV - 1, SSD_CONV_DIM), f32),
                   jnp.zeros((1, SSD_HEADS, SSD_HEAD_DIM, SSD_STATE), f32),
                   jnp.zeros((1, ML_HEADS, ML_QK_DIM, ML_V_DIM), f32),
                   jnp.zeros((1, ML_HEADS, ML_QK_DIM), f32),
                   jnp.zeros((1, ML_HEADS), f32))
    hm, m_new, wts = _mixer(meta_tokens.astype(f32), zero_states, wts, B=1, T=N_META, L=N_META, Ls=N_META)
    _, m_ffn, cast_w = _ffn(hm, jnp.zeros((1, FFN_CONV - 1, 2 * D_FF), f32), wts, B=1, T=N_META)
    wts = {**wts, **cast_w}
    ys, s_ffn, _ = _ffn(hs, state_ffn_conv[0], wts, B=Bs, T=Ts)
    p_init = tuple(jnp.broadcast_to(s, (Bp,) + s.shape[1:]) for s in (*m_new, m_ffn))
    hp, p_new, wts = _mixer(x_prompt.reshape(Bp * Tp, D_MODEL), p_init[:5], wts, B=Bp, T=Tp, L=128, Ls=128)
    yp, p_ffn, _ = _ffn(hp, p_init[5], wts, B=Bp, T=Tp)
    return (yp.reshape(Bp, Tp, D_MODEL), ys.reshape(Bs, Ts, D_MODEL),
            *(s[None] for s in (*p_new, p_ffn)), *(s[None] for s in (*s_new, s_ffn)))
```

```python
import functools

import jax
import jax.numpy as jnp
from jax import lax
from jax.experimental import pallas as pl
from jax.experimental.pallas import tpu as pltpu

f32 = jnp.float32
bf16 = jnp.bfloat16

D_MODEL = 2048
N_META = 16
SSD_HEADS = 32
SSD_HEAD_DIM = 64
SSD_D_INNER = SSD_HEADS * SSD_HEAD_DIM
SSD_GROUPS = 2
SSD_STATE = 128
SSD_CONV = 4
SSD_BC = 2 * SSD_GROUPS * SSD_STATE
SSD_CONV_DIM = SSD_D_INNER + SSD_BC
ML_HEADS = 8
ML_QK_DIM = 128
ML_V_DIM = 256
ML_QK_INNER = ML_HEADS * ML_QK_DIM
ML_D_INNER = ML_HEADS * ML_V_DIM
D_FF = 5632
FFN_CONV = 3
EPS = 1e-6
NEG = -1e30

LANES = 128
SUBLANES = 8
VMEM_LIMIT = 56 * 1024 * 1024
MXU_COLS = 256
ROW_TILE = 1024
FFN_TILE = 512
INPROJ_TILE = 1536
OUTPROJ_TILE = 1024
OUTPROJ_CAST_TILE = 512
INPROJ_CAST_TILE = 512
SCAN_GROUPS = 2


def _row_tile(M):
    return ROW_TILE if M % ROW_TILE == 0 else M


U_Z, U_X, U_V, U_O = 0, 1, 2, 3
U_Q, U_K = 8, 9
U_BC = 20
U_MAIN = 4 * 2048 + 2 * 1024 + 512
GATE_LANE = 32
U_SMALL = 2 * LANES


def _dot(a, b):
    return jnp.dot(a.astype(bf16), b.astype(bf16), preferred_element_type=f32)


def _dot_nt(a, b):
    return lax.dot_general(a.astype(bf16), b.astype(bf16), (((1,), (1,)), ((), ())),
                           preferred_element_type=f32)


def _dot_tn(a, b):
    return lax.dot_general(a.astype(bf16), b.astype(bf16), (((0,), (0,)), ((), ())),
                           preferred_element_type=f32)


def _split3(a):
    hi = a.astype(bf16)
    r1 = a - hi.astype(f32)
    mid = r1.astype(bf16)
    lo = (r1 - mid.astype(f32)).astype(bf16)
    return hi, mid, lo


def _sel_right(a, e01):
    return jnp.dot(jnp.concatenate(_split3(a), axis=1), jnp.concatenate([e01] * 3, axis=0),
                   preferred_element_type=f32)


def _sel_left(e01, a):
    return jnp.dot(jnp.concatenate([e01] * 3, axis=1), jnp.concatenate(_split3(a), axis=0),
                   preferred_element_type=f32)


def _split2(a):
    hi = a.astype(bf16)
    mid = (a - hi.astype(f32)).astype(bf16)
    return jnp.concatenate([hi, mid], axis=1)


def _expand_heads(a, e01x2):
    return jnp.dot(_split2(a), e01x2, preferred_element_type=f32)


def _rowsum_mxu(a):
    return jnp.dot(_split2(a), jnp.ones((2 * a.shape[1], LANES), bf16), preferred_element_type=f32)


def _transpose_exact(a, eye):
    return lax.dot_general(jnp.concatenate([eye] * 3, axis=1), jnp.concatenate(_split3(a), axis=1),
                           (((1,), (1,)), ((), ())), preferred_element_type=f32)


def _iota2(shape, axis):
    return lax.broadcasted_iota(jnp.int32, shape, axis)


def _as01(m):
    return jnp.where(m, 1.0, 0.0).astype(bf16)


def _eye():
    return _as01(_iota2((LANES, LANES), 0) == _iota2((LANES, LANES), 1))


def _seq_masks(L, Ls):
    t = _iota2((L, L), 0)
    s = _iota2((L, L), 1)
    shift = Ls.bit_length() - 1
    same = (t >> shift) == (s >> shift)
    causal = same & (s <= t)
    causal_t = same & (t <= s)
    last = s == (t | (Ls - 1))
    return causal, _as01(causal), _as01(causal_t), _as01(last)


def _sigmoid(x):
    return 1.0 / (1.0 + jnp.exp(-x))


def _silu(x):
    return x * _sigmoid(x)


def _softplus(x):
    return jnp.maximum(x, 0.0) + jnp.log(1.0 + jnp.exp(-jnp.abs(x)))


def _rms(x, w):
    r = lax.rsqrt(jnp.mean(x * x, axis=-1, keepdims=True) + EPS)
    return (x * r) * w


def _causal_conv(x, hist, w, b):
    L, C = x.shape
    K = w.shape[0]
    r = _iota2((SUBLANES, C), 0)
    y = b + x * w[K - 1:K, :]
    for s in range(1, K):
        zt = jnp.where(r >= SUBLANES - s, hist, x[L - SUBLANES:, :])
        z = zt if L == SUBLANES else jnp.concatenate([x[:L - SUBLANES, :], zt], axis=0)
        y = y + pltpu.roll(z, s, 0) * w[K - 1 - s:K - s, :]
    return y


def _causal_conv_seqs(x, prev, w, b):
    L, C = x.shape
    K = w.shape[0]
    nseq = L // SUBLANES
    r = _iota2((L, C), 0) & (SUBLANES - 1)
    per_row = lambda a: jnp.broadcast_to(a, (nseq, SUBLANES, C)).reshape(L, C)
    y = b + x * w[K - 1:K, :]
    for s in range(1, K):
        head = per_row(prev[K - 1 - s])
        for rr in range(1, s):
            head = jnp.where(r == rr, per_row(prev[K - 1 - s + rr]), head)
        y = y + jnp.where(r >= s, pltpu.roll(x, s, 0), head) * w[K - 1 - s:K - s, :]
    return y


def _params(n_axes):
    return pltpu.CompilerParams(dimension_semantics=("arbitrary",) * n_axes,
                                vmem_limit_bytes=VMEM_LIMIT)


def _inproj_kernel(x_ref, nw_ref, w_ref, ws_ref, u_ref, us_ref, xn_ref):
    @pl.when(pl.program_id(1) == 0)
    def _():
        xn = _rms(x_ref[...], nw_ref[...]).astype(bf16)
        xn_ref[...] = xn
        us_ref[...] = _dot_nt(xn, ws_ref[...])
        u_ref[...] = _dot_nt(xn, w_ref[...])

    @pl.when(pl.program_id(1) > 0)
    def _():
        u_ref[...] = _dot_nt(xn_ref[...], w_ref[...])


def _inproj_cast_kernel(x_ref, nw_ref, w_ref, ws_ref, u_ref, us_ref, wb_ref, xn_ref):
    @pl.when(pl.program_id(1) == 0)
    def _():
        xn = _rms(x_ref[...], nw_ref[...]).astype(bf16)
        xn_ref[...] = xn
        us_ref[...] = _dot_nt(xn, ws_ref[...])

    wb_ref[...] = w_ref[...].astype(bf16)
    u_ref[...] = _dot_nt(xn_ref[...], wb_ref[...])


def _inproj(x, wts):
    M = x.shape[0]
    tm = _row_tile(M)
    const = lambda shape: pl.BlockSpec(shape, lambda i, j: (0,) * len(shape))
    out_shape = [jax.ShapeDtypeStruct((M, U_MAIN), f32), jax.ShapeDtypeStruct((M, U_SMALL), f32)]
    if "w_main" in wts:
        tn = INPROJ_TILE
        u, us = pl.pallas_call(
            _inproj_kernel,
            grid=(M // tm, U_MAIN // tn),
            in_specs=[pl.BlockSpec((tm, D_MODEL), lambda i, j: (i, 0)), const((1, D_MODEL)),
                      pl.BlockSpec((tn, D_MODEL), lambda i, j: (j, 0)), const((U_SMALL, D_MODEL))],
            out_specs=[pl.BlockSpec((tm, tn), lambda i, j: (i, j)),
                       pl.BlockSpec((tm, U_SMALL), lambda i, j: (i, 0))],
            out_shape=out_shape,
            scratch_shapes=[pltpu.VMEM((tm, D_MODEL), bf16)],
            compiler_params=_params(2),
            name="inproj",
        )(x, wts["norm1_w"], wts["w_main"], wts["w_small"])
        return u, us, {}
    assert M == tm, "the casting call must see every weight tile exactly once"
    tn = INPROJ_CAST_TILE
    bounds, dst = [], 0
    for name in ("z", "x", "v", "o", "q", "k", "bc"):
        src, width = wts["w_in_offs"][name]
        assert width % tn == 0 and dst % tn == 0
        bounds.append((dst // tn, src))
        dst += width
    unit = 2 * SUBLANES
    assert dst == U_MAIN and all(src % unit == 0 for _, src in bounds)

    def src_row(j):
        row = jnp.int32(0)
        for first_blk, src in bounds:
            row = jnp.where(j >= first_blk, src // unit + (j - first_blk) * (tn // unit), row)
        return row * unit

    u, us, w_main = pl.pallas_call(
        _inproj_cast_kernel,
        grid=(1, U_MAIN // tn),
        in_specs=[pl.BlockSpec((tm, D_MODEL), lambda i, j: (i, 0)), const((1, D_MODEL)),
                  pl.BlockSpec((pl.Element(tn), pl.Element(D_MODEL)), lambda i, j: (src_row(j), 0)),
                  const((U_SMALL, D_MODEL))],
        out_specs=[pl.BlockSpec((tm, tn), lambda i, j: (i, j)),
                   pl.BlockSpec((tm, U_SMALL), lambda i, j: (i, 0)),
                   pl.BlockSpec((tn, D_MODEL), lambda i, j: (j, 0))],
        out_shape=out_shape + [jax.ShapeDtypeStruct((U_MAIN, D_MODEL), bf16)],
        scratch_shapes=[pltpu.VMEM((tm, D_MODEL), bf16)],
        compiler_params=_params(2),
        name="inproj_cast",
    )(x, wts["norm1_w"], wts["w_in_t"], wts["w_small"])
    return u, us, dict(w_main=w_main)


def _ssd_kernel(z_ref, x_ref, bc_ref, sm_ref, hx_ref, hbc_ref, cwx_ref, cwbc_ref, cbx_ref, cbbc_ref,
                dtb_ref, alog_ref, dexp_ref, nw_ref, ehp_ref, s0_ref,
                y_ref, s_ref, tailx_ref, tailbc_ref, yz_ref, *, L, Ls, n_chunks):
    nseq = L // Ls
    c = pl.program_id(1)
    xpre = x_ref[...]
    bcpre = bc_ref[...]
    if nseq > 1:
        prev_x = [hx_ref[:, k:k + 1, :] for k in range(SSD_CONV - 1)]
        prev_bc = [hbc_ref[:, k:k + 1, :] for k in range(SSD_CONV - 1)]
        xc = _silu(_causal_conv_seqs(xpre, prev_x, cwx_ref[...], cbx_ref[...]))
        bcc = _silu(_causal_conv_seqs(bcpre, prev_bc, cwbc_ref[...], cbbc_ref[...]))
    else:
        if n_chunks == 1:
            hx, hbc = hx_ref[0], hbc_ref[0]
        else:
            first = c == 0
            hx = jnp.where(first, hx_ref[0], tailx_ref[...])
            hbc = jnp.where(first, hbc_ref[0], tailbc_ref[...])
        xc = _silu(_causal_conv(xpre, hx, cwx_ref[...], cbx_ref[...]))
        bcc = _silu(_causal_conv(bcpre, hbc, cwbc_ref[...], cbbc_ref[...]))
    if n_chunks > 1:
        tailx_ref[...] = xpre[L - SUBLANES:, :]
        tailbc_ref[...] = bcpre[L - SUBLANES:, :]

    lane = _iota2((L, LANES), 1)
    dt = jnp.where(lane < SSD_HEADS, _softplus(sm_ref[:, :LANES] + dtb_ref[...]), 0.0)
    dta = dt * (-jnp.exp(alog_ref[...]))

    causal, tri, tri_t, last = _seq_masks(L, Ls)
    eye = _eye()
    cum = _sel_left(tri, dta)
    cum_t = _sel_right(_transpose_exact(dta, eye), tri_t)
    cum_last = _sel_left(last, cum)
    ehp = ehp_ref[...]
    xdt = xc * _expand_heads(dt, ehp)
    xdtw = xdt * jnp.exp(_expand_heads(cum_last - cum, ehp))
    ecum = jnp.exp(_expand_heads(cum, ehp))

    if n_chunks > 1:
        @pl.when(c == 0)
        def _():
            s_ref[...] = s0_ref[...]
        sprev_ref = s_ref
    else:
        sprev_ref = s0_ref

    GE = SSD_D_INNER // SSD_GROUPS
    HPG = SSD_HEADS // SSD_GROUPS
    shift = Ls.bit_length() - 1
    seq_of_row = _iota2((L, 1), 0) >> shift
    lane_lo = lane < SSD_HEAD_DIM
    ys, intra = [], []
    for g in range(SSD_GROUPS):
        bm = bcc[:, g * SSD_STATE:(g + 1) * SSD_STATE]
        cm = bcc[:, (SSD_GROUPS + g) * SSD_STATE:(SSD_GROUPS + g + 1) * SSD_STATE]
        cb = _dot_nt(cm, bm)
        acc = None
        for j in range(nseq):
            cmj = cm if nseq == 1 else jnp.where(seq_of_row == j, cm, 0.0)
            t = _dot_nt(cmj, sprev_ref[j, g * GE:(g + 1) * GE, :])
            acc = t if acc is None else acc + t
        ys.append(acc)
        for hp in range(HPG // 2):
            col0 = g * GE + hp * LANES
            xpair = xdt[:, col0:col0 + LANES].astype(bf16)
            for e in range(2):
                h = g * HPG + hp * 2 + e
                seg = cum[:, h:h + 1] - cum_t[h:h + 1, :]
                m = jnp.exp(jnp.where(causal, seg, NEG)) * cb
                intra.append(jnp.dot(m.astype(bf16), xpair, preferred_element_type=f32))
    ssq = jnp.zeros((L, 1), f32)
    for g in range(SSD_GROUPS):
        for hp in range(HPG // 2):
            col0 = g * GE + hp * LANES
            blk = slice(col0, col0 + LANES)
            o0, o1 = intra[col0 // LANES * 2], intra[col0 // LANES * 2 + 1]
            y = (jnp.where(lane_lo, o0, o1)
                 + ecum[:, blk] * ys[g][:, hp * LANES:(hp + 1) * LANES]
                 + dexp_ref[:, blk] * xc[:, blk])
            yz = y * _silu(z_ref[:, blk])
            ssq = ssq + jnp.sum(yz * yz, axis=-1, keepdims=True)
            yz_ref[:, blk] = yz
    r = lax.rsqrt(ssq * (1.0 / SSD_D_INNER) + EPS)
    y_ref[...] = ((yz_ref[...] * r) * nw_ref[...]).astype(bf16)

    for j in range(nseq):
        tl = (j + 1) * Ls - 1
        dec = jnp.exp(jnp.broadcast_to(cum_t[:, tl:tl + 1], (LANES, LANES)))
        for g in range(SSD_GROUPS):
            bm = bcc[:, g * SSD_STATE:(g + 1) * SSD_STATE]
            xw = xdtw[:, g * GE:(g + 1) * GE]
            if nseq > 1:
                xw = jnp.where(seq_of_row == j, xw, 0.0)
            ds = _dot_tn(xw, bm)
            for e in range(HPG):
                h = g * HPG + e
                rs = slice(h * SSD_HEAD_DIM, (h + 1) * SSD_HEAD_DIM)
                s_ref[j, rs, :] = (dec[h:h + 1, :] * sprev_ref[j, rs, :]
                                   + ds[e * SSD_HEAD_DIM:(e + 1) * SSD_HEAD_DIM, :])


def _ssd(u, us, hx, hbc, s0, wts, *, B, T, L, Ls):
    nseq = L // Ls
    n_chunks = T // Ls if nseq == 1 else 1
    nblk = B // nseq
    rb = lambda i, c: i * n_chunks + c
    if nseq > 1:
        assert Ls == SUBLANES
        h_specs = [pl.BlockSpec((nseq, SSD_CONV - 1, SSD_D_INNER), lambda i, c: (i, 0, 0)),
                   pl.BlockSpec((nseq, SSD_CONV - 1, SSD_BC), lambda i, c: (i, 0, SSD_D_INNER // SSD_BC))]
    else:
        h_specs = [pl.BlockSpec((1, SUBLANES, SSD_D_INNER), lambda i, c: (i, 0, 0)),
                   pl.BlockSpec((1, SUBLANES, SSD_BC), lambda i, c: (i, 0, 0))]
    const = lambda shape: pl.BlockSpec(shape, lambda i, c: (0,) * len(shape))
    kern = functools.partial(_ssd_kernel, L=L, Ls=Ls, n_chunks=n_chunks)
    return pl.pallas_call(
        kern,
        grid=(nblk, n_chunks),
        in_specs=[pl.BlockSpec((L, SSD_D_INNER), lambda i, c: (rb(i, c), U_Z)),
                  pl.BlockSpec((L, SSD_D_INNER), lambda i, c: (rb(i, c), U_X)),
                  pl.BlockSpec((L, SSD_BC), lambda i, c: (rb(i, c), U_BC)),
                  pl.BlockSpec((L, U_SMALL), lambda i, c: (rb(i, c), 0)),
                  *h_specs,
                  const((SSD_CONV, SSD_D_INNER)), const((SSD_CONV, SSD_BC)),
                  const((1, SSD_D_INNER)), const((1, SSD_BC)),
                  const((1, LANES)), const((1, LANES)),
                  const((1, SSD_D_INNER)), const((1, SSD_D_INNER)),
                  const((2 * LANES, SSD_D_INNER)),
                  pl.BlockSpec((nseq, SSD_D_INNER, SSD_STATE), lambda i, c: (i, 0, 0))],
        out_specs=[pl.BlockSpec((L, SSD_D_INNER), lambda i, c: (rb(i, c), 0)),
                   pl.BlockSpec((nseq, SSD_D_INNER, SSD_STATE), lambda i, c: (i, 0, 0))],
        out_shape=[jax.ShapeDtypeStruct((B * T, SSD_D_INNER), bf16),
                   jax.ShapeDtypeStruct((B, SSD_D_INNER, SSD_STATE), f32)],
        scratch_shapes=[pltpu.VMEM((SUBLANES, SSD_D_INNER), f32),
                        pltpu.VMEM((SUBLANES, SSD_BC), f32),
                        pltpu.VMEM((L, SSD_D_INNER), f32)],
        compiler_params=_params(2),
        name="ssd",
    )(u, u, u, us, hx, hbc, wts["cw_x"], wts["cw_bc"], wts["cb_x"], wts["cb_bc"],
      wts["dt_bias"], wts["a_log"], wts["d_exp"], wts["ssd_norm_w"], wts["ehp"], s0)


def _interleave(gens):
    while gens:
        alive = []
        for g in gens:
            try:
                next(g)
                alive.append(g)
            except StopIteration:
                pass
        gens = alive


def _group_views(refs, g, nseq):
    return [r.at[g] if kind == "tok" else r.at[pl.ds(g * nseq, nseq)] if kind == "state" else r
            for r, kind in refs]


def _mlstm_kernel(*refs, L, Ls, n_chunks, groups):
    kinds = ["tok"] * 5 + ["const"] * 3 + ["state"] * 3 + ["tok"] + ["state"] * 3
    _interleave([_mlstm_chunk(*_group_views(list(zip(refs, kinds)), g, L // Ls), L=L, Ls=Ls, n_chunks=n_chunks)
                 for g in range(groups)])


def _mlstm_chunk(q_ref, k_ref, v_ref, o_ref, sm_ref, ib_ref, fb_ref, nw_ref, c0_ref, n0_ref, m0_ref,
                 y_ref, c_ref, n_ref, m_ref, *, L, Ls, n_chunks):
    nseq = L // Ls
    c = pl.program_id(1)
    if n_chunks > 1:
        @pl.when(c == 0)
        def _():
            c_ref[...] = c0_ref[...]
            n_ref[...] = n0_ref[...]
            m_ref[...] = m0_ref[...]
        cprev_ref, nprev_ref, mprev_ref = c_ref, n_ref, m_ref
    else:
        cprev_ref, nprev_ref, mprev_ref = c0_ref, n0_ref, m0_ref

    per_tok = lambda a: jnp.broadcast_to(a, (nseq, Ls, LANES)).reshape(L, LANES)
    ig = sm_ref[:, :LANES] + ib_ref[...]
    fraw = sm_ref[:, LANES:] + fb_ref[...]
    lf = -_softplus(-fraw)
    causal, tri, _, last = _seq_masks(L, Ls)
    eye = _eye()
    F = _sel_left(tri, lf)
    FL = _sel_left(last, F)
    mp = per_tok(mprev_ref[:, 0:1, :])
    r_t = _transpose_exact(ig - F, eye)
    inter = F + mp
    lw = FL - F + ig
    segmax = jnp.max(lw.reshape(nseq, Ls, LANES), axis=1, keepdims=True)
    m_new = jnp.maximum(FL + mp, per_tok(segmax))
    sc = jnp.exp(lw - m_new)
    dec = jnp.exp(FL + mp - m_new)
    m_out = m_new.reshape(nseq, Ls, LANES)[:, 0:SUBLANES, :]
    yield

    shift = Ls.bit_length() - 1
    seq_of_row = _iota2((L, 1), 0) >> shift
    kscale = ML_QK_DIM ** -0.5
    heads = range(ML_HEADS)
    qcols = lambda h: slice(h * ML_QK_DIM, (h + 1) * ML_QK_DIM)
    vcols = lambda h: slice(h * ML_V_DIM, (h + 1) * ML_V_DIM)
    q = [q_ref[:, qcols(h)] for h in heads]
    k = [k_ref[:, qcols(h)] * kscale for h in heads]
    v = [v_ref[:, vcols(h)].astype(bf16) for h in heads]
    qk = [_dot_nt(q[h], k[h]) for h in heads]
    yield
    qc = []
    for h in heads:
        acc = None
        for j in range(nseq):
            qj = q[h] if nseq == 1 else jnp.where(seq_of_row == j, q[h], 0.0)
            t = _dot(qj, cprev_ref[j, qcols(h), :])
            acc = t if acc is None else acc + t
        qc.append(acc)
    yield
    rep = lambda col: jnp.broadcast_to(col, (L, LANES))
    twice = lambda a: jnp.concatenate([a, a], axis=1)
    qn = []
    for h in heads:
        acc = None
        for j in range(nseq):
            qj = q[h] if nseq == 1 else jnp.where(seq_of_row == j, q[h], 0.0)
            t = _dot_nt(qj, jnp.broadcast_to(nprev_ref[j, h:h + 1, :], (LANES, ML_QK_DIM)))
            acc = t if acc is None else acc + t
        qn.append(acc)
    yield
    dm, m_rep, inter_rep = [], [], []
    for h in heads:
        gl = GATE_LANE + h
        d = jnp.where(causal, F[:, gl:gl + 1] + r_t[gl:gl + 1, :], NEG)
        dm.append(d)
        inter_rep.append(rep(inter[:, gl:gl + 1]))
        m_rep.append(jnp.maximum(rep(jnp.max(d, axis=-1, keepdims=True)), inter_rep[h]))
    yield
    w = [jnp.exp(dm[h] - (m_rep[h] if L == LANES else m_rep[h][:, 0:1])) * qk[h] for h in heads]
    yield
    wv = [jnp.dot(w[h].astype(bf16), v[h], preferred_element_type=f32) for h in heads]
    wsum = [_rowsum_mxu(w[h]) for h in heads]
    yield
    hh = []
    for h in heads:
        wi = jnp.exp(inter_rep[h] - m_rep[h])
        den = wsum[h] + wi * qn[h]
        inv = 1.0 / jnp.maximum(jnp.abs(den), jnp.exp(-m_rep[h]))
        hh.append((wv[h] + twice(wi) * qc[h]) * twice(inv))
    yield
    ssq = [_rowsum_mxu(hh[h] * hh[h]) for h in heads]
    yield
    for h in heads:
        r = lax.rsqrt(ssq[h] * (1.0 / ML_V_DIM) + EPS)
        hn = (hh[h] * twice(r)) * nw_ref[:, vcols(h)]
        y_ref[:, vcols(h)] = (_sigmoid(o_ref[:, vcols(h)]) * hn).astype(bf16)
        if h % 2 == 1:
            yield
    for h in heads:
        gl = GATE_LANE + h
        ksc = k[h] * sc[:, gl:gl + 1]
        for j in range(nseq):
            r0 = j * Ls
            dj = dec[r0:r0 + 1, gl:gl + 1]
            kj = ksc if nseq == 1 else jnp.where(seq_of_row == j, ksc, 0.0)
            c_ref[j, qcols(h), :] = dj * cprev_ref[j, qcols(h), :] + _dot_tn(kj, v[h])
            n_ref[j, h:h + 1, :] = (dj * nprev_ref[j, h:h + 1, :]
                                    + jnp.sum(ksc[r0:r0 + Ls, :], axis=0, keepdims=True))
        if h % 2 == 1:
            yield
    m_ref[...] = m_out


def _scan_groups(nblk, nseq):
    return SCAN_GROUPS if nseq == 1 and nblk % SCAN_GROUPS == 0 else 1


def _mlstm(u, us, c0, n0, m0, wts, *, B, T, L, Ls):
    nseq = L // Ls
    n_chunks = T // Ls if nseq == 1 else 1
    nblk = B // nseq
    G = _scan_groups(nblk, nseq)
    rows = n_chunks * L
    u3, us3 = u.reshape(nblk, rows, U_MAIN), us.reshape(nblk, rows, U_SMALL)
    tok = lambda width, col: pl.BlockSpec((G, L, width), lambda i, c: (i, c, col))
    const = lambda shape: pl.BlockSpec(shape, lambda i, c: (0,) * len(shape))
    st = lambda shape: pl.BlockSpec((G * nseq,) + shape, lambda i, c: (i, 0, 0))
    kern = functools.partial(_mlstm_kernel, L=L, Ls=Ls, n_chunks=n_chunks, groups=G)
    y, c_new, n_new, m_new = pl.pallas_call(
        kern,
        grid=(nblk // G, n_chunks),
        in_specs=[tok(ML_QK_INNER, U_Q), tok(ML_QK_INNER, U_K), tok(ML_D_INNER, U_V), tok(ML_D_INNER, U_O),
                  tok(U_SMALL, 0),
                  const((1, LANES)), const((1, LANES)), const((1, ML_D_INNER)),
                  st((ML_QK_INNER, ML_V_DIM)), st((ML_HEADS, ML_QK_DIM)), st((SUBLANES, LANES))],
        out_specs=[tok(ML_D_INNER, 0),
                   st((ML_QK_INNER, ML_V_DIM)), st((ML_HEADS, ML_QK_DIM)), st((SUBLANES, LANES))],
        out_shape=[jax.ShapeDtypeStruct((nblk, rows, ML_D_INNER), bf16),
                   jax.ShapeDtypeStruct((B, ML_QK_INNER, ML_V_DIM), f32),
                   jax.ShapeDtypeStruct((B, ML_HEADS, ML_QK_DIM), f32),
                   jax.ShapeDtypeStruct((B, SUBLANES, LANES), f32)],
        compiler_params=_params(2),
        name="mlstm",
    )(u3, u3, u3, u3, us3, wts["i_bias"], wts["f_bias"], wts["ml_norm_w"], c0, n0, m0)
    return y.reshape(B * T, ML_D_INNER), c_new, n_new, m_new


def _outproj_cast_kernel(ys_ref, ym_ref, ws_ref, wm_ref, h_ref, o_ref, wsb_ref, wmb_ref):
    @pl.when(pl.program_id(1) == 0)
    def _():
        wsb_ref[...] = ws_ref[...].astype(bf16)
        wmb_ref[...] = wm_ref[...].astype(bf16)

    o_ref[...] = (h_ref[...]
                  + jnp.dot(ys_ref[...], wsb_ref[...], preferred_element_type=f32)
                  + jnp.dot(ym_ref[...], wmb_ref[...], preferred_element_type=f32))


def _outproj_kernel(ys_ref, ym_ref, ws_ref, wm_ref, h_ref, o_ref):
    o_ref[...] = (h_ref[...]
                  + jnp.dot(ys_ref[...], ws_ref[...], preferred_element_type=f32)
                  + jnp.dot(ym_ref[...], wm_ref[...], preferred_element_type=f32))


def _outproj(ys, ym, wts, h):
    M = h.shape[0]
    tm = _row_tile(M)
    if "w_out_s" in wts:
        tn = OUTPROJ_TILE
        h1 = pl.pallas_call(
            _outproj_kernel,
            grid=(M // tm, D_MODEL // tn),
            in_specs=[pl.BlockSpec((tm, SSD_D_INNER), lambda i, j: (i, 0)),
                      pl.BlockSpec((tm, ML_D_INNER), lambda i, j: (i, 0)),
                      pl.BlockSpec((SSD_D_INNER, tn), lambda i, j: (0, j)),
                      pl.BlockSpec((ML_D_INNER, tn), lambda i, j: (0, j)),
                      pl.BlockSpec((tm, tn), lambda i, j: (i, j))],
            out_specs=pl.BlockSpec((tm, tn), lambda i, j: (i, j)),
            out_shape=jax.ShapeDtypeStruct((M, D_MODEL), f32),
            compiler_params=_params(2),
            name="outproj",
        )(ys, ym, wts["w_out_s"], wts["w_out_m"], h)
        return h1, {}
    tn = OUTPROJ_CAST_TILE
    wspec = pl.BlockSpec((SSD_D_INNER, tn), lambda j, i: (0, j))
    wshape = jax.ShapeDtypeStruct((SSD_D_INNER, D_MODEL), bf16)
    h1, ws, wm = pl.pallas_call(
        _outproj_cast_kernel,
        grid=(D_MODEL // tn, M // tm),
        in_specs=[pl.BlockSpec((tm, SSD_D_INNER), lambda j, i: (i, 0)),
                  pl.BlockSpec((tm, ML_D_INNER), lambda j, i: (i, 0)),
                  wspec,
                  pl.BlockSpec((ML_D_INNER, tn), lambda j, i: (1, j)),
                  pl.BlockSpec((tm, tn), lambda j, i: (i, j))],
        out_specs=[pl.BlockSpec((tm, tn), lambda j, i: (i, j)), wspec, wspec],
        out_shape=[jax.ShapeDtypeStruct((M, D_MODEL), f32), wshape, wshape],
        compiler_params=_params(2),
        name="outproj_cast",
    )(ys, ym, wts["w_out"], wts["w_out"], h)
    return h1, dict(w_out_s=ws, w_out_m=wm)


def _ffn_kernel(h_ref, nw_ref, wg_ref, wv_ref, cwg_ref, cwv_ref, cbg_ref, cbv_ref, wd_ref, fw_ref, *rest,
                tm, tf, multi, blocks_per_seq, cast):
    if multi:
        hg_ref, hv_ref, y_ref, tg_ref, tv_ref, xn_ref = rest
    elif cast:
        hg_ref, hv_ref, y_ref, tg_ref, tv_ref, wgb_ref, wvb_ref, wdb_ref, xn_ref, carg_ref, carv_ref = rest
    else:
        hg_ref, hv_ref, y_ref, tg_ref, tv_ref, xn_ref, carg_ref, carv_ref = rest
    i = pl.program_id(0)
    j = pl.program_id(1)

    if cast:
        wgb_ref[...] = wg_ref[...].astype(bf16)
        wvb_ref[...] = wv_ref[...].astype(bf16)
        wdb_ref[...] = wd_ref[...].astype(bf16)
        wg_ref, wv_ref, wd_ref = wgb_ref, wvb_ref, wdb_ref
    subs = [slice(c0, c0 + MXU_COLS) for c0 in range(0, tf, MXU_COLS)]

    def tile(xn, first):
        up_dots = lambda cs: [jnp.dot(xn, w_ref[:, cs], preferred_element_type=f32) for w_ref in (wg_ref, wv_ref)]
        ups_next = up_dots(subs[0])
        for n, cs in enumerate(subs):
            ups = ups_next
            if n + 1 < len(subs):
                ups_next = up_dots(subs[n + 1])
            convd = []
            for half, (up, cw_ref, cb_ref) in enumerate(zip(ups, (cwg_ref, cwv_ref), (cbg_ref, cbv_ref))):
                if multi:
                    s_ref, t_ref = ((hg_ref, tg_ref), (hv_ref, tv_ref))[half]
                    prev = [s_ref[:, k:k + 1, cs] for k in range(FFN_CONV - 1)]
                    convd.append(_causal_conv_seqs(up, prev, cw_ref[:, cs], cb_ref[:, cs]))
                    up3 = up.reshape(tm // SUBLANES, SUBLANES, MXU_COLS)
                    t_ref[:, :, cs] = up3[:, SUBLANES - (FFN_CONV - 1):, :]
                else:
                    h_ref_, car_ref, t_ref = ((hg_ref, carg_ref, tg_ref), (hv_ref, carv_ref, tv_ref))[half]
                    tail = up[tm - SUBLANES:, :]
                    if blocks_per_seq == 1:
                        hist = h_ref_[0, :, cs]
                    else:
                        hist = jnp.where((i % blocks_per_seq) == 0, h_ref_[0, :, cs], car_ref[j, :, cs])
                        car_ref[j, :, cs] = tail
                    t_ref[0, :, cs] = tail
                    convd.append(_causal_conv(up, hist, cw_ref[:, cs], cb_ref[:, cs]))
            act = (_silu(convd[0]) * convd[1]).astype(bf16)
            down = jnp.dot(act, wd_ref[cs, :], preferred_element_type=f32)
            if first and n == 0:
                y_ref[...] = down
            else:
                y_ref[...] += down

    @pl.when(j == 0)
    def _():
        xn = _rms(h_ref[...], nw_ref[...]).astype(bf16)
        xn_ref[...] = xn
        tile(xn, True)

    last = pl.num_programs(1) - 1

    @pl.when((j > 0) & (j < last))
    def _():
        tile(xn_ref[...], False)

    @pl.when(j == last)
    def _():
        tile(xn_ref[...], False)
        y_ref[...] = _rms(h_ref[...] + y_ref[...], fw_ref[...])


def _ffn(h, ffn0, wts, *, B, T):
    M = h.shape[0]
    tm = _row_tile(M)
    tf = FFN_TILE
    n_ff = D_FF // tf
    multi = T < tm
    const = lambda shape: pl.BlockSpec(shape, lambda i, j: (0,) * len(shape))
    if multi:
        assert T == SUBLANES
        blocks_per_seq = 1
        nseq = tm // T
        hist = [ffn0, ffn0]
        h_specs = [pl.BlockSpec((nseq, FFN_CONV - 1, tf), lambda i, j: (i, 0, j)),
                   pl.BlockSpec((nseq, FFN_CONV - 1, tf), lambda i, j: (i, 0, n_ff + j))]
        t_specs = [pl.BlockSpec((nseq, FFN_CONV - 1, tf), lambda i, j: (i, 0, j))] * 2
        t_shapes = [jax.ShapeDtypeStruct((B, FFN_CONV - 1, D_FF), f32)] * 2
        scratch = []
    else:
        blocks_per_seq = T // tm
        pad = jnp.pad(ffn0, ((0, 0), (SUBLANES - (FFN_CONV - 1), 0), (0, 0)))
        hist = [pad, pad]
        h_specs = [pl.BlockSpec((1, SUBLANES, tf), lambda i, j: (i // blocks_per_seq, 0, j)),
                   pl.BlockSpec((1, SUBLANES, tf), lambda i, j: (i // blocks_per_seq, 0, n_ff + j))]
        t_specs = [pl.BlockSpec((1, SUBLANES, tf), lambda i, j: (i, 0, j))] * 2
        t_shapes = [jax.ShapeDtypeStruct((M // tm, SUBLANES, D_FF), f32)] * 2
        scratch = [pltpu.VMEM((n_ff, SUBLANES, tf), f32)] * 2
    cast = "w_up_g" not in wts
    assert not (cast and (multi or M != tm)), "the casting call must see every weight tile exactly once"
    up_spec = pl.BlockSpec((D_MODEL, tf), lambda i, j: (0, j))
    down_spec = pl.BlockSpec((tf, D_MODEL), lambda i, j: (j, 0))
    if cast:
        w_specs = [up_spec, pl.BlockSpec((D_MODEL, tf), lambda i, j: (0, n_ff + j)), down_spec]
        w_args = [wts["w_up"], wts["w_up"], wts["w_down"]]
        wb_specs = [up_spec, up_spec, down_spec]
        wb_shapes = [jax.ShapeDtypeStruct((D_MODEL, D_FF), bf16)] * 2 + [jax.ShapeDtypeStruct((D_FF, D_MODEL), bf16)]
    else:
        w_specs = [up_spec, up_spec, down_spec]
        w_args = [wts["w_up_g"], wts["w_up_v"], wts["w_down_b"]]
        wb_specs, wb_shapes = [], []
    kern = functools.partial(_ffn_kernel, tm=tm, tf=tf, multi=multi, blocks_per_seq=blocks_per_seq, cast=cast)
    y, *outs = pl.pallas_call(
        kern,
        grid=(M // tm, n_ff),
        in_specs=[pl.BlockSpec((tm, D_MODEL), lambda i, j: (i, 0), pipeline_mode=pl.Buffered(1)),
                  const((1, D_MODEL)),
                  w_specs[0], w_specs[1],
                  pl.BlockSpec((FFN_CONV, tf), lambda i, j: (0, j)),
                  pl.BlockSpec((FFN_CONV, tf), lambda i, j: (0, n_ff + j)),
                  pl.BlockSpec((1, tf), lambda i, j: (0, j)),
                  pl.BlockSpec((1, tf), lambda i, j: (0, n_ff + j)),
                  w_specs[2],
                  const((1, D_MODEL)),
                  *h_specs],
        out_specs=[pl.BlockSpec((tm, D_MODEL), lambda i, j: (i, 0)), *t_specs, *wb_specs],
        out_shape=[jax.ShapeDtypeStruct((M, D_MODEL), f32), *t_shapes, *wb_shapes],
        scratch_shapes=[pltpu.VMEM((tm, D_MODEL), bf16), *scratch],
        compiler_params=_params(2),
        name="ffn_cast" if cast else "ffn",
    )(h, wts["norm2_w"], w_args[0], w_args[1], wts["ffn_cw"], wts["ffn_cw"],
      wts["ffn_cb"], wts["ffn_cb"], w_args[2], wts["final_norm_w"], *hist)
    tails, new_w = outs[:2], {}
    if cast:
        new_w = dict(w_up_g=outs[2], w_up_v=outs[3], w_down_b=outs[4])
    if multi:
        return y, jnp.concatenate(tails, axis=-1), new_w
    last = jnp.concatenate(tails, axis=-1).reshape(B, blocks_per_seq, SUBLANES, 2 * D_FF)
    return y, last[:, blocks_per_seq - 1, SUBLANES - (FFN_CONV - 1):, :], new_w


def _hist_tile(state):
    return jnp.pad(state, ((0, 0), (SUBLANES - state.shape[1], 0), (0, 0)))


def _mixer(h, states, wts, *, B, T, L, Ls):
    conv0, s0, c0, n0, m0 = states
    u, us, cast_w = _inproj(h, wts)
    wts = {**wts, **cast_w}
    if L // Ls > 1:
        hx = hbc = conv0
    else:
        hx, hbc = _hist_tile(conv0[:, :, :SSD_D_INNER]), _hist_tile(conv0[:, :, SSD_D_INNER:])
    y_ssd, s_new = _ssd(u, us, hx, hbc, s0.reshape(B, SSD_D_INNER, SSD_STATE), wts, B=B, T=T, L=L, Ls=Ls)
    m_pad = jnp.broadcast_to(
        jnp.pad(m0, ((0, 0), (GATE_LANE, LANES - GATE_LANE - ML_HEADS)))[:, None, :], (B, SUBLANES, LANES))
    y_ml, c_new, n_new, m_new = _mlstm(u, us, c0.reshape(B, ML_QK_INNER, ML_V_DIM), n0, m_pad, wts,
                                       B=B, T=T, L=L, Ls=Ls)
    h1, cast_w = _outproj(y_ssd, y_ml, wts, h)
    wts = {**wts, **cast_w}
    ur = u.reshape(B, T, U_MAIN)[:, T - (SSD_CONV - 1):, :]
    conv_new = jnp.concatenate([ur[:, :, U_X * SSD_D_INNER:(U_X + 1) * SSD_D_INNER],
                                ur[:, :, U_BC * SSD_BC:(U_BC + 1) * SSD_BC]], axis=-1)
    new_states = (conv_new,
                  s_new.reshape(B, SSD_HEADS, SSD_HEAD_DIM, SSD_STATE),
                  c_new.reshape(B, ML_HEADS, ML_QK_DIM, ML_V_DIM),
                  n_new,
                  m_new[:, 0, GATE_LANE:GATE_LANE + ML_HEADS])
    return h1, new_states, wts


def _prep_weights(norm1_w, w_in, ssd_conv_w, ssd_conv_b, ssd_dt_bias, ssd_A_log, ssd_D, ssd_norm_w,
                  ml_i_bias, ml_f_bias, ml_norm_w, w_out, norm2_w, w_up, ffn_conv_w, ffn_conv_b, w_down,
                  final_norm_w):
    w_t = w_in.T
    o = 0
    rows, offs = {}, {}
    for name, width in (("z", SSD_D_INNER), ("x", SSD_D_INNER), ("bc", SSD_BC), ("dt", SSD_HEADS),
                        ("q", ML_QK_INNER), ("k", ML_QK_INNER), ("v", ML_D_INNER), ("i", ML_HEADS),
                        ("f", ML_HEADS), ("o", ML_D_INNER)):
        offs[name] = (o, width)
        if name in ("dt", "i", "f"):
            rows[name] = lax.optimization_barrier(w_t[o:o + width, :]).astype(bf16)
        o += width
    zpad = lambda n: jnp.zeros((n, D_MODEL), bf16)
    w_small = jnp.concatenate([rows["dt"], rows["i"], zpad(LANES - GATE_LANE - ML_HEADS),
                               zpad(GATE_LANE), rows["f"], zpad(LANES - GATE_LANE - ML_HEADS)], axis=0)
    lane_row = lambda v, off: jnp.pad(v.astype(f32), (off, LANES - off - v.shape[0]))[None, :]
    hp = jnp.arange(SSD_D_INNER) // SSD_HEAD_DIM
    ehp = (jnp.arange(LANES)[:, None] == hp[None, :]).astype(bf16)
    return dict(
        norm1_w=norm1_w[None, :], w_in_t=w_t, w_in_offs=offs, w_small=w_small,
        cw_x=ssd_conv_w[:, :SSD_D_INNER], cw_bc=ssd_conv_w[:, SSD_D_INNER:],
        cb_x=ssd_conv_b[None, :SSD_D_INNER], cb_bc=ssd_conv_b[None, SSD_D_INNER:],
        dt_bias=lane_row(ssd_dt_bias, 0), a_log=lane_row(ssd_A_log, 0),
        d_exp=jnp.repeat(ssd_D.astype(f32), SSD_HEAD_DIM)[None, :], ssd_norm_w=ssd_norm_w[None, :],
        ehp=jnp.concatenate([ehp, ehp], axis=0),
        i_bias=lane_row(ml_i_bias, GATE_LANE), f_bias=lane_row(ml_f_bias, GATE_LANE),
        ml_norm_w=ml_norm_w[None, :],
        w_out=w_out, norm2_w=norm2_w[None, :], w_up=w_up,
        ffn_cw=ffn_conv_w, ffn_cb=ffn_conv_b[None, :], w_down=w_down,
        final_norm_w=final_norm_w[None, :])


def kernel(x_prompt, x_sample, state_ssd_conv, state_ssd, state_mlstm_C, state_mlstm_n, state_mlstm_m,
           state_ffn_conv, meta_tokens, norm1_w, w_in, ssd_conv_w, ssd_conv_b, ssd_dt_bias, ssd_A_log,
           ssd_D, ssd_norm_w, ml_i_bias, ml_f_bias, ml_norm_w, w_out, norm2_w, w_up, ffn_conv_w,
           ffn_conv_b, w_down, final_norm_w):
    depth = w_in.shape[0]
    assert depth == 1, "single-layer step"
    Bp, Tp, _ = x_prompt.shape
    Bs, Ts, _ = x_sample.shape
    wts = _prep_weights(norm1_w[0], w_in[0], ssd_conv_w[0], ssd_conv_b[0], ssd_dt_bias[0], ssd_A_log[0],
                        ssd_D[0], ssd_norm_w[0], ml_i_bias[0], ml_f_bias[0], ml_norm_w[0], w_out[0],
                        norm2_w[0], w_up[0], ffn_conv_w[0], ffn_conv_b[0], w_down[0], final_norm_w)
    s_init = (state_ssd_conv[0], state_ssd[0], state_mlstm_C[0], state_mlstm_n[0], state_mlstm_m[0])
    hs, s_new, wts = _mixer(x_sample.reshape(Bs * Ts, D_MODEL), s_init, wts, B=Bs, T=Ts, L=8 * Ts, Ls=Ts)
    zero_states = (jnp.zeros((1, SSD_CONV - 1, SSD_CONV_DIM), f32),
                   jnp.zeros((1, SSD_HEADS, SSD_HEAD_DIM, SSD_STATE), f32),
                   jnp.zeros((1, ML_HEADS, ML_QK_DIM, ML_V_DIM), f32),
                   jnp.zeros((1, ML_HEADS, ML_QK_DIM), f32),
                   jnp.zeros((1, ML_HEADS), f32))
    hm, m_new, wts = _mixer(meta_tokens.astype(f32), zero_states, wts, B=1, T=N_META, L=N_META, Ls=N_META)
    _, m_ffn, cast_w = _ffn(hm, jnp.zeros((1, FFN_CONV - 1, 2 * D_FF), f32), wts, B=1, T=N_META)
    wts = {**wts, **cast_w}
    ys, s_ffn, _ = _ffn(hs, state_ffn_conv[0], wts, B=Bs, T=Ts)
    p_init = tuple(jnp.broadcast_to(s, (Bp,) + s.shape[1:]) for s in (*m_new, m_ffn))
    hp, p_new, wts = _mixer(x_prompt.reshape(Bp * Tp, D_MODEL), p_init[:5], wts, B=Bp, T=Tp, L=128, Ls=128)
    yp, p_ffn, _ = _ffn(hp, p_init[5], wts, B=Bp, T=Tp)
    return (yp.reshape(Bp, Tp, D_MODEL), ys.reshape(Bs, Ts, D_MODEL),
            *(s[None] for s in (*p_new, p_ffn)), *(s[None] for s in (*s_new, s_ffn)))
```

```python
import functools

import jax
import jax.numpy as jnp
from jax import lax
from jax.experimental import pallas as pl
from jax.experimental.pallas import tpu as pltpu

f32 = jnp.float32
bf16 = jnp.bfloat16

D_MODEL = 2048
N_META = 16
SSD_HEADS = 32
SSD_HEAD_DIM = 64
SSD_D_INNER = SSD_HEADS * SSD_HEAD_DIM
SSD_GROUPS = 2
SSD_STATE = 128
SSD_CONV = 4
SSD_BC = 2 * SSD_GROUPS * SSD_STATE
SSD_CONV_DIM = SSD_D_INNER + SSD_BC
ML_HEADS = 8
ML_QK_DIM = 128
ML_V_DIM = 256
ML_QK_INNER = ML_HEADS * ML_QK_DIM
ML_D_INNER = ML_HEADS * ML_V_DIM
D_FF = 5632
FFN_CONV = 3
EPS = 1e-6
NEG = -1e30

LANES = 128
SUBLANES = 8
VMEM_LIMIT = 56 * 1024 * 1024
MXU_COLS = 256
ROW_TILE = 1024
FFN_TILE = 512
INPROJ_TILE = 1536
OUTPROJ_TILE = 1024
OUTPROJ_CAST_TILE = 512
INPROJ_CAST_TILE = 512
SCAN_GROUPS = 2


def _row_tile(M):
    return ROW_TILE if M % ROW_TILE == 0 else M


U_Z, U_X, U_V, U_O = 0, 1, 2, 3
U_Q, U_K = 8, 9
U_BC = 20
U_MAIN = 4 * 2048 + 2 * 1024 + 512
GATE_LANE = 32
U_SMALL = 2 * LANES


def _dot(a, b):
    return jnp.dot(a.astype(bf16), b.astype(bf16), preferred_element_type=f32)


def _dot_nt(a, b):
    return lax.dot_general(a.astype(bf16), b.astype(bf16), (((1,), (1,)), ((), ())),
                           preferred_element_type=f32)


def _dot_tn(a, b):
    return lax.dot_general(a.astype(bf16), b.astype(bf16), (((0,), (0,)), ((), ())),
                           preferred_element_type=f32)


def _split3(a):
    hi = a.astype(bf16)
    r1 = a - hi.astype(f32)
    mid = r1.astype(bf16)
    lo = (r1 - mid.astype(f32)).astype(bf16)
    return hi, mid, lo


def _sel_right(a, e01):
    return jnp.dot(jnp.concatenate(_split3(a), axis=1), jnp.concatenate([e01] * 3, axis=0),
                   preferred_element_type=f32)


def _sel_left(e01, a):
    return jnp.dot(jnp.concatenate([e01] * 3, axis=1), jnp.concatenate(_split3(a), axis=0),
                   preferred_element_type=f32)


def _split2(a):
    hi = a.astype(bf16)
    mid = (a - hi.astype(f32)).astype(bf16)
    return jnp.concatenate([hi, mid], axis=1)


def _expand_heads(a, e01x2):
    return jnp.dot(_split2(a), e01x2, preferred_element_type=f32)


def _rowsum_mxu(a):
    return jnp.dot(_split2(a), jnp.ones((2 * a.shape[1], LANES), bf16), preferred_element_type=f32)


def _transpose_exact(a, eye):
    return lax.dot_general(jnp.concatenate([eye] * 3, axis=1), jnp.concatenate(_split3(a), axis=1),
                           (((1,), (1,)), ((), ())), preferred_element_type=f32)


def _iota2(shape, axis):
    return lax.broadcasted_iota(jnp.int32, shape, axis)


def _as01(m):
    return jnp.where(m, 1.0, 0.0).astype(bf16)


def _eye():
    return _as01(_iota2((LANES, LANES), 0) == _iota2((LANES, LANES), 1))


def _seq_masks(L, Ls):
    t = _iota2((L, L), 0)
    s = _iota2((L, L), 1)
    shift = Ls.bit_length() - 1
    same = (t >> shift) == (s >> shift)
    causal = same & (s <= t)
    causal_t = same & (t <= s)
    last = s == (t | (Ls - 1))
    return causal, _as01(causal), _as01(causal_t), _as01(last)


def _sigmoid(x):
    return 1.0 / (1.0 + jnp.exp(-x))


def _silu(x):
    return x * _sigmoid(x)


def _softplus(x):
    return jnp.maximum(x, 0.0) + jnp.log(1.0 + jnp.exp(-jnp.abs(x)))


def _rms(x, w):
    r = lax.rsqrt(jnp.mean(x * x, axis=-1, keepdims=True) + EPS)
    return (x * r) * w


def _causal_conv(x, hist, w, b):
    L, C = x.shape
    K = w.shape[0]
    r = _iota2((SUBLANES, C), 0)
    y = b + x * w[K - 1:K, :]
    for s in range(1, K):
        zt = jnp.where(r >= SUBLANES - s, hist, x[L - SUBLANES:, :])
        z = zt if L == SUBLANES else jnp.concatenate([x[:L - SUBLANES, :], zt], axis=0)
        y = y + pltpu.roll(z, s, 0) * w[K - 1 - s:K - s, :]
    return y


def _causal_conv_seqs(x, prev, w, b):
    L, C = x.shape
    K = w.shape[0]
    nseq = L // SUBLANES
    r = _iota2((L, C), 0) & (SUBLANES - 1)
    per_row = lambda a: jnp.broadcast_to(a, (nseq, SUBLANES, C)).reshape(L, C)
    y = b + x * w[K - 1:K, :]
    for s in range(1, K):
        head = per_row(prev[K - 1 - s])
        for rr in range(1, s):
            head = jnp.where(r == rr, per_row(prev[K - 1 - s + rr]), head)
        y = y + jnp.where(r >= s, pltpu.roll(x, s, 0), head) * w[K - 1 - s:K - s, :]
    return y


def _params(n_axes):
    return pltpu.CompilerParams(dimension_semantics=("arbitrary",) * n_axes,
                                vmem_limit_bytes=VMEM_LIMIT)


def _inproj_kernel(x_ref, nw_ref, w_ref, ws_ref, u_ref, us_ref, xn_ref):
    @pl.when(pl.program_id(1) == 0)
    def _():
        xn = _rms(x_ref[...], nw_ref[...]).astype(bf16)
        xn_ref[...] = xn
        us_ref[...] = _dot_nt(xn, ws_ref[...])
        u_ref[...] = _dot_nt(xn, w_ref[...])

    @pl.when(pl.program_id(1) > 0)
    def _():
        u_ref[...] = _dot_nt(xn_ref[...], w_ref[...])


def _inproj_cast_kernel(x_ref, nw_ref, w_ref, ws_ref, u_ref, us_ref, wb_ref, xn_ref):
    @pl.when(pl.program_id(1) == 0)
    def _():
        xn = _rms(x_ref[...], nw_ref[...]).astype(bf16)
        xn_ref[...] = xn
        us_ref[...] = _dot_nt(xn, ws_ref[...])

    wb_ref[...] = w_ref[...].astype(bf16)
    u_ref[...] = _dot_nt(xn_ref[...], wb_ref[...])


def _inproj(x, wts):
    M = x.shape[0]
    tm = _row_tile(M)
    const = lambda shape: pl.BlockSpec(shape, lambda i, j: (0,) * len(shape))
    out_shape = [jax.ShapeDtypeStruct((M, U_MAIN), f32), jax.ShapeDtypeStruct((M, U_SMALL), f32)]
    if "w_main" in wts:
        tn = INPROJ_TILE
        u, us = pl.pallas_call(
            _inproj_kernel,
            grid=(M // tm, U_MAIN // tn),
            in_specs=[pl.BlockSpec((tm, D_MODEL), lambda i, j: (i, 0)), const((1, D_MODEL)),
                      pl.BlockSpec((tn, D_MODEL), lambda i, j: (j, 0)), const((U_SMALL, D_MODEL))],
            out_specs=[pl.BlockSpec((tm, tn), lambda i, j: (i, j)),
                       pl.BlockSpec((tm, U_SMALL), lambda i, j: (i, 0))],
            out_shape=out_shape,
            scratch_shapes=[pltpu.VMEM((tm, D_MODEL), bf16)],
            compiler_params=_params(2),
            name="inproj",
        )(x, wts["norm1_w"], wts["w_main"], wts["w_small"])
        return u, us, {}
    assert M == tm, "the casting call must see every weight tile exactly once"
    tn = INPROJ_CAST_TILE
    bounds, dst = [], 0
    for name in ("z", "x", "v", "o", "q", "k", "bc"):
        src, width = wts["w_in_offs"][name]
        assert width % tn == 0 and dst % tn == 0
        bounds.append((dst // tn, src))
        dst += width
    unit = 2 * SUBLANES
    assert dst == U_MAIN and all(src % unit == 0 for _, src in bounds)

    def src_row(j):
        row = jnp.int32(0)
        for first_blk, src in bounds:
            row = jnp.where(j >= first_blk, src // unit + (j - first_blk) * (tn // unit), row)
        return row * unit

    u, us, w_main = pl.pallas_call(
        _inproj_cast_kernel,
        grid=(1, U_MAIN // tn),
        in_specs=[pl.BlockSpec((tm, D_MODEL), lambda i, j: (i, 0)), const((1, D_MODEL)),
                  pl.BlockSpec((pl.Element(tn), pl.Element(D_MODEL)), lambda i, j: (src_row(j), 0)),
                  const((U_SMALL, D_MODEL))],
        out_specs=[pl.BlockSpec((tm, tn), lambda i, j: (i, j)),
                   pl.BlockSpec((tm, U_SMALL), lambda i, j: (i, 0)),
                   pl.BlockSpec((tn, D_MODEL), lambda i, j: (j, 0))],
        out_shape=out_shape + [jax.ShapeDtypeStruct((U_MAIN, D_MODEL), bf16)],
        scratch_shapes=[pltpu.VMEM((tm, D_MODEL), bf16)],
        compiler_params=_params(2),
        name="inproj_cast",
    )(x, wts["norm1_w"], wts["w_in_t"], wts["w_small"])
    return u, us, dict(w_main=w_main)


def _ssd_kernel(z_ref, x_ref, bc_ref, sm_ref, hx_ref, hbc_ref, cwx_ref, cwbc_ref, cbx_ref, cbbc_ref,
                dtb_ref, alog_ref, dexp_ref, nw_ref, ehp_ref, s0_ref,
                y_ref, s_ref, tailx_ref, tailbc_ref, yz_ref, *, L, Ls, n_chunks):
    nseq = L // Ls
    c = pl.program_id(1)
    xpre = x_ref[...]
    bcpre = bc_ref[...]
    if nseq > 1:
        prev_x = [hx_ref[:, k:k + 1, :] for k in range(SSD_CONV - 1)]
        prev_bc = [hbc_ref[:, k:k + 1, :] for k in range(SSD_CONV - 1)]
        xc = _silu(_causal_conv_seqs(xpre, prev_x, cwx_ref[...], cbx_ref[...]))
        bcc = _silu(_causal_conv_seqs(bcpre, prev_bc, cwbc_ref[...], cbbc_ref[...]))
    else:
        if n_chunks == 1:
            hx, hbc = hx_ref[0], hbc_ref[0]
        else:
            first = c == 0
            hx = jnp.where(first, hx_ref[0], tailx_ref[...])
            hbc = jnp.where(first, hbc_ref[0], tailbc_ref[...])
        xc = _silu(_causal_conv(xpre, hx, cwx_ref[...], cbx_ref[...]))
        bcc = _silu(_causal_conv(bcpre, hbc, cwbc_ref[...], cbbc_ref[...]))
    if n_chunks > 1:
        tailx_ref[...] = xpre[L - SUBLANES:, :]
        tailbc_ref[...] = bcpre[L - SUBLANES:, :]

    lane = _iota2((L, LANES), 1)
    dt = jnp.where(lane < SSD_HEADS, _softplus(sm_ref[:, :LANES] + dtb_ref[...]), 0.0)
    dta = dt * (-jnp.exp(alog_ref[...]))

    causal, tri, tri_t, last = _seq_masks(L, Ls)
    eye = _eye()
    cum = _sel_left(tri, dta)
    cum_t = _sel_right(_transpose_exact(dta, eye), tri_t)
    cum_last = _sel_left(last, cum)
    ehp = ehp_ref[...]
    xdt = xc * _expand_heads(dt, ehp)
    xdtw = xdt * jnp.exp(_expand_heads(cum_last - cum, ehp))
    ecum = jnp.exp(_expand_heads(cum, ehp))

    if n_chunks > 1:
        @pl.when(c == 0)
        def _():
            s_ref[...] = s0_ref[...]
        sprev_ref = s_ref
    else:
        sprev_ref = s0_ref

    GE = SSD_D_INNER // SSD_GROUPS
    HPG = SSD_HEADS // SSD_GROUPS
    shift = Ls.bit_length() - 1
    seq_of_row = _iota2((L, 1), 0) >> shift
    lane_lo = lane < SSD_HEAD_DIM
    ys, intra = [], []
    for g in range(SSD_GROUPS):
        bm = bcc[:, g * SSD_STATE:(g + 1) * SSD_STATE]
        cm = bcc[:, (SSD_GROUPS + g) * SSD_STATE:(SSD_GROUPS + g + 1) * SSD_STATE]
        cb = _dot_nt(cm, bm)
        acc = None
        for j in range(nseq):
            cmj = cm if nseq == 1 else jnp.where(seq_of_row == j, cm, 0.0)
            t = _dot_nt(cmj, sprev_ref[j, g * GE:(g + 1) * GE, :])
            acc = t if acc is None else acc + t
        ys.append(acc)
        for hp in range(HPG // 2):
            col0 = g * GE + hp * LANES
            xpair = xdt[:, col0:col0 + LANES].astype(bf16)
            for e in range(2):
                h = g * HPG + hp * 2 + e
                seg = cum[:, h:h + 1] - cum_t[h:h + 1, :]
                m = jnp.exp(jnp.where(causal, seg, NEG)) * cb
                intra.append(jnp.dot(m.astype(bf16), xpair, preferred_element_type=f32))
    ssq = jnp.zeros((L, 1), f32)
    for g in range(SSD_GROUPS):
        for hp in range(HPG // 2):
            col0 = g * GE + hp * LANES
            blk = slice(col0, col0 + LANES)
            o0, o1 = intra[col0 // LANES * 2], intra[col0 // LANES * 2 + 1]
            y = (jnp.where(lane_lo, o0, o1)
                 + ecum[:, blk] * ys[g][:, hp * LANES:(hp + 1) * LANES]
                 + dexp_ref[:, blk] * xc[:, blk])
            yz = y * _silu(z_ref[:, blk])
            ssq = ssq + jnp.sum(yz * yz, axis=-1, keepdims=True)
            yz_ref[:, blk] = yz
    r = lax.rsqrt(ssq * (1.0 / SSD_D_INNER) + EPS)
    y_ref[...] = ((yz_ref[...] * r) * nw_ref[...]).astype(bf16)

    for j in range(nseq):
        tl = (j + 1) * Ls - 1
        dec = jnp.exp(jnp.broadcast_to(cum_t[:, tl:tl + 1], (LANES, LANES)))
        for g in range(SSD_GROUPS):
            bm = bcc[:, g * SSD_STATE:(g + 1) * SSD_STATE]
            xw = xdtw[:, g * GE:(g + 1) * GE]
            if nseq > 1:
                xw = jnp.where(seq_of_row == j, xw, 0.0)
            ds = _dot_tn(xw, bm)
            for e in range(HPG):
                h = g * HPG + e
                rs = slice(h * SSD_HEAD_DIM, (h + 1) * SSD_HEAD_DIM)
                s_ref[j, rs, :] = (dec[h:h + 1, :] * sprev_ref[j, rs, :]
                                   + ds[e * SSD_HEAD_DIM:(e + 1) * SSD_HEAD_DIM, :])


def _ssd(u, us, hx, hbc, s0, wts, *, B, T, L, Ls):
    nseq = L // Ls
    n_chunks = T // Ls if nseq == 1 else 1
    nblk = B // nseq
    rb = lambda i, c: i * n_chunks + c
    if nseq > 1:
        assert Ls == SUBLANES
        h_specs = [pl.BlockSpec((nseq, SSD_CONV - 1, SSD_D_INNER), lambda i, c: (i, 0, 0)),
                   pl.BlockSpec((nseq, SSD_CONV - 1, SSD_BC), lambda i, c: (i, 0, SSD_D_INNER // SSD_BC))]
    else:
        h_specs = [pl.BlockSpec((1, SUBLANES, SSD_D_INNER), lambda i, c: (i, 0, 0)),
                   pl.BlockSpec((1, SUBLANES, SSD_BC), lambda i, c: (i, 0, 0))]
    const = lambda shape: pl.BlockSpec(shape, lambda i, c: (0,) * len(shape))
    kern = functools.partial(_ssd_kernel, L=L, Ls=Ls, n_chunks=n_chunks)
    return pl.pallas_call(
        kern,
        grid=(nblk, n_chunks),
        in_specs=[pl.BlockSpec((L, SSD_D_INNER), lambda i, c: (rb(i, c), U_Z)),
                  pl.BlockSpec((L, SSD_D_INNER), lambda i, c: (rb(i, c), U_X)),
                  pl.BlockSpec((L, SSD_BC), lambda i, c: (rb(i, c), U_BC)),
                  pl.BlockSpec((L, U_SMALL), lambda i, c: (rb(i, c), 0)),
                  *h_specs,
                  const((SSD_CONV, SSD_D_INNER)), const((SSD_CONV, SSD_BC)),
                  const((1, SSD_D_INNER)), const((1, SSD_BC)),
                  const((1, LANES)), const((1, LANES)),
                  const((1, SSD_D_INNER)), const((1, SSD_D_INNER)),
                  const((2 * LANES, SSD_D_INNER)),
                  pl.BlockSpec((nseq, SSD_D_INNER, SSD_STATE), lambda i, c: (i, 0, 0))],
        out_specs=[pl.BlockSpec((L, SSD_D_INNER), lambda i, c: (rb(i, c), 0)),
                   pl.BlockSpec((nseq, SSD_D_INNER, SSD_STATE), lambda i, c: (i, 0, 0))],
        out_shape=[jax.ShapeDtypeStruct((B * T, SSD_D_INNER), bf16),
                   jax.ShapeDtypeStruct((B, SSD_D_INNER, SSD_STATE), f32)],
        scratch_shapes=[pltpu.VMEM((SUBLANES, SSD_D_INNER), f32),
                        pltpu.VMEM((SUBLANES, SSD_BC), f32),
                        pltpu.VMEM((L, SSD_D_INNER), f32)],
        compiler_params=_params(2),
        name="ssd",
    )(u, u, u, us, hx, hbc, wts["cw_x"], wts["cw_bc"], wts["cb_x"], wts["cb_bc"],
      wts["dt_bias"], wts["a_log"], wts["d_exp"], wts["ssd_norm_w"], wts["ehp"], s0)


def _interleave(gens):
    while gens:
        alive = []
        for g in gens:
            try:
                next(g)
                alive.append(g)
            except StopIteration:
                pass
        gens = alive


def _group_views(refs, g, nseq):
    return [r.at[g] if kind == "tok" else r.at[pl.ds(g * nseq, nseq)] if kind == "state" else r
            for r, kind in refs]


def _mlstm_kernel(*refs, L, Ls, n_chunks, groups):
    kinds = ["tok"] * 5 + ["const"] * 3 + ["state"] * 3 + ["tok"] + ["state"] * 3
    _interleave([_mlstm_chunk(*_group_views(list(zip(refs, kinds)), g, L // Ls), L=L, Ls=Ls, n_chunks=n_chunks)
                 for g in range(groups)])


def _mlstm_chunk(q_ref, k_ref, v_ref, o_ref, sm_ref, ib_ref, fb_ref, nw_ref, c0_ref, n0_ref, m0_ref,
                 y_ref, c_ref, n_ref, m_ref, *, L, Ls, n_chunks):
    nseq = L // Ls
    c = pl.program_id(1)
    if n_chunks > 1:
        @pl.when(c == 0)
        def _():
            c_ref[...] = c0_ref[...]
            n_ref[...] = n0_ref[...]
            m_ref[...] = m0_ref[...]
        cprev_ref, nprev_ref, mprev_ref = c_ref, n_ref, m_ref
    else:
        cprev_ref, nprev_ref, mprev_ref = c0_ref, n0_ref, m0_ref

    per_tok = lambda a: jnp.broadcast_to(a, (nseq, Ls, LANES)).reshape(L, LANES)
    ig = sm_ref[:, :LANES] + ib_ref[...]
    fraw = sm_ref[:, LANES:] + fb_ref[...]
    lf = -_softplus(-fraw)
    causal, tri, _, last = _seq_masks(L, Ls)
    eye = _eye()
    F = _sel_left(tri, lf)
    FL = _sel_left(last, F)
    mp = per_tok(mprev_ref[:, 0:1, :])
    r_t = _transpose_exact(ig - F, eye)
    inter = F + mp
    lw = FL - F + ig
    segmax = jnp.max(lw.reshape(nseq, Ls, LANES), axis=1, keepdims=True)
    m_new = jnp.maximum(FL + mp, per_tok(segmax))
    sc = jnp.exp(lw - m_new)
    dec = jnp.exp(FL + mp - m_new)
    m_out = m_new.reshape(nseq, Ls, LANES)[:, 0:SUBLANES, :]
    yield

    shift = Ls.bit_length() - 1
    seq_of_row = _iota2((L, 1), 0) >> shift
    kscale = ML_QK_DIM ** -0.5
    heads = range(ML_HEADS)
    qcols = lambda h: slice(h * ML_QK_DIM, (h + 1) * ML_QK_DIM)
    vcols = lambda h: slice(h * ML_V_DIM, (h + 1) * ML_V_DIM)
    q = [q_ref[:, qcols(h)] for h in heads]
    k = [k_ref[:, qcols(h)] * kscale for h in heads]
    v = [v_ref[:, vcols(h)].astype(bf16) for h in heads]
    qk = [_dot_nt(q[h], k[h]) for h in heads]
    yield
    qc = []
    for h in heads:
        acc = None
        for j in range(nseq):
            qj = q[h] if nseq == 1 else jnp.where(seq_of_row == j, q[h], 0.0)
            t = _dot(qj, cprev_ref[j, qcols(h), :])
            acc = t if acc is None else acc + t
        qc.append(acc)
    yield
    rep = lambda col: jnp.broadcast_to(col, (L, LANES))
    twice = lambda a: jnp.concatenate([a, a], axis=1)
    qn = []
    for h in heads:
        acc = None
        for j in range(nseq):
            qj = q[h] if nseq == 1 else jnp.where(seq_of_row == j, q[h], 0.0)
            t = _dot_nt(qj, jnp.broadcast_to(nprev_ref[j, h:h + 1, :], (LANES, ML_QK_DIM)))
            acc = t if acc is None else acc + t
        qn.append(acc)
    yield
    dm, m_rep, inter_rep = [], [], []
    for h in heads:
        gl = GATE_LANE + h
        d = jnp.where(causal, F[:, gl:gl + 1] + r_t[gl:gl + 1, :], NEG)
        dm.append(d)
        inter_rep.append(rep(inter[:, gl:gl + 1]))
        m_rep.append(jnp.maximum(rep(jnp.max(d, axis=-1, keepdims=True)), inter_rep[h]))
    yield
    w = [jnp.exp(dm[h] - (m_rep[h] if L == LANES else m_rep[h][:, 0:1])) * qk[h] for h in heads]
    yield
    wv = [jnp.dot(w[h].astype(bf16), v[h], preferred_element_type=f32) for h in heads]
    wsum = [_rowsum_mxu(w[h]) for h in heads]
    yield
    hh = []
    for h in heads:
        wi = jnp.exp(inter_rep[h] - m_rep[h])
        den = wsum[h] + wi * qn[h]
        inv = 1.0 / jnp.maximum(jnp.abs(den), jnp.exp(-m_rep[h]))
        hh.append((wv[h] + twice(wi) * qc[h]) * twice(inv))
    yield
    ssq = [_rowsum_mxu(hh[h] * hh[h]) for h in heads]
    yield
    for h in heads:
        r = lax.rsqrt(ssq[h] * (1.0 / ML_V_DIM) + EPS)
        hn = (hh[h] * twice(r)) * nw_ref[:, vcols(h)]
        y_ref[:, vcols(h)] = (_sigmoid(o_ref[:, vcols(h)]) * hn).astype(bf16)
        if h % 2 == 1:
            yield
    for h in heads:
        gl = GATE_LANE + h
        ksc = k[h] * sc[:, gl:gl + 1]
        for j in range(nseq):
            r0 = j * Ls
            dj = dec[r0:r0 + 1, gl:gl + 1]
            kj = ksc if nseq == 1 else jnp.where(seq_of_row == j, ksc, 0.0)
            c_ref[j, qcols(h), :] = dj * cprev_ref[j, qcols(h), :] + _dot_tn(kj, v[h])
            n_ref[j, h:h + 1, :] = (dj * nprev_ref[j, h:h + 1, :]
                                    + jnp.sum(ksc[r0:r0 + Ls, :], axis=0, keepdims=True))
        if h % 2 == 1:
            yield
    m_ref[...] = m_out


def _scan_groups(nblk, nseq):
    return SCAN_GROUPS if nseq == 1 and nblk % SCAN_GROUPS == 0 else 1


def _mlstm(u, us, c0, n0, m0, wts, *, B, T, L, Ls):
    nseq = L // Ls
    n_chunks = T // Ls if nseq == 1 else 1
    nblk = B // nseq
    G = _scan_groups(nblk, nseq)
    rows = n_chunks * L
    u3, us3 = u.reshape(nblk, rows, U_MAIN), us.reshape(nblk, rows, U_SMALL)
    tok = lambda width, col: pl.BlockSpec((G, L, width), lambda i, c: (i, c, col))
    const = lambda shape: pl.BlockSpec(shape, lambda i, c: (0,) * len(shape))
    st = lambda shape: pl.BlockSpec((G * nseq,) + shape, lambda i, c: (i, 0, 0))
    kern = functools.partial(_mlstm_kernel, L=L, Ls=Ls, n_chunks=n_chunks, groups=G)
    y, c_new, n_new, m_new = pl.pallas_call(
        kern,
        grid=(nblk // G, n_chunks),
        in_specs=[tok(ML_QK_INNER, U_Q), tok(ML_QK_INNER, U_K), tok(ML_D_INNER, U_V), tok(ML_D_INNER, U_O),
                  tok(U_SMALL, 0),
                  const((1, LANES)), const((1, LANES)), const((1, ML_D_INNER)),
                  st((ML_QK_INNER, ML_V_DIM)), st((ML_HEADS, ML_QK_DIM)), st((SUBLANES, LANES))],
        out_specs=[tok(ML_D_INNER, 0),
                   st((ML_QK_INNER, ML_V_DIM)), st((ML_HEADS, ML_QK_DIM)), st((SUBLANES, LANES))],
        out_shape=[jax.ShapeDtypeStruct((nblk, rows, ML_D_INNER), bf16),
                   jax.ShapeDtypeStruct((B, ML_QK_INNER, ML_V_DIM), f32),
                   jax.ShapeDtypeStruct((B, ML_HEADS, ML_QK_DIM), f32),
                   jax.ShapeDtypeStruct((B, SUBLANES, LANES), f32)],
        compiler_params=_params(2),
        name="mlstm",
    )(u3, u3, u3, u3, us3, wts["i_bias"], wts["f_bias"], wts["ml_norm_w"], c0, n0, m0)
    return y.reshape(B * T, ML_D_INNER), c_new, n_new, m_new


def _outproj_cast_kernel(ys_ref, ym_ref, ws_ref, wm_ref, h_ref, o_ref, wsb_ref, wmb_ref):
    @pl.when(pl.program_id(1) == 0)
    def _():
        wsb_ref[...] = ws_ref[...].astype(bf16)
        wmb_ref[...] = wm_ref[...].astype(bf16)

    o_ref[...] = (h_ref[...]
                  + jnp.dot(ys_ref[...], wsb_ref[...], preferred_element_type=f32)
                  + jnp.dot(ym_ref[...], wmb_ref[...], preferred_element_type=f32))


def _outproj_kernel(ys_ref, ym_ref, ws_ref, wm_ref, h_ref, o_ref):
    o_ref[...] = (h_ref[...]
                  + jnp.dot(ys_ref[...], ws_ref[...], preferred_element_type=f32)
                  + jnp.dot(ym_ref[...], wm_ref[...], preferred_element_type=f32))


def _outproj(ys, ym, wts, h):
    M = h.shape[0]
    tm = _row_tile(M)
    if "w_out_s" in wts:
        tn = OUTPROJ_TILE
        h1 = pl.pallas_call(
            _outproj_kernel,
            grid=(M // tm, D_MODEL // tn),
            in_specs=[pl.BlockSpec((tm, SSD_D_INNER), lambda i, j: (i, 0)),
                      pl.BlockSpec((tm, ML_D_INNER), lambda i, j: (i, 0)),
                      pl.BlockSpec((SSD_D_INNER, tn), lambda i, j: (0, j)),
                      pl.BlockSpec((ML_D_INNER, tn), lambda i, j: (0, j)),
                      pl.BlockSpec((tm, tn), lambda i, j: (i, j))],
            out_specs=pl.BlockSpec((tm, tn), lambda i, j: (i, j)),
            out_shape=jax.ShapeDtypeStruct((M, D_MODEL), f32),
            compiler_params=_params(2),
            name="outproj",
        )(ys, ym, wts["w_out_s"], wts["w_out_m"], h)
        return h1, {}
    tn = OUTPROJ_CAST_TILE
    wspec = pl.BlockSpec((SSD_D_INNER, tn), lambda j, i: (0, j))
    wshape = jax.ShapeDtypeStruct((SSD_D_INNER, D_MODEL), bf16)
    h1, ws, wm = pl.pallas_call(
        _outproj_cast_kernel,
        grid=(D_MODEL // tn, M // tm),
        in_specs=[pl.BlockSpec((tm, SSD_D_INNER), lambda j, i: (i, 0)),
                  pl.BlockSpec((tm, ML_D_INNER), lambda j, i: (i, 0)),
                  wspec,
                  pl.BlockSpec((ML_D_INNER, tn), lambda j, i: (1, j)),
                  pl.BlockSpec((tm, tn), lambda j, i: (i, j))],
        out_specs=[pl.BlockSpec((tm, tn), lambda j, i: (i, j)), wspec, wspec],
        out_shape=[jax.ShapeDtypeStruct((M, D_MODEL), f32), wshape, wshape],
        compiler_params=_params(2),
        name="outproj_cast",
    )(ys, ym, wts["w_out"], wts["w_out"], h)
    return h1, dict(w_out_s=ws, w_out_m=wm)


def _ffn_kernel(h_ref, nw_ref, wg_ref, wv_ref, cwg_ref, cwv_ref, cbg_ref, cbv_ref, wd_ref, fw_ref, *rest,
                tm, tf, multi, blocks_per_seq, cast):
    if multi:
        hg_ref, hv_ref, y_ref, tg_ref, tv_ref, xn_ref = rest
    elif cast:
        hg_ref, hv_ref, y_ref, tg_ref, tv_ref, wgb_ref, wvb_ref, wdb_ref, xn_ref, carg_ref, carv_ref = rest
    else:
        hg_ref, hv_ref, y_ref, tg_ref, tv_ref, xn_ref, carg_ref, carv_ref = rest
    i = pl.program_id(0)
    j = pl.program_id(1)

    if cast:
        wgb_ref[...] = wg_ref[...].astype(bf16)
        wvb_ref[...] = wv_ref[...].astype(bf16)
        wdb_ref[...] = wd_ref[...].astype(bf16)
        wg_ref, wv_ref, wd_ref = wgb_ref, wvb_ref, wdb_ref
    subs = [slice(c0, c0 + MXU_COLS) for c0 in range(0, tf, MXU_COLS)]

    def tile(xn, first):
        up_dots = lambda cs: [jnp.dot(xn, w_ref[:, cs], preferred_element_type=f32) for w_ref in (wg_ref, wv_ref)]
        ups_next = up_dots(subs[0])
        for n, cs in enumerate(subs):
            ups = ups_next
            if n + 1 < len(subs):
                ups_next = up_dots(subs[n + 1])
            convd = []
            for half, (up, cw_ref, cb_ref) in enumerate(zip(ups, (cwg_ref, cwv_ref), (cbg_ref, cbv_ref))):
                if multi:
                    s_ref, t_ref = ((hg_ref, tg_ref), (hv_ref, tv_ref))[half]
                    prev = [s_ref[:, k:k + 1, cs] for k in range(FFN_CONV - 1)]
                    convd.append(_causal_conv_seqs(up, prev, cw_ref[:, cs], cb_ref[:, cs]))
                    up3 = up.reshape(tm // SUBLANES, SUBLANES, MXU_COLS)
                    t_ref[:, :, cs] = up3[:, SUBLANES - (FFN_CONV - 1):, :]
                else:
                    h_ref_, car_ref, t_ref = ((hg_ref, carg_ref, tg_ref), (hv_ref, carv_ref, tv_ref))[half]
                    tail = up[tm - SUBLANES:, :]
                    if blocks_per_seq == 1:
                        hist = h_ref_[0, :, cs]
                    else:
                        hist = jnp.where((i % blocks_per_seq) == 0, h_ref_[0, :, cs], car_ref[j, :, cs])
                        car_ref[j, :, cs] = tail
                    t_ref[0, :, cs] = tail
                    convd.append(_causal_conv(up, hist, cw_ref[:, cs], cb_ref[:, cs]))
            act = (_silu(convd[0]) * convd[1]).astype(bf16)
            down = jnp.dot(act, wd_ref[cs, :], preferred_element_type=f32)
            if first and n == 0:
                y_ref[...] = down
            else:
                y_ref[...] += down

    @pl.when(j == 0)
    def _():
        xn = _rms(h_ref[...], nw_ref[...]).astype(bf16)
        xn_ref[...] = xn
        tile(xn, True)

    last = pl.num_programs(1) - 1

    @pl.when((j > 0) & (j < last))
    def _():
        tile(xn_ref[...], False)

    @pl.when(j == last)
    def _():
        tile(xn_ref[...], False)
        y_ref[...] = _rms(h_ref[...] + y_ref[...], fw_ref[...])


def _ffn(h, ffn0, wts, *, B, T):
    M = h.shape[0]
    tm = _row_tile(M)
    tf = FFN_TILE
    n_ff = D_FF // tf
    multi = T < tm
    const = lambda shape: pl.BlockSpec(shape, lambda i, j: (0,) * len(shape))
    if multi:
        assert T == SUBLANES
        blocks_per_seq = 1
        nseq = tm // T
        hist = [ffn0, ffn0]
        h_specs = [pl.BlockSpec((nseq, FFN_CONV - 1, tf), lambda i, j: (i, 0, j)),
                   pl.BlockSpec((nseq, FFN_CONV - 1, tf), lambda i, j: (i, 0, n_ff + j))]
        t_specs = [pl.BlockSpec((nseq, FFN_CONV - 1, tf), lambda i, j: (i, 0, j))] * 2
        t_shapes = [jax.ShapeDtypeStruct((B, FFN_CONV - 1, D_FF), f32)] * 2
        scratch = []
    else:
        blocks_per_seq = T // tm
        pad = jnp.pad(ffn0, ((0, 0), (SUBLANES - (FFN_CONV - 1), 0), (0, 0)))
        hist = [pad, pad]
        h_specs = [pl.BlockSpec((1, SUBLANES, tf), lambda i, j: (i // blocks_per_seq, 0, j)),
                   pl.BlockSpec((1, SUBLANES, tf), lambda i, j: (i // blocks_per_seq, 0, n_ff + j))]
        t_specs = [pl.BlockSpec((1, SUBLANES, tf), lambda i, j: (i, 0, j))] * 2
        t_shapes = [jax.ShapeDtypeStruct((M // tm, SUBLANES, D_FF), f32)] * 2
        scratch = [pltpu.VMEM((n_ff, SUBLANES, tf), f32)] * 2
    cast = "w_up_g" not in wts
    assert not (cast and (multi or M != tm)), "the casting call must see every weight tile exactly once"
    up_spec = pl.BlockSpec((D_MODEL, tf), lambda i, j: (0, j))
    down_spec = pl.BlockSpec((tf, D_MODEL), lambda i, j: (j, 0))
    if cast:
        w_specs = [up_spec, pl.BlockSpec((D_MODEL, tf), lambda i, j: (0, n_ff + j)), down_spec]
        w_args = [wts["w_up"], wts["w_up"], wts["w_down"]]
        wb_specs = [up_spec, up_spec, down_spec]
        wb_shapes = [jax.ShapeDtypeStruct((D_MODEL, D_FF), bf16)] * 2 + [jax.ShapeDtypeStruct((D_FF, D_MODEL), bf16)]
    else:
        w_specs = [up_spec, up_spec, down_spec]
        w_args = [wts["w_up_g"], wts["w_up_v"], wts["w_down_b"]]
        wb_specs, wb_shapes = [], []
    kern = functools.partial(_ffn_kernel, tm=tm, tf=tf, multi=multi, blocks_per_seq=blocks_per_seq, cast=cast)
    y, *outs = pl.pallas_call(
        kern,
        grid=(M // tm, n_ff),
        in_specs=[pl.BlockSpec((tm, D_MODEL), lambda i, j: (i, 0), pipeline_mode=pl.Buffered(1 if multi else 2)),
                  const((1, D_MODEL)),
                  w_specs[0], w_specs[1],
                  pl.BlockSpec((FFN_CONV, tf), lambda i, j: (0, j)),
                  pl.BlockSpec((FFN_CONV, tf), lambda i, j: (0, n_ff + j)),
                  pl.BlockSpec((1, tf), lambda i, j: (0, j)),
                  pl.BlockSpec((1, tf), lambda i, j: (0, n_ff + j)),
                  w_specs[2],
                  const((1, D_MODEL)),
                  *h_specs],
        out_specs=[pl.BlockSpec((tm, D_MODEL), lambda i, j: (i, 0)), *t_specs, *wb_specs],
        out_shape=[jax.ShapeDtypeStruct((M, D_MODEL), f32), *t_shapes, *wb_shapes],
        scratch_shapes=[pltpu.VMEM((tm, D_MODEL), bf16), *scratch],
        compiler_params=_params(2),
        name="ffn_cast" if cast else "ffn",
    )(h, wts["norm2_w"], w_args[0], w_args[1], wts["ffn_cw"], wts["ffn_cw"],
      wts["ffn_cb"], wts["ffn_cb"], w_args[2], wts["final_norm_w"], *hist)
    tails, new_w = outs[:2], {}
    if cast:
        new_w = dict(w_up_g=outs[2], w_up_v=outs[3], w_down_b=outs[4])
    if multi:
        return y, jnp.concatenate(tails, axis=-1), new_w
    last = jnp.concatenate(tails, axis=-1).reshape(B, blocks_per_seq, SUBLANES, 2 * D_FF)
    return y, last[:, blocks_per_seq - 1, SUBLANES - (FFN_CONV - 1):, :], new_w


def _hist_tile(state):
    return jnp.pad(state, ((0, 0), (SUBLANES - state.shape[1], 0), (0, 0)))


def _mixer(h, states, wts, *, B, T, L, Ls):
    conv0, s0, c0, n0, m0 = states
    u, us, cast_w = _inproj(h, wts)
    wts = {**wts, **cast_w}
    if L // Ls > 1:
        hx = hbc = conv0
    else:
        hx, hbc = _hist_tile(conv0[:, :, :SSD_D_INNER]), _hist_tile(conv0[:, :, SSD_D_INNER:])
    y_ssd, s_new = _ssd(u, us, hx, hbc, s0.reshape(B, SSD_D_INNER, SSD_STATE), wts, B=B, T=T, L=L, Ls=Ls)
    m_pad = jnp.broadcast_to(
        jnp.pad(m0, ((0, 0), (GATE_LANE, LANES - GATE_LANE - ML_HEADS)))[:, None, :], (B, SUBLANES, LANES))
    y_ml, c_new, n_new, m_new = _mlstm(u, us, c0.reshape(B, ML_QK_INNER, ML_V_DIM), n0, m_pad, wts,
                                       B=B, T=T, L=L, Ls=Ls)
    h1, cast_w = _outproj(y_ssd, y_ml, wts, h)
    wts = {**wts, **cast_w}
    ur = u.reshape(B, T, U_MAIN)[:, T - (SSD_CONV - 1):, :]
    conv_new = jnp.concatenate([ur[:, :, U_X * SSD_D_INNER:(U_X + 1) * SSD_D_INNER],
                                ur[:, :, U_BC * SSD_BC:(U_BC + 1) * SSD_BC]], axis=-1)
    new_states = (conv_new,
                  s_new.reshape(B, SSD_HEADS, SSD_HEAD_DIM, SSD_STATE),
                  c_new.reshape(B, ML_HEADS, ML_QK_DIM, ML_V_DIM),
                  n_new,
                  m_new[:, 0, GATE_LANE:GATE_LANE + ML_HEADS])
    return h1, new_states, wts


def _prep_weights(norm1_w, w_in, ssd_conv_w, ssd_conv_b, ssd_dt_bias, ssd_A_log, ssd_D, ssd_norm_w,
                  ml_i_bias, ml_f_bias, ml_norm_w, w_out, norm2_w, w_up, ffn_conv_w, ffn_conv_b, w_down,
                  final_norm_w):
    w_t = w_in.T
    o = 0
    rows, offs = {}, {}
    for name, width in (("z", SSD_D_INNER), ("x", SSD_D_INNER), ("bc", SSD_BC), ("dt", SSD_HEADS),
                        ("q", ML_QK_INNER), ("k", ML_QK_INNER), ("v", ML_D_INNER), ("i", ML_HEADS),
                        ("f", ML_HEADS), ("o", ML_D_INNER)):
        offs[name] = (o, width)
        if name in ("dt", "i", "f"):
            rows[name] = lax.optimization_barrier(w_t[o:o + width, :]).astype(bf16)
        o += width
    zpad = lambda n: jnp.zeros((n, D_MODEL), bf16)
    w_small = jnp.concatenate([rows["dt"], rows["i"], zpad(LANES - GATE_LANE - ML_HEADS),
                               zpad(GATE_LANE), rows["f"], zpad(LANES - GATE_LANE - ML_HEADS)], axis=0)
    lane_row = lambda v, off: jnp.pad(v.astype(f32), (off, LANES - off - v.shape[0]))[None, :]
    hp = jnp.arange(SSD_D_INNER) // SSD_HEAD_DIM
    ehp = (jnp.arange(LANES)[:, None] == hp[None, :]).astype(bf16)
    return dict(
        norm1_w=norm1_w[None, :], w_in_t=w_t, w_in_offs=offs, w_small=w_small,
        cw_x=ssd_conv_w[:, :SSD_D_INNER], cw_bc=ssd_conv_w[:, SSD_D_INNER:],
        cb_x=ssd_conv_b[None, :SSD_D_INNER], cb_bc=ssd_conv_b[None, SSD_D_INNER:],
        dt_bias=lane_row(ssd_dt_bias, 0), a_log=lane_row(ssd_A_log, 0),
        d_exp=jnp.repeat(ssd_D.astype(f32), SSD_HEAD_DIM)[None, :], ssd_norm_w=ssd_norm_w[None, :],
        ehp=jnp.concatenate([ehp, ehp], axis=0),
        i_bias=lane_row(ml_i_bias, GATE_LANE), f_bias=lane_row(ml_f_bias, GATE_LANE),
        ml_norm_w=ml_norm_w[None, :],
        w_out=w_out, norm2_w=norm2_w[None, :], w_up=w_up,
        ffn_cw=ffn_conv_w, ffn_cb=ffn_conv_b[None, :], w_down=w_down,
        final_norm_w=final_norm_w[None, :])


def kernel(x_prompt, x_sample, state_ssd_conv, state_ssd, state_mlstm_C, state_mlstm_n, state_mlstm_m,
           state_ffn_conv, meta_tokens, norm1_w, w_in, ssd_conv_w, ssd_conv_b, ssd_dt_bias, ssd_A_log,
           ssd_D, ssd_norm_w, ml_i_bias, ml_f_bias, ml_norm_w, w_out, norm2_w, w_up, ffn_conv_w,
           ffn_conv_b, w_down, final_norm_w):
    depth = w_in.shape[0]
    assert depth == 1, "single-layer step"
    Bp, Tp, _ = x_prompt.shape
    Bs, Ts, _ = x_sample.shape
    wts = _prep_weights(norm1_w[0], w_in[0], ssd_conv_w[0], ssd_conv_b[0], ssd_dt_bias[0], ssd_A_log[0],
                        ssd_D[0], ssd_norm_w[0], ml_i_bias[0], ml_f_bias[0], ml_norm_w[0], w_out[0],
                        norm2_w[0], w_up[0], ffn_conv_w[0], ffn_conv_b[0], w_down[0], final_norm_w)
    s_init = (state_ssd_conv[0], state_ssd[0], state_mlstm_C[0], state_mlstm_n[0], state_mlstm_m[0])
    hs, s_new, wts = _mixer(x_sample.reshape(Bs * Ts, D_MODEL), s_init, wts, B=Bs, T=Ts, L=8 * Ts, Ls=Ts)
    zero_states = (jnp.zeros((1, SSD_CONV - 1, SSD_CONV_DIM), f32),
                   jnp.zeros((1, SSD_HEADS, SSD_HEAD_DIM, SSD_STATE), f32),
                   jnp.zeros((1, ML_HEADS, ML_QK_DIM, ML_V_DIM), f32),
                   jnp.zeros((1, ML_HEADS, ML_QK_DIM), f32),
                   jnp.zeros((1, ML_HEADS), f32))
    hm, m_new, wts = _mixer(meta_tokens.astype(f32), zero_states, wts, B=1, T=N_META, L=N_META, Ls=N_META)
    _, m_ffn, cast_w = _ffn(hm, jnp.zeros((1, FFN_CONV - 1, 2 * D_FF), f32), wts, B=1, T=N_META)
    wts = {**wts, **cast_w}
    ys, s_ffn, _ = _ffn(hs, state_ffn_conv[0], wts, B=Bs, T=Ts)
    p_init = tuple(jnp.broadcast_to(s, (Bp,) + s.shape[1:]) for s in (*m_new, m_ffn))
    hp, p_new, wts = _mixer(x_prompt.reshape(Bp * Tp, D_MODEL), p_init[:5], wts, B=Bp, T=Tp, L=128, Ls=128)
    yp, p_ffn, _ = _ffn(hp, p_init[5], wts, B=Bp, T=Tp)
    return (yp.reshape(Bp, Tp, D_MODEL), ys.reshape(Bs, Ts, D_MODEL),
            *(s[None] for s in (*p_new, p_ffn)), *(s[None] for s in (*s_new, s_ffn)))
```
